```python
import math
import jax, jax.numpy as jnp
from jax import lax
import numpy as np

D_MODEL = 1024
BATCH = 8
SEQ = 2048
DEPTH = 4

CHUNK = 64
Q_BLOCK = 128
N_MIXERS = 4
ALPHA = (2 * DEPTH) ** 0.25
BETA = (8 * DEPTH) ** -0.25
LN_EPS = 1e-5
RMS_EPS = 1e-6
NEG_INF = -1e30

DIFF_HEAD_DIM = 64
DIFF_HEADS = D_MODEL // (2 * DIFF_HEAD_DIM)
DIFF_LAMBDA_STD = 0.1

CA_HEAD_DIM = 64
CA_HEADS = D_MODEL // CA_HEAD_DIM
CA_LEFT_CHUNKS = 8
CA_BAND = (CA_LEFT_CHUNKS + 1) * CHUNK
CA_REL_CLIP = 128

MLA_HEADS = D_MODEL // 64
MLA_NOPE_DIM = 64
MLA_ROPE_DIM = 32
MLA_V_DIM = 64
MLA_Q_RANK = 384
MLA_KV_RANK = 256
ROPE_THETA = 10000.0

GMLP_CHUNK = 128
GMLP_GROUPS = 8
GMLP_WIDTH = D_MODEL

D_FF = 2816
N_EXPERTS = 8
TOP_K = 2
D_FF_EXPERT = 3584

N_A = (DEPTH + 3) // 4
N_B = (DEPTH + 2) // 4
N_C = (DEPTH + 1) // 4
N_D = DEPTH // 4
N_DENSE = (DEPTH + 1) // 2
N_MOE = DEPTH // 2

kernel_name = "hybrid_chunk_causal_deepnorm_trunk"


def layer_norm(x, g, b):
    xf = x.astype(jnp.float32)
    mu = jnp.mean(xf, axis=-1, keepdims=True)
    xc = xf - mu
    var = jnp.mean(xc * xc, axis=-1, keepdims=True)
    return (xc * lax.rsqrt(var + LN_EPS) * g + b).astype(x.dtype)


def rms_norm(x, g, eps=RMS_EPS):
    xf = x.astype(jnp.float32)
    ms = jnp.mean(xf * xf, axis=-1, keepdims=True)
    return (xf * lax.rsqrt(ms + eps) * g).astype(x.dtype)


def rotary(x, pos):
    half = x.shape[-1] // 2
    inv_freq = ROPE_THETA ** (-jnp.arange(half, dtype=jnp.float32) / half)
    ang = pos.astype(jnp.float32)[:, None] * inv_freq[None, :]
    cos = jnp.cos(ang)[:, None, :]
    sin = jnp.sin(ang)[:, None, :]
    xf = x.astype(jnp.float32)
    x1, x2 = xf[..., :half], xf[..., half:]
    return jnp.concatenate([x1 * cos - x2 * sin, x2 * cos + x1 * sin], axis=-1).astype(x.dtype)


def chunk_causal_attention(q, k, v, coef, scale):
    seq = q.shape[1]
    coef = coef.astype(jnp.float32)
    outs = []
    for blk in range(seq // Q_BLOCK):
        q0 = blk * Q_BLOCK
        k_end = q0 + Q_BLOCK
        s = jnp.einsum('bqmhd,bkmhd->bmhqk', q[:, q0:k_end], k[:, :k_end],
                       preferred_element_type=jnp.float32) * scale
        q_chunk = (q0 + jnp.arange(Q_BLOCK)) // CHUNK
        k_chunk = jnp.arange(k_end) // CHUNK
        visible = k_chunk[None, :] <= q_chunk[:, None]
        p = jax.nn.softmax(jnp.where(visible, s, NEG_INF), axis=-1)
        p = jnp.einsum('m,bmhqk->bhqk', coef, p)
        outs.append(jnp.einsum('bhqk,bkhd->bqhd', p.astype(v.dtype), v[:, :k_end]))
    return jnp.concatenate(outs, axis=1)


def diff_attention(x, wq, wk, wv, lq1, lk1, lq2, lk2, sub_g, wo, lambda_init):
    b, s, _ = x.shape
    q = (x @ wq).reshape(b, s, DIFF_HEADS, 2, DIFF_HEAD_DIM).transpose(0, 1, 3, 2, 4)
    k = (x @ wk).reshape(b, s, DIFF_HEADS, 2, DIFF_HEAD_DIM).transpose(0, 1, 3, 2, 4)
    v = (x @ wv).reshape(b, s, DIFF_HEADS, 2 * DIFF_HEAD_DIM)
    lam = (jnp.exp(jnp.sum(lq1.astype(jnp.float32) * lk1.astype(jnp.float32)))
           - jnp.exp(jnp.sum(lq2.astype(jnp.float32) * lk2.astype(jnp.float32))) + lambda_init)
    coef = jnp.stack([jnp.ones_like(lam), -lam])
    o = chunk_causal_attention(q, k, v, coef, DIFF_HEAD_DIM ** -0.5)
    o = rms_norm(o, sub_g, eps=1e-5) * (1.0 - lambda_init)
    return o.reshape(b, s, D_MODEL) @ wo


def band_chunk_attention(x, w_qkv, rel_bias, wo):
    b, s, _ = x.shape
    nc = s // CHUNK
    qkv = (x @ w_qkv).reshape(b, s, 3, CA_HEADS, CA_HEAD_DIM)
    q, k, v = qkv[:, :, 0], qkv[:, :, 1], qkv[:, :, 2]
    pad = CA_LEFT_CHUNKS * CHUNK
    k_pad = jnp.pad(k, ((0, 0), (pad, 0), (0, 0), (0, 0)))
    v_pad = jnp.pad(v, ((0, 0), (pad, 0), (0, 0), (0, 0)))
    band = jnp.arange(nc)[:, None] * CHUNK + jnp.arange(CA_BAND)[None, :]
    k_band = k_pad[:, band]
    v_band = v_pad[:, band]
    qc = q.reshape(b, nc, CHUNK, CA_HEADS, CA_HEAD_DIM)
    sc = jnp.einsum('bcqhd,bckhd->bhcqk', qc, k_band,
                    preferred_element_type=jnp.float32) * (CA_HEAD_DIM ** -0.5)
    rel = jnp.arange(CHUNK)[:, None] - jnp.arange(CA_BAND)[None, :] + pad
    rel_idx = jnp.clip(rel, -CA_REL_CLIP, CA_REL_CLIP) + CA_REL_CLIP
    bias = rel_bias[:, rel_idx].astype(jnp.float32)
    valid = band >= pad
    sc = jnp.where(valid[None, None, :, None, :], sc + bias[None, :, None], NEG_INF)
    p = jax.nn.softmax(sc, axis=-1)
    o = jnp.einsum('bhcqk,bckhd->bcqhd', p.astype(v.dtype), v_band).reshape(b, s, D_MODEL)
    return o @ wo


def mla_attention(x, w_dq, q_norm_g, w_uq, w_dkv, kv_norm_g, w_ukv, wo, pos):
    b, s, _ = x.shape
    cq = rms_norm(x @ w_dq, q_norm_g)
    q = (cq @ w_uq).reshape(b, s, MLA_HEADS, MLA_NOPE_DIM + MLA_ROPE_DIM)
    q_nope, q_rope = q[..., :MLA_NOPE_DIM], rotary(q[..., MLA_NOPE_DIM:], pos)
    ckv = x @ w_dkv
    c_kv = rms_norm(ckv[..., :MLA_KV_RANK], kv_norm_g)
    k_rope = rotary(ckv[..., MLA_KV_RANK:][:, :, None, :], pos)
    kv = (c_kv @ w_ukv).reshape(b, s, MLA_HEADS, MLA_NOPE_DIM + MLA_V_DIM)
    k_nope, v = kv[..., :MLA_NOPE_DIM], kv[..., MLA_NOPE_DIM:]
    q_full = jnp.concatenate([q_nope, q_rope], axis=-1)[:, :, None]
    k_full = jnp.concatenate(
        [k_nope, jnp.broadcast_to(k_rope, (b, s, MLA_HEADS, MLA_ROPE_DIM))], axis=-1)[:, :, None]
    o = chunk_causal_attention(q_full, k_full, v, jnp.ones((1,), jnp.float32),
                               (MLA_NOPE_DIM + MLA_ROPE_DIM) ** -0.5)
    return o.reshape(b, s, MLA_HEADS * MLA_V_DIM) @ wo


def chunk_spatial_gating(x, w_in, v_norm_g, v_norm_b, w_s, b_s, w_out):
    b, s, _ = x.shape
    nc = s // GMLP_CHUNK
    h = jax.nn.gelu(x @ w_in)
    u, v = h[..., :GMLP_WIDTH], h[..., GMLP_WIDTH:]
    v = layer_norm(v, v_norm_g, v_norm_b)
    v = v.reshape(b, nc, GMLP_CHUNK, GMLP_GROUPS, GMLP_WIDTH // GMLP_GROUPS)
    sub = jnp.arange(GMLP_CHUNK) // CHUNK
    mask = sub[None, :] <= sub[:, None]
    w = jnp.where(mask[None], w_s, 0.0)
    mixed = jnp.einsum('gij,bcjgd->bcigd', w, v) + b_s.T[None, None, :, :, None]
    return (u * mixed.reshape(b, s, GMLP_WIDTH)) @ w_out


def swiglu(x, w1, w3, w2):
    return (jax.nn.silu(x @ w1) * (x @ w3)) @ w2


def moe_swiglu(x, w_router, w1, w3, w2):
    b, s, d = x.shape
    t = x.reshape(b * s, d)
    logits = (t @ w_router).astype(jnp.float32)
    top_val, top_idx = lax.top_k(logits, TOP_K)
    gates = jax.nn.softmax(top_val, axis=-1)
    combine = jnp.sum(jax.nn.one_hot(top_idx, N_EXPERTS, dtype=jnp.float32) * gates[..., None], axis=1)
    y = jnp.zeros_like(t)
    for e in range(N_EXPERTS):
        y = y + combine[:, e:e + 1].astype(t.dtype) * swiglu(t, w1[e], w3[e], w2[e])
    return y.reshape(b, s, d)


def setup_inputs(seed: int = 0) -> dict:
    key = jax.random.key(seed)
    keys = iter(jax.random.split(key, 48))

    def nrm(shape, scale):
        return jax.random.normal(next(keys), shape, jnp.float32) * scale

    def gain(shape):
        return 1.0 + nrm(shape, 0.02)

    d = D_MODEL
    return {
        "x": nrm((BATCH, SEQ, d), 1.0),
        "ln_mix_g": gain((DEPTH, d)),
        "ln_mix_b": nrm((DEPTH, d), 0.02),
        "ln_ffn_g": gain((DEPTH, d)),
        "ln_ffn_b": nrm((DEPTH, d), 0.02),
        "diff_wq": nrm((N_A, d, d), d ** -0.5),
        "diff_wk": nrm((N_A, d, d), d ** -0.5),
        "diff_wv": nrm((N_A, d, d), d ** -0.5),
        "diff_lq1": nrm((N_A, DIFF_HEAD_DIM), DIFF_LAMBDA_STD),
        "diff_lk1": nrm((N_A, DIFF_HEAD_DIM), DIFF_LAMBDA_STD),
        "diff_lq2": nrm((N_A, DIFF_HEAD_DIM), DIFF_LAMBDA_STD),
        "diff_lk2": nrm((N_A, DIFF_HEAD_DIM), DIFF_LAMBDA_STD),
        "diff_sub_g": gain((N_A, 2 * DIFF_HEAD_DIM)),
        "diff_wo": nrm((N_A, d, d), d ** -0.5 * BETA),
        "ca_w_qkv": nrm((N_B, d, 3 * d), d ** -0.5),
        "ca_rel_bias": nrm((N_B, CA_HEADS, 2 * CA_REL_CLIP + 1), 0.2),
        "ca_wo": nrm((N_B, d, d), d ** -0.5 * BETA),
        "mla_w_dq": nrm((N_C, d, MLA_Q_RANK), d ** -0.5),
        "mla_q_norm_g": gain((N_C, MLA_Q_RANK)),
        "mla_w_uq": nrm((N_C, MLA_Q_RANK, MLA_HEADS * (MLA_NOPE_DIM + MLA_ROPE_DIM)), MLA_Q_RANK ** -0.5),
        "mla_w_dkv": nrm((N_C, d, MLA_KV_RANK + MLA_ROPE_DIM), d ** -0.5),
        "mla_kv_norm_g": gain((N_C, MLA_KV_RANK)),
        "mla_w_ukv": nrm((N_C, MLA_KV_RANK, MLA_HEADS * (MLA_NOPE_DIM + MLA_V_DIM)), MLA_KV_RANK ** -0.5),
        "mla_wo": nrm((N_C, MLA_HEADS * MLA_V_DIM, d), (MLA_HEADS * MLA_V_DIM) ** -0.5 * BETA),
        "sg_w_in": nrm((N_D, d, 2 * GMLP_WIDTH), d ** -0.5),
        "sg_v_norm_g": gain((N_D, GMLP_WIDTH)),
        "sg_v_norm_b": nrm((N_D, GMLP_WIDTH), 0.02),
        "sg_w_s": nrm((N_D, GMLP_GROUPS, GMLP_CHUNK, GMLP_CHUNK), GMLP_CHUNK ** -0.5),
        "sg_b_s": 1.0 + nrm((N_D, GMLP_GROUPS, GMLP_CHUNK), 0.01),
        "sg_w_out": nrm((N_D, GMLP_WIDTH, d), GMLP_WIDTH ** -0.5 * BETA),
        "ffn_w1": nrm((N_DENSE, d, D_FF), d ** -0.5),
        "ffn_w3": nrm((N_DENSE, d, D_FF), d ** -0.5),
        "ffn_w2": nrm((N_DENSE, D_FF, d), D_FF ** -0.5 * BETA),
        "moe_w_router": nrm((N_MOE, d, N_EXPERTS), d ** -0.5),
        "moe_w1": nrm((N_MOE, N_EXPERTS, d, D_FF_EXPERT), d ** -0.5),
        "moe_w3": nrm((N_MOE, N_EXPERTS, d, D_FF_EXPERT), d ** -0.5),
        "moe_w2": nrm((N_MOE, N_EXPERTS, D_FF_EXPERT, d), D_FF_EXPERT ** -0.5 * BETA),
    }


def reference(x, ln_mix_g, ln_mix_b, ln_ffn_g, ln_ffn_b,
              diff_wq, diff_wk, diff_wv, diff_lq1, diff_lk1, diff_lq2, diff_lk2, diff_sub_g, diff_wo,
              ca_w_qkv, ca_rel_bias, ca_wo,
              mla_w_dq, mla_q_norm_g, mla_w_uq, mla_w_dkv, mla_kv_norm_g, mla_w_ukv, mla_wo,
              sg_w_in, sg_v_norm_g, sg_v_norm_b, sg_w_s, sg_b_s, sg_w_out,
              ffn_w1, ffn_w3, ffn_w2,
              moe_w_router, moe_w1, moe_w3, moe_w2):
    pos = jnp.arange(x.shape[1])
    h = x
    for i in range(DEPTH):
        kind, j = i % N_MIXERS, i // N_MIXERS
        if kind == 0:
            lambda_init = 0.8 - 0.6 * math.exp(-0.3 * i)
            m = diff_attention(h, diff_wq[j], diff_wk[j], diff_wv[j], diff_lq1[j], diff_lk1[j],
                               diff_lq2[j], diff_lk2[j], diff_sub_g[j], diff_wo[j], lambda_init)
        elif kind == 1:
            m = band_chunk_attention(h, ca_w_qkv[j], ca_rel_bias[j], ca_wo[j])
        elif kind == 2:
            m = mla_attention(h, mla_w_dq[j], mla_q_norm_g[j], mla_w_uq[j], mla_w_dkv[j],
                              mla_kv_norm_g[j], mla_w_ukv[j], mla_wo[j], pos)
        else:
            m = chunk_spatial_gating(h, sg_w_in[j], sg_v_norm_g[j], sg_v_norm_b[j],
                                     sg_w_s[j], sg_b_s[j], sg_w_out[j])
        h = layer_norm(ALPHA * h + m, ln_mix_g[i], ln_mix_b[i])
        if i % 2 == 0:
            f = swiglu(h, ffn_w1[i // 2], ffn_w3[i // 2], ffn_w2[i // 2])
        else:
            f = moe_swiglu(h, moe_w_router[i // 2], moe_w1[i // 2], moe_w3[i // 2], moe_w2[i // 2])
        h = layer_norm(ALPHA * h + f, ln_ffn_g[i], ln_ffn_b[i])
    return h
```

```python
import functools
import math

import jax
import jax.numpy as jnp
from jax import lax
from jax.experimental import pallas as pl
from jax.experimental.pallas import tpu as pltpu

F32 = jnp.float32
BF16 = jnp.bfloat16
I32 = jnp.int32

LANE = 128
VMEM_LIMIT = 56 * 1024 * 1024

CHUNK = 64
N_EXPERTS = 8
LN_EPS = 1e-5
RMS_EPS = 1e-6
NEG_INF = -1e30
ROPE_THETA = 10000.0
CA_LEFT = 8
CA_REL_CLIP = 128
ROPE_HALF = 16


def _cp(n_axes, vmem=VMEM_LIMIT):
    return pltpu.CompilerParams(dimension_semantics=("arbitrary",) * n_axes, vmem_limit_bytes=vmem)


def _resident(shape):
    nd = len(shape)
    return pl.BlockSpec(shape, lambda *_: (0,) * nd, pipeline_mode=pl.Buffered(1))


def _res_ln(h, m, g, b, alpha):
    z = alpha * h + m
    mu = jnp.mean(z, axis=-1, keepdims=True)
    zc = z - mu
    var = jnp.mean(zc * zc, axis=-1, keepdims=True)
    return zc * lax.rsqrt(var + LN_EPS) * g + b


def _rms(x, g, eps):
    ms = jnp.mean(x * x, axis=-1, keepdims=True)
    return x * lax.rsqrt(ms + eps) * g


def _linear_kernel(x_ref, w_ref, o_ref):
    x = x_ref[...].astype(BF16)
    o_ref[...] = jnp.dot(x, w_ref[...], preferred_element_type=F32).astype(o_ref.dtype)


def _linear(x, w, *, tm=512, out_dtype=BF16, name="linear"):
    m, k = x.shape
    n = w.shape[1]
    tm = min(tm, m)
    return pl.pallas_call(
        _linear_kernel,
        grid=(m // tm,),
        in_specs=[pl.BlockSpec((tm, k), lambda i: (i, 0)), _resident((k, n))],
        out_specs=pl.BlockSpec((tm, n), lambda i: (i, 0)),
        out_shape=jax.ShapeDtypeStruct((m, n), out_dtype),
        compiler_params=_cp(1),
        name=name,
    )(x, w)


def _proj_res_ln_kernel(a_ref, w_ref, h_ref, g_ref, b_ref, oh_ref, ohb_ref, *, alpha):
    m = jnp.dot(a_ref[...], w_ref[...], preferred_element_type=F32)
    y = _res_ln(h_ref[...], m, g_ref[...], b_ref[...], alpha)
    oh_ref[...] = y
    ohb_ref[...] = y.astype(BF16)


def _proj_res_ln(a, w, h, g, b, *, alpha, tm=512, name="proj_res_ln"):
    m, k = a.shape
    d = w.shape[1]
    tm = min(tm, m)
    row = lambda i: (i, 0)
    return pl.pallas_call(
        functools.partial(_proj_res_ln_kernel, alpha=alpha),
        grid=(m // tm,),
        in_specs=[pl.BlockSpec((tm, k), row), _resident((k, d)), pl.BlockSpec((tm, d), row),
                  _resident((1, d)), _resident((1, d))],
        out_specs=[pl.BlockSpec((tm, d), row), pl.BlockSpec((tm, d), row)],
        out_shape=[jax.ShapeDtypeStruct((m, d), F32), jax.ShapeDtypeStruct((m, d), BF16)],
        compiler_params=_cp(1),
        name=name,
    )(a, w, h, g, b)


def _ffn_kernel(x_ref, w1_ref, w3_ref, w2_ref, h_ref, g_ref, b_ref, oh_ref, ohb_ref, acc_ref, *, alpha, fc):
    x = x_ref[...]
    f = w1_ref.shape[1]
    for c in range(f // fc):
        sl = slice(c * fc, (c + 1) * fc)
        a = jnp.dot(x, w1_ref[:, sl], preferred_element_type=F32)
        b3 = jnp.dot(x, w3_ref[:, sl], preferred_element_type=F32)
        gated = (jax.nn.silu(a) * b3).astype(BF16)
        part = jnp.dot(gated, w2_ref[sl, :], preferred_element_type=F32)
        if c == 0:
            acc_ref[...] = part
        else:
            acc_ref[...] += part
    y = _res_ln(h_ref[...], acc_ref[...], g_ref[...], b_ref[...], alpha)
    oh_ref[...] = y
    ohb_ref[...] = y.astype(BF16)


def _ffn(xb, w1, w3, w2, h, g, b, *, alpha, tm=512, fc=256):
    m, d = xb.shape
    f = w1.shape[1]
    tm = min(tm, m)
    row = lambda i: (i, 0)
    return pl.pallas_call(
        functools.partial(_ffn_kernel, alpha=alpha, fc=fc),
        grid=(m // tm,),
        in_specs=[pl.BlockSpec((tm, d), row), _resident((d, f)), _resident((d, f)), _resident((f, d)),
                  pl.BlockSpec((tm, d), row), _resident((1, d)), _resident((1, d))],
        out_specs=[pl.BlockSpec((tm, d), row), pl.BlockSpec((tm, d), row)],
        out_shape=[jax.ShapeDtypeStruct((m, d), F32), jax.ShapeDtypeStruct((m, d), BF16)],
        scratch_shapes=[pltpu.VMEM((tm, d), F32)],
        compiler_params=_cp(1),
        name="ffn_swiglu",
    )(xb, w1, w3, w2, h, g, b)


def _flash_kernel(*refs, mode, tq, lambda_init):
    if mode == "diff":
        lq1_ref, lk1_ref, lq2_ref, lk2_ref, subg_ref, q_ref, k_ref, v_ref, o_ref = refs
    else:
        q_ref, k_ref, v_ref, o_ref = refs
    qi = pl.program_id(2)
    low = lax.broadcasted_iota(I32, (tq, LANE), 1) < 64
    if mode == "diff":
        q = q_ref[...] * jnp.asarray(CHUNK ** -0.5, BF16)
        zero = jnp.zeros_like(q)
        qa = jnp.where(low, q, zero)
        qb = jnp.where(low, zero, q)
    else:
        qa = q_ref[:, :LANE]
        qb = q_ref[:, LANE:]
    dn = (((1,), (1,)), ((), ()))

    def update(s, vb, m, l, acc):
        m_new = jnp.maximum(m, jnp.max(s, axis=-1, keepdims=True))
        scale = jnp.exp(m - m_new)
        p = jnp.exp(s - m_new)
        l = scale * l + jnp.sum(p, axis=-1, keepdims=True)
        acc = scale * acc + jnp.dot(p.astype(BF16), vb, preferred_element_type=F32)
        return m_new, l, acc

    def block(j, carry, masked):
        ma, la, aa, mb, lb, ab = carry
        start = pl.multiple_of(j * tq, tq)
        kb = k_ref[pl.ds(start, tq), :]
        vb = v_ref[pl.ds(start, tq), :]
        if mode == "diff":
            ka, kb2 = kb, kb
        else:
            ka, kb2 = kb[:, :LANE], kb[:, LANE:]
        sa = lax.dot_general(qa, ka, dn, preferred_element_type=F32)
        sb = lax.dot_general(qb, kb2, dn, preferred_element_type=F32)
        if masked:
            r = lax.broadcasted_iota(I32, (tq, tq), 0) >> 6
            c = lax.broadcasted_iota(I32, (tq, tq), 1) >> 6
            vis = c <= r
            sa = jnp.where(vis, sa, NEG_INF)
            sb = jnp.where(vis, sb, NEG_INF)
        ma, la, aa = update(sa, vb, ma, la, aa)
        mb, lb, ab = update(sb, vb, mb, lb, ab)
        return ma, la, aa, mb, lb, ab

    m0 = jnp.full((tq, 1), NEG_INF, F32)
    l0 = jnp.zeros((tq, 1), F32)
    a0 = jnp.zeros((tq, LANE), F32)
    carry = lax.fori_loop(0, qi, lambda j, c: block(j, c, False), (m0, l0, a0, m0, l0, a0))
    ma, la, aa, mb, lb, ab = block(qi, carry, True)

    if mode == "diff":
        lam = (jnp.exp(jnp.sum(lq1_ref[...] * lk1_ref[...], axis=-1, keepdims=True))
               - jnp.exp(jnp.sum(lq2_ref[...] * lk2_ref[...], axis=-1, keepdims=True)) + lambda_init)
        o = aa / la - lam * (ab / lb)
        o = _rms(o, subg_ref[...], 1e-5) * (1.0 - lambda_init)
    else:
        o = jnp.where(low, aa / la, ab / lb)
    o_ref[...] = o.astype(o_ref.dtype)


def _flash(mode, q_arr, k_arr, v_arr, *, batch, seq, groups, q_blk0, k_blk0, v_blk0, extra=(), lambda_init=0.0,
           tq=256):
    qw = LANE if mode == "diff" else 2 * LANE
    tq = min(tq, seq)
    nq = seq // tq
    n = batch * seq
    in_specs = [pl.BlockSpec(e.shape, lambda b, g, i: (0, 0)) for e in extra]
    in_specs += [
        pl.BlockSpec((tq, qw), lambda b, g, i: (b * nq + i, q_blk0 + g)),
        pl.BlockSpec((seq, qw), lambda b, g, i: (b, k_blk0 + g)),
        pl.BlockSpec((seq, LANE), lambda b, g, i: (b, v_blk0 + g)),
    ]
    return pl.pallas_call(
        functools.partial(_flash_kernel, mode=mode, tq=tq, lambda_init=lambda_init),
        grid=(batch, groups, nq),
        in_specs=in_specs,
        out_specs=pl.BlockSpec((tq, LANE), lambda b, g, i: (b * nq + i, g)),
        out_shape=jax.ShapeDtypeStruct((n, groups * LANE), BF16),
        compiler_params=_cp(3),
        name="flash_" + mode,
    )(*extra, q_arr, k_arr, v_arr)


BAND_TQ = 2 * CHUNK
BAND_W = (CA_LEFT + 2) * CHUNK
BAND_PAD = CA_LEFT * CHUNK


def _band_kernel(q_ref, k_ref, v_ref, bias_ref, o_ref, kpad_ref, vpad_ref, *, seq):
    t = pl.program_id(2)

    @pl.when(t == 0)
    def _():
        zeros = jnp.zeros((BAND_PAD, LANE), BF16)
        kpad_ref[0:BAND_PAD, :] = zeros
        vpad_ref[0:BAND_PAD, :] = zeros
        kpad_ref[BAND_PAD:BAND_PAD + seq, :] = k_ref[...]
        vpad_ref[BAND_PAD:BAND_PAD + seq, :] = v_ref[...]

    start = pl.multiple_of(t * BAND_TQ, BAND_TQ)
    kw = kpad_ref[pl.ds(start, BAND_W), :]
    vw = vpad_ref[pl.ds(start, BAND_W), :]
    low = lax.broadcasted_iota(I32, (BAND_TQ, LANE), 1) < 64
    q = q_ref[...] * jnp.asarray(CHUNK ** -0.5, BF16)
    zero = jnp.zeros_like(q)
    col = lax.broadcasted_iota(I32, (BAND_TQ, BAND_W), 1)
    invalid = col + t * BAND_TQ < BAND_PAD
    dn = (((1,), (1,)), ((), ()))
    outs = []
    for hh, qh in enumerate((jnp.where(low, q, zero), jnp.where(low, zero, q))):
        s = lax.dot_general(qh, kw, dn, preferred_element_type=F32) + bias_ref[0, hh]
        s = jnp.where(invalid, NEG_INF, s)
        m = jnp.max(s, axis=-1, keepdims=True)
        p = jnp.exp(s - m)
        l = jnp.sum(p, axis=-1, keepdims=True)
        outs.append(jnp.dot(p.astype(BF16), vw, preferred_element_type=F32) / l)
    o_ref[...] = jnp.where(low, outs[0], outs[1]).astype(o_ref.dtype)


def _band_attention(qkv, bias, *, batch, seq, pairs):
    n = batch * seq
    nt = seq // BAND_TQ
    return pl.pallas_call(
        functools.partial(_band_kernel, seq=seq),
        grid=(pairs, batch, nt),
        in_specs=[
            pl.BlockSpec((BAND_TQ, LANE), lambda g, b, t: (b * nt + t, g)),
            pl.BlockSpec((seq, LANE), lambda g, b, t: (b, pairs + g)),
            pl.BlockSpec((seq, LANE), lambda g, b, t: (b, 2 * pairs + g)),
            pl.BlockSpec((1, 2, BAND_TQ, BAND_W), lambda g, b, t: (g, 0, 0, 0)),
        ],
        out_specs=pl.BlockSpec((BAND_TQ, LANE), lambda g, b, t: (b * nt + t, g)),
        out_shape=jax.ShapeDtypeStruct((n, pairs * LANE), BF16),
        scratch_shapes=[pltpu.VMEM((seq + BAND_PAD, LANE), BF16), pltpu.VMEM((seq + BAND_PAD, LANE), BF16)],
        compiler_params=_cp(3),
        name="band_attention",
    )(qkv, qkv, qkv, bias)


def _band_bias_tiles(rel_bias):
    i = jnp.arange(BAND_TQ)[:, None]
    j = jnp.arange(BAND_W)[None, :]
    idx = jnp.clip(i - j + BAND_PAD, -CA_REL_CLIP, CA_REL_CLIP) + CA_REL_CLIP
    ci, cj = i // CHUNK, j // CHUNK
    vis = (cj >= ci) & (cj <= ci + CA_LEFT)
    bias = jnp.where(vis[None], rel_bias.astype(F32)[:, idx], NEG_INF)
    return bias.reshape(rel_bias.shape[0] // 2, 2, BAND_TQ, BAND_W)


def _mla_cq_kernel(x_ref, w_ref, g_ref, o_ref):
    c = jnp.dot(x_ref[...], w_ref[...], preferred_element_type=F32)
    o_ref[...] = _rms(c, g_ref[...], RMS_EPS).astype(o_ref.dtype)


def _mla_q_kernel(c_ref, wa_ref, wb_ref, ct_ref, st_ref, o_ref, *, scale, heads):
    c = c_ref[...]
    a = jnp.dot(c, wa_ref[...], preferred_element_type=F32)
    b = jnp.dot(c, wb_ref[...], preferred_element_type=F32)
    ct = ct_ref[...] * scale
    st = st_ref[...] * scale
    for h in range(heads):
        sl = slice(h * LANE, (h + 1) * LANE)
        o_ref[:, sl] = (a[:, sl] * ct + b[:, sl] * st).astype(o_ref.dtype)


def _mla_ckv_kernel(x_ref, w_ref, g_ref, ct_ref, st_ref, o_ref, *, rank):
    y = jnp.dot(x_ref[...], w_ref[...], preferred_element_type=F32)
    o_ref[:, :rank] = _rms(y[:, :rank], g_ref[...], RMS_EPS).astype(o_ref.dtype)
    kr = y[:, rank:rank + LANE] * ct_ref[...] + y[:, rank + LANE:rank + 2 * LANE] * st_ref[...]
    o_ref[:, rank:] = kr.astype(o_ref.dtype)


GMLP_CHUNK = 128


def _sg_in_kernel(x_ref, w_ref, g_ref, b_ref, u_ref, v_ref):
    width = u_ref.shape[1]
    hh = jax.nn.gelu(jnp.dot(x_ref[...], w_ref[...], preferred_element_type=F32))
    u_ref[...] = hh[:, :width].astype(u_ref.dtype)
    v = hh[:, width:]
    mu = jnp.mean(v, axis=-1, keepdims=True)
    vc = v - mu
    var = jnp.mean(vc * vc, axis=-1, keepdims=True)
    v_ref[...] = (vc * lax.rsqrt(var + LN_EPS) * g_ref[...] + b_ref[...]).astype(v_ref.dtype)


def _sg_mix_kernel(u_ref, v_ref, ws_ref, bs_ref, wo_ref, h_ref, g_ref, b_ref, oh_ref, ohb_ref, gated_ref, *,
                   alpha, groups):
    tm = u_ref.shape[0]
    r = lax.broadcasted_iota(I32, (GMLP_CHUNK, GMLP_CHUNK), 0) >> 6
    c = lax.broadcasted_iota(I32, (GMLP_CHUNK, GMLP_CHUNK), 1) >> 6
    vis = c <= r
    for gi in range(groups):
        w = jnp.where(vis, ws_ref[gi], 0.0).astype(BF16)
        cs = slice(gi * LANE, (gi + 1) * LANE)
        for ch in range(tm // GMLP_CHUNK):
            rs = slice(ch * GMLP_CHUNK, (ch + 1) * GMLP_CHUNK)
            mixed = jnp.dot(w, v_ref[rs, cs], preferred_element_type=F32) + bs_ref[:, cs]
            gated_ref[rs, cs] = (u_ref[rs, cs].astype(F32) * mixed).astype(BF16)
    m = jnp.dot(gated_ref[...], wo_ref[...], preferred_element_type=F32)
    y = _res_ln(h_ref[...], m, g_ref[...], b_ref[...], alpha)
    oh_ref[...] = y
    ohb_ref[...] = y.astype(BF16)


INFO_IDX, INFO_RANK, INFO_GATE = 0, 2, 4


def _router_kernel(h_ref, w_ref, info_ref, cnt_ref, run_ref):
    i = pl.program_id(0)
    tr = h_ref.shape[0]

    @pl.when(i == 0)
    def _():
        run_ref[...] = jnp.zeros_like(run_ref)

    x = h_ref[...]
    x1 = x.astype(BF16)
    r1 = x - x1.astype(F32)
    x2 = r1.astype(BF16)
    x3 = (r1 - x2.astype(F32)).astype(BF16)
    w = w_ref[...]
    lane = lax.broadcasted_iota(I32, (tr, LANE), 1)
    t = (jnp.dot(x1, w, preferred_element_type=F32)
         + jnp.where(lane < 2 * N_EXPERTS, jnp.dot(x2, w, preferred_element_type=F32), 0.0)
         + jnp.where(lane < N_EXPERTS, jnp.dot(x3, w, preferred_element_type=F32), 0.0))
    logits = t + pltpu.roll(t, LANE - N_EXPERTS, 1) + pltpu.roll(t, LANE - 2 * N_EXPERTS, 1)
    big = jnp.float32(3e38)
    logits = jnp.where(lane < N_EXPERTS, logits, -big)

    lane_f = lane.astype(F32)
    v1 = jnp.max(logits, axis=-1, keepdims=True)
    i1 = jnp.min(jnp.where(logits == v1, lane_f, float(LANE)), axis=-1, keepdims=True)
    rest = jnp.where(lane_f == i1, -big, logits)
    v2 = jnp.max(rest, axis=-1, keepdims=True)
    i2 = jnp.min(jnp.where(rest == v2, lane_f, float(LANE)), axis=-1, keepdims=True)
    e = jnp.exp(v2 - v1)
    g1 = 1.0 / (1.0 + e)
    g2 = e / (1.0 + e)

    oh1 = jnp.where(lane_f == i1, 1.0, 0.0)
    oh2 = jnp.where(lane_f == i2, 1.0, 0.0)
    oh = oh1 + oh2
    rr = lax.broadcasted_iota(I32, (tr, tr), 0)
    cc = lax.broadcasted_iota(I32, (tr, tr), 1)
    before = jnp.where(cc < rr, 1.0, 0.0).astype(BF16)
    prior = jnp.dot(before, oh.astype(BF16), preferred_element_type=F32) + run_ref[...]
    rank1 = jnp.sum(prior * oh1, axis=-1, keepdims=True)
    rank2 = jnp.sum(prior * oh2, axis=-1, keepdims=True)
    run_ref[...] += jnp.sum(oh, axis=0, keepdims=True)
    cnt_ref[...] = run_ref[...]

    info = jnp.where(lane == INFO_IDX, i1, 0.0)
    info = jnp.where(lane == INFO_IDX + 1, i2, info)
    info = jnp.where(lane == INFO_RANK, rank1, info)
    info = jnp.where(lane == INFO_RANK + 1, rank2, info)
    info = jnp.where(lane == INFO_GATE, g1, info)
    info = jnp.where(lane == INFO_GATE + 1, g2, info)
    info_ref[...] = info


def _dispatch_kernel(pos_ref, h_ref, init_ref, xs_ref, sem):
    del init_ref
    td = h_ref.shape[0]

    def copy(r, p):
        return pltpu.make_async_copy(h_ref.at[pl.ds(r, 1), :], xs_ref.at[pl.ds(p, 1), :], sem)

    def start(r, carry):
        copy(r, pos_ref[0, 0, 2 * r]).start()
        copy(r, pos_ref[0, 0, 2 * r + 1]).start()
        return carry

    def wait(r, carry):
        copy(r, pos_ref[0, 0, 2 * r]).wait()
        copy(r, pos_ref[0, 0, 2 * r + 1]).wait()
        return carry

    lax.fori_loop(0, td, start, 0)
    lax.fori_loop(0, td, wait, 0)


def _experts_kernel(te_ref, nu_ref, x_ref, w1_ref, w3_ref, w2_ref, o_ref, xb_ref, acc_ref, *, fc):
    i = pl.program_id(0)
    j = pl.program_id(1)
    nj = pl.num_programs(1)
    del te_ref

    @pl.when(i < nu_ref[0])
    def _():
        @pl.when(j == 0)
        def _():
            xb_ref[...] = x_ref[...].astype(BF16)

        x = xb_ref[...]
        tf = w1_ref.shape[1]
        part = None
        for c in range(tf // fc):
            sl = slice(c * fc, (c + 1) * fc)
            a = jnp.dot(x, w1_ref[:, sl], preferred_element_type=F32)
            b3 = jnp.dot(x, w3_ref[:, sl], preferred_element_type=F32)
            gated = (jax.nn.silu(a) * b3).astype(BF16)
            d = jnp.dot(gated, w2_ref[sl, :], preferred_element_type=F32)
            part = d if part is None else part + d

        @pl.when(j == 0)
        def _():
            acc_ref[...] = part

        @pl.when(j > 0)
        def _():
            acc_ref[...] += part

        @pl.when(j == nj - 1)
        def _():
            o_ref[...] = acc_ref[...]

    @pl.when((i >= nu_ref[0]) & (j == nj - 1))
    def _():
        o_ref[...] = jnp.zeros_like(o_ref)


def _combine_kernel(pos_ref, info_ref, h_ref, g_ref, b_ref, ys_ref, oh_ref, ohb_ref, buf_ref, sem, *, alpha):
    tc = h_ref.shape[0]

    def copy(r, k):
        p = pos_ref[0, 0, 2 * r + k]
        return pltpu.make_async_copy(ys_ref.at[pl.ds(p, 1), :], buf_ref.at[k, pl.ds(r, 1), :], sem)

    def start(r, carry):
        copy(r, 0).start()
        copy(r, 1).start()
        return carry

    def wait(r, carry):
        copy(r, 0).wait()
        copy(r, 1).wait()
        return carry

    lax.fori_loop(0, tc, start, 0)
    lax.fori_loop(0, tc, wait, 0)
    info = info_ref[...]
    g1 = info[:, INFO_GATE:INFO_GATE + 1]
    g2 = info[:, INFO_GATE + 1:INFO_GATE + 2]
    m = g1 * buf_ref[0] + g2 * buf_ref[1]
    y = _res_ln(h_ref[...], m, g_ref[...], b_ref[...], alpha)
    oh_ref[...] = y
    ohb_ref[...] = y.astype(BF16)


def _split3(w):
    hi = w.astype(BF16)
    r = w - hi.astype(F32)
    mid = r.astype(BF16)
    lo = (r - mid.astype(F32)).astype(BF16)
    return hi, mid, lo


def _moe(h, w_router, w1, w3, w2, g, b, *, alpha, tm=512, tf=1792, fc=256, td=256):
    n, d = h.shape
    f = w1.shape[2]
    tm = min(tm, n)
    td = min(td, n)
    tf = min(tf, f)
    row = lambda i: (i, 0)

    hi, mid, lo = _split3(w_router.astype(F32))
    wr = jnp.concatenate([hi, mid, lo, jnp.zeros((d, LANE - 3 * N_EXPERTS), BF16)], axis=1)
    tr = min(512, n)
    info, cnt = pl.pallas_call(
        _router_kernel,
        grid=(n // tr,),
        in_specs=[pl.BlockSpec((tr, d), row), _resident((d, LANE))],
        out_specs=[pl.BlockSpec((tr, LANE), row), pl.BlockSpec((1, LANE), lambda i: (0, 0))],
        out_shape=[jax.ShapeDtypeStruct((n, LANE), F32), jax.ShapeDtypeStruct((1, LANE), F32)],
        scratch_shapes=[pltpu.VMEM((1, LANE), F32)],
        compiler_params=_cp(1),
        name="moe_router",
    )(h, wr)

    idx = info[:, INFO_IDX:INFO_IDX + 2].astype(I32)
    rank = info[:, INFO_RANK:INFO_RANK + 2].astype(I32)
    counts = cnt[0, :N_EXPERTS].astype(I32)
    padded = ((counts + tm - 1) // tm) * tm
    ends = jnp.cumsum(padded)
    starts = ends - padded
    pos = starts[idx] + rank
    n_tiles = (2 * n) // tm + N_EXPERTS
    n_used = (ends[-1] // tm).astype(I32)
    tile_start = jnp.arange(n_tiles, dtype=I32) * tm
    tile_e = jnp.sum(tile_start[:, None] >= ends[None, :], axis=1).astype(I32)
    tile_e = jnp.minimum(tile_e, tile_e[jnp.maximum(n_used - 1, 0)])
    rows = n_tiles * tm
    pos_blocks = pos.reshape(n // td, 1, 2 * td)

    xs = pl.pallas_call(
        _dispatch_kernel,
        grid=(n // td,),
        in_specs=[pl.BlockSpec((1, 1, 2 * td), lambda i: (i, 0, 0), memory_space=pltpu.SMEM),
                  pl.BlockSpec((td, d), row),
                  pl.BlockSpec(memory_space=pl.ANY)],
        out_specs=pl.BlockSpec(memory_space=pl.ANY),
        out_shape=jax.ShapeDtypeStruct((rows, d), F32),
        scratch_shapes=[pltpu.SemaphoreType.DMA(())],
        input_output_aliases={2: 0},
        compiler_params=_cp(1),
        name="moe_dispatch",
    )(pos_blocks, h, jnp.zeros((rows, d), F32))

    nj = f // tf

    def x_map(i, j, te, nu):
        return (jnp.minimum(i, nu[0] - 1), 0)

    def w13_map(i, j, te, nu):
        return (te[i], 0, jnp.where(i < nu[0], j, nj - 1))

    def w2_map(i, j, te, nu):
        return (te[i], jnp.where(i < nu[0], j, nj - 1), 0)

    ys = pl.pallas_call(
        functools.partial(_experts_kernel, fc=fc),
        grid_spec=pltpu.PrefetchScalarGridSpec(
            num_scalar_prefetch=2,
            grid=(n_tiles, nj),
            in_specs=[pl.BlockSpec((tm, d), x_map),
                      pl.BlockSpec((None, d, tf), w13_map),
                      pl.BlockSpec((None, d, tf), w13_map),
                      pl.BlockSpec((None, tf, d), w2_map)],
            out_specs=pl.BlockSpec((tm, d), lambda i, j, te, nu: (i, 0)),
            scratch_shapes=[pltpu.VMEM((tm, d), BF16), pltpu.VMEM((tm, d), F32)],
        ),
        out_shape=jax.ShapeDtypeStruct((rows, d), F32),
        compiler_params=_cp(2),
        name="moe_experts",
    )(tile_e, n_used.reshape(1), xs, w1, w3, w2)

    tc = td
    return pl.pallas_call(
        functools.partial(_combine_kernel, alpha=alpha),
        grid=(n // tc,),
        in_specs=[pl.BlockSpec((1, 1, 2 * tc), lambda i: (i, 0, 0), memory_space=pltpu.SMEM),
                  pl.BlockSpec((tc, LANE), row),
                  pl.BlockSpec((tc, d), row),
                  _resident((1, d)), _resident((1, d)),
                  pl.BlockSpec(memory_space=pl.ANY)],
        out_specs=[pl.BlockSpec((tc, d), row), pl.BlockSpec((tc, d), row)],
        out_shape=[jax.ShapeDtypeStruct((n, d), F32), jax.ShapeDtypeStruct((n, d), BF16)],
        scratch_shapes=[pltpu.VMEM((2, tc, d), F32), pltpu.SemaphoreType.DMA(())],
        compiler_params=_cp(1),
        name="moe_combine",
    )(pos_blocks, info, h, g, b, ys)


def _diff_mixer(h, hb, wq, wk, wv, lq1, lk1, lq2, lk2, sub_g, wo, g, b, *, batch, seq, alpha, lambda_init):
    d = wq.shape[0]
    heads = d // LANE
    qkv = _linear(hb, jnp.concatenate([wq, wk, wv], axis=1).astype(BF16), name="diff_qkv")
    extra = tuple(a.reshape(1, -1).astype(F32) for a in (lq1, lk1, lq2, lk2, sub_g))
    o = _flash("diff", qkv, qkv, qkv, batch=batch, seq=seq, groups=heads, q_blk0=0, k_blk0=heads, v_blk0=2 * heads,
               extra=extra, lambda_init=lambda_init)
    return _proj_res_ln(o, wo.astype(BF16), h, g, b, alpha=alpha, name="diff_out")


def _band_mixer(h, hb, w_qkv, rel_bias, wo, g, b, *, batch, seq, alpha):
    d = w_qkv.shape[0]
    qkv = _linear(hb, w_qkv.astype(BF16), name="band_qkv")
    o = _band_attention(qkv, _band_bias_tiles(rel_bias), batch=batch, seq=seq, pairs=d // LANE)
    return _proj_res_ln(o, wo.astype(BF16), h, g, b, alpha=alpha, name="band_out")


def _mla_mixer(h, hb, w_dq, q_g, w_uq, w_dkv, kv_g, w_ukv, wo, g, b, *, batch, seq, alpha, tm=512):
    n, d = hb.shape
    q_rank = w_dq.shape[1]
    kv_rank = kv_g.shape[-1]
    heads = wo.shape[0] // CHUNK
    nope, rope = 64, 2 * ROPE_HALF
    tm = min(tm, seq)
    row = lambda i: (i, 0)
    tab = lambda i: (i % (seq // tm), 0)

    inv_freq = ROPE_THETA ** (-jnp.arange(ROPE_HALF, dtype=F32) / ROPE_HALF)
    ang = jnp.arange(seq).astype(F32)[:, None] * inv_freq[None, :]
    cos, sin = jnp.cos(ang), jnp.sin(ang)
    z = lambda w: jnp.zeros((seq, w), F32)
    ct_q = jnp.concatenate([jnp.ones((seq, nope), F32), cos, cos, z(LANE - nope - rope)], axis=1)
    st_q = jnp.concatenate([z(nope), sin, sin, z(LANE - nope - rope)], axis=1)
    ct_k = jnp.concatenate([cos, cos, z(LANE - rope)], axis=1)
    st_k = jnp.concatenate([sin, sin, z(LANE - rope)], axis=1)

    wq3 = w_uq.reshape(q_rank, heads, nope + rope)
    qn, q1, q2 = wq3[..., :nope], wq3[..., nope:nope + ROPE_HALF], wq3[..., nope + ROPE_HALF:]
    zq = lambda w: jnp.zeros((q_rank, heads, w), w_uq.dtype)
    wa = jnp.concatenate([qn, q1, q2, zq(LANE - nope - rope)], axis=2).reshape(q_rank, heads * LANE)
    wb = jnp.concatenate([zq(nope), -q2, q1, zq(LANE - nope - rope)], axis=2).reshape(q_rank, heads * LANE)

    k1, k2 = w_dkv[:, kv_rank:kv_rank + ROPE_HALF], w_dkv[:, kv_rank + ROPE_HALF:]
    zd = lambda w: jnp.zeros((d, w), w_dkv.dtype)
    wd = jnp.concatenate([w_dkv[:, :kv_rank], k1, k2, zd(LANE - rope), -k2, k1, zd(LANE - rope)], axis=1)

    wkv3 = w_ukv.reshape(kv_rank, heads, nope + CHUNK)
    zk = lambda r, w: jnp.zeros((r, heads, w), w_ukv.dtype)
    k_top = jnp.concatenate([wkv3[..., :nope], zk(kv_rank, LANE - nope)], axis=2).reshape(kv_rank, heads * LANE)
    eye = jnp.broadcast_to(jnp.eye(rope, dtype=w_ukv.dtype)[:, None, :], (rope, heads, rope))
    k_rope = jnp.concatenate([zk(rope, nope), eye, zk(rope, LANE - nope - rope)], axis=2).reshape(rope, heads * LANE)
    v_top = wkv3[..., nope:].reshape(kv_rank, heads * CHUNK)
    wkv = jnp.concatenate([
        jnp.concatenate([k_top, v_top], axis=1),
        jnp.concatenate([k_rope, jnp.zeros((rope, heads * CHUNK), w_ukv.dtype)], axis=1),
        jnp.zeros((LANE - rope, heads * (LANE + CHUNK)), w_ukv.dtype)], axis=0)

    cq = pl.pallas_call(
        _mla_cq_kernel,
        grid=(n // tm,),
        in_specs=[pl.BlockSpec((tm, d), row), _resident((d, q_rank)), _resident((1, q_rank))],
        out_specs=pl.BlockSpec((tm, q_rank), row),
        out_shape=jax.ShapeDtypeStruct((n, q_rank), BF16),
        compiler_params=_cp(1),
        name="mla_cq",
    )(hb, w_dq.astype(BF16), q_g.reshape(1, -1))

    q = pl.pallas_call(
        functools.partial(_mla_q_kernel, scale=float((nope + rope) ** -0.5), heads=heads),
        grid=(n // tm,),
        in_specs=[pl.BlockSpec((tm, q_rank), row), _resident((q_rank, heads * LANE)),
                  _resident((q_rank, heads * LANE)), pl.BlockSpec((tm, LANE), tab), pl.BlockSpec((tm, LANE), tab)],
        out_specs=pl.BlockSpec((tm, heads * LANE), row),
        out_shape=jax.ShapeDtypeStruct((n, heads * LANE), BF16),
        compiler_params=_cp(1),
        name="mla_q",
    )(cq, wa.astype(BF16), wb.astype(BF16), ct_q, st_q)

    ckr = pl.pallas_call(
        functools.partial(_mla_ckv_kernel, rank=kv_rank),
        grid=(n // tm,),
        in_specs=[pl.BlockSpec((tm, d), row), _resident((d, kv_rank + 2 * LANE)), _resident((1, kv_rank)),
                  pl.BlockSpec((tm, LANE), tab), pl.BlockSpec((tm, LANE), tab)],
        out_specs=pl.BlockSpec((tm, kv_rank + LANE), row),
        out_shape=jax.ShapeDtypeStruct((n, kv_rank + LANE), BF16),
        compiler_params=_cp(1),
        name="mla_ckv",
    )(hb, wd.astype(BF16), kv_g.reshape(1, -1), ct_k, st_k)

    kv = _linear(ckr, wkv.astype(BF16), name="mla_kv")
    o = _flash("mla", q, kv, kv, batch=batch, seq=seq, groups=heads // 2, q_blk0=0, k_blk0=0, v_blk0=heads)
    return _proj_res_ln(o, wo.astype(BF16), h, g, b, alpha=alpha, name="mla_out")


def _sg_mixer(h, hb, w_in, vg, vb, w_s, b_s, w_out, g, b, *, alpha, tm=512):
    n, d = hb.shape
    width = w_out.shape[0]
    groups = w_s.shape[0]
    tm = min(tm, n)
    row = lambda i: (i, 0)
    u, v = pl.pallas_call(
        _sg_in_kernel,
        grid=(n // tm,),
        in_specs=[pl.BlockSpec((tm, d), row), _resident((d, 2 * width)), _resident((1, width)),
                  _resident((1, width))],
        out_specs=[pl.BlockSpec((tm, width), row), pl.BlockSpec((tm, width), row)],
        out_shape=[jax.ShapeDtypeStruct((n, width), BF16), jax.ShapeDtypeStruct((n, width), BF16)],
        compiler_params=_cp(1),
        name="sg_in",
    )(hb, w_in.astype(BF16), vg.reshape(1, -1), vb.reshape(1, -1))
    bs_full = jnp.repeat(b_s.T.astype(F32), width // groups, axis=1)
    return pl.pallas_call(
        functools.partial(_sg_mix_kernel, alpha=alpha, groups=groups),
        grid=(n // tm,),
        in_specs=[pl.BlockSpec((tm, width), row), pl.BlockSpec((tm, width), row),
                  _resident((groups, GMLP_CHUNK, GMLP_CHUNK)), _resident((GMLP_CHUNK, width)),
                  _resident((width, d)), pl.BlockSpec((tm, d), row), _resident((1, d)), _resident((1, d))],
        out_specs=[pl.BlockSpec((tm, d), row), pl.BlockSpec((tm, d), row)],
        out_shape=[jax.ShapeDtypeStruct((n, d), F32), jax.ShapeDtypeStruct((n, d), BF16)],
        scratch_shapes=[pltpu.VMEM((tm, width), BF16)],
        compiler_params=_cp(1),
        name="sg_mix",
    )(u, v, w_s.astype(F32), bs_full, w_out.astype(BF16), h, g, b)


def kernel(x, ln_mix_g, ln_mix_b, ln_ffn_g, ln_ffn_b, diff_wq, diff_wk, diff_wv, diff_lq1, diff_lk1, diff_lq2, diff_lk2, diff_sub_g, diff_wo, ca_w_qkv, ca_rel_bias, ca_wo, mla_w_dq, mla_q_norm_g, mla_w_uq, mla_w_dkv, mla_kv_norm_g, mla_w_ukv, mla_wo, sg_w_in, sg_v_norm_g, sg_v_norm_b, sg_w_s, sg_b_s, sg_w_out, ffn_w1, ffn_w3, ffn_w2, moe_w_router, moe_w1, moe_w3, moe_w2):
    batch, seq, d = x.shape
    depth = ln_mix_g.shape[0]
    alpha = float((2 * depth) ** 0.25)
    h = x.reshape(batch * seq, d).astype(F32)
    hb = h.astype(BF16)
    vec = lambda a: a.reshape(1, -1).astype(F32)
    for i in range(depth):
        kind, j = i % 4, i // 4
        mg, mb = vec(ln_mix_g[i]), vec(ln_mix_b[i])
        if kind == 0:
            h, hb = _diff_mixer(h, hb, diff_wq[j], diff_wk[j], diff_wv[j], diff_lq1[j], diff_lk1[j], diff_lq2[j],
                                diff_lk2[j], diff_sub_g[j], diff_wo[j], mg, mb, batch=batch, seq=seq, alpha=alpha,
                                lambda_init=0.8 - 0.6 * math.exp(-0.3 * i))
        elif kind == 1:
            h, hb = _band_mixer(h, hb, ca_w_qkv[j], ca_rel_bias[j], ca_wo[j], mg, mb, batch=batch, seq=seq,
                                alpha=alpha)
        elif kind == 2:
            h, hb = _mla_mixer(h, hb, mla_w_dq[j], mla_q_norm_g[j], mla_w_uq[j], mla_w_dkv[j], mla_kv_norm_g[j],
                               mla_w_ukv[j], mla_wo[j], mg, mb, batch=batch, seq=seq, alpha=alpha)
        else:
            h, hb = _sg_mixer(h, hb, sg_w_in[j], sg_v_norm_g[j], sg_v_norm_b[j], sg_w_s[j], sg_b_s[j], sg_w_out[j],
                              mg, mb, alpha=alpha)
        fg, fb = vec(ln_ffn_g[i]), vec(ln_ffn_b[i])
        if i % 2 == 0:
            h, hb = _ffn(hb, ffn_w1[i // 2].astype(BF16), ffn_w3[i // 2].astype(BF16), ffn_w2[i // 2].astype(BF16),
                         h, fg, fb, alpha=alpha)
        else:
            h, hb = _moe(h, moe_w_router[i // 2], moe_w1[i // 2].astype(BF16), moe_w3[i // 2].astype(BF16),
                         moe_w2[i // 2].astype(BF16), fg, fb, alpha=alpha)
    return h.reshape(batch, seq, d).astype(x.dtype)
```

```python
import functools
import math

import jax
import jax.numpy as jnp
import numpy as np
from jax import lax
from jax.experimental import pallas as pl
from jax.experimental.pallas import tpu as pltpu

F32 = jnp.float32
BF16 = jnp.bfloat16
I32 = jnp.int32

LANE = 128
VMEM_LIMIT = 56 * 1024 * 1024

CHUNK = 64
N_EXPERTS = 8
LN_EPS = 1e-5
RMS_EPS = 1e-6
NEG_INF = -1e30
ROPE_THETA = 10000.0
CA_LEFT = 8
CA_REL_CLIP = 128
ROPE_HALF = 16


def _cp(n_axes, vmem=VMEM_LIMIT):
    return pltpu.CompilerParams(dimension_semantics=("arbitrary",) * n_axes, vmem_limit_bytes=vmem)


def _resident(shape):
    nd = len(shape)
    return pl.BlockSpec(shape, lambda *_: (0,) * nd, pipeline_mode=pl.Buffered(1))


def _res_ln(h, m, g, b, alpha):
    z = alpha * h + m
    mu = jnp.mean(z, axis=-1, keepdims=True)
    zc = z - mu
    var = jnp.mean(zc * zc, axis=-1, keepdims=True)
    return zc * lax.rsqrt(var + LN_EPS) * g + b


def _rms(x, g, eps):
    ms = jnp.mean(x * x, axis=-1, keepdims=True)
    return x * lax.rsqrt(ms + eps) * g


def _linear_kernel(x_ref, w_ref, o_ref):
    x = x_ref[...].astype(BF16)
    o_ref[...] = jnp.dot(x, w_ref[...], preferred_element_type=F32).astype(o_ref.dtype)


def _linear(x, w, *, tm=512, out_dtype=BF16, name="linear"):
    m, k = x.shape
    n = w.shape[1]
    tm = min(tm, m)
    return pl.pallas_call(
        _linear_kernel,
        grid=(m // tm,),
        in_specs=[pl.BlockSpec((tm, k), lambda i: (i, 0)), _resident((k, n))],
        out_specs=pl.BlockSpec((tm, n), lambda i: (i, 0)),
        out_shape=jax.ShapeDtypeStruct((m, n), out_dtype),
        compiler_params=_cp(1),
        name=name,
    )(x, w)


def _proj_res_ln_kernel(a_ref, w_ref, h_ref, g_ref, b_ref, oh_ref, ohb_ref, *, alpha):
    m = jnp.dot(a_ref[...], w_ref[...], preferred_element_type=F32)
    y = _res_ln(h_ref[...], m, g_ref[...], b_ref[...], alpha)
    oh_ref[...] = y
    ohb_ref[...] = y.astype(BF16)


def _proj_res_ln(a, w, h, g, b, *, alpha, tm=512, name="proj_res_ln"):
    m, k = a.shape
    d = w.shape[1]
    tm = min(tm, m)
    row = lambda i: (i, 0)
    return pl.pallas_call(
        functools.partial(_proj_res_ln_kernel, alpha=alpha),
        grid=(m // tm,),
        in_specs=[pl.BlockSpec((tm, k), row), _resident((k, d)), pl.BlockSpec((tm, d), row),
                  _resident((1, d)), _resident((1, d))],
        out_specs=[pl.BlockSpec((tm, d), row), pl.BlockSpec((tm, d), row)],
        out_shape=[jax.ShapeDtypeStruct((m, d), F32), jax.ShapeDtypeStruct((m, d), BF16)],
        compiler_params=_cp(1),
        name=name,
    )(a, w, h, g, b)


def _ffn_kernel(x_ref, w1_ref, w3_ref, w2_ref, h_ref, g_ref, b_ref, oh_ref, ohb_ref, acc_ref, *, alpha, fc):
    x = x_ref[...]
    f = w1_ref.shape[1]
    for c in range(f // fc):
        sl = slice(c * fc, (c + 1) * fc)
        a = jnp.dot(x, w1_ref[:, sl], preferred_element_type=F32)
        b3 = jnp.dot(x, w3_ref[:, sl], preferred_element_type=F32)
        gated = (jax.nn.silu(a) * b3).astype(BF16)
        part = jnp.dot(gated, w2_ref[sl, :], preferred_element_type=F32)
        if c == 0:
            acc_ref[...] = part
        else:
            acc_ref[...] += part
    y = _res_ln(h_ref[...], acc_ref[...], g_ref[...], b_ref[...], alpha)
    oh_ref[...] = y
    ohb_ref[...] = y.astype(BF16)


def _ffn(xb, w1, w3, w2, h, g, b, *, alpha, tm=512, fc=256):
    m, d = xb.shape
    f = w1.shape[1]
    tm = min(tm, m)
    row = lambda i: (i, 0)
    return pl.pallas_call(
        functools.partial(_ffn_kernel, alpha=alpha, fc=fc),
        grid=(m // tm,),
        in_specs=[pl.BlockSpec((tm, d), row), _resident((d, f)), _resident((d, f)), _resident((f, d)),
                  pl.BlockSpec((tm, d), row), _resident((1, d)), _resident((1, d))],
        out_specs=[pl.BlockSpec((tm, d), row), pl.BlockSpec((tm, d), row)],
        out_shape=[jax.ShapeDtypeStruct((m, d), F32), jax.ShapeDtypeStruct((m, d), BF16)],
        scratch_shapes=[pltpu.VMEM((tm, d), F32)],
        compiler_params=_cp(1),
        name="ffn_swiglu",
    )(xb, w1, w3, w2, h, g, b)


ONES_ROWS = 16
ACC_ROWS = LANE + ONES_ROWS


def _vt_ext(v_blk):
    vt = v_blk.astype(F32).T.astype(BF16)
    return jnp.concatenate([vt, jnp.ones((ONES_ROWS, v_blk.shape[0]), BF16)], axis=0)


def _split_heads(x):
    low = lax.broadcasted_iota(I32, x.shape, 1) < CHUNK
    zero = jnp.zeros_like(x)
    return jnp.where(low, x, zero), jnp.where(low, zero, x)


_NT = (((1,), (1,)), ((), ()))


def _flash_kernel(*refs, mode, tq, lambda_init):
    if mode == "diff":
        lq1_ref, lk1_ref, lq2_ref, lk2_ref, subg_ref, q_ref, k_ref, v_ref, o_ref, vt_ref, m_ref, acc_ref, s_ref = refs
    else:
        q_ref, k_ref, v_ref, o_ref, vt_ref, m_ref, acc_ref, s_ref = refs
    qi = pl.program_id(2)

    @pl.when(qi == 0)
    def _():
        for j in range(vt_ref.shape[0]):
            vt_ref[j] = _vt_ext(v_ref[j * tq:(j + 1) * tq, :])

    if mode == "diff":
        qs = _split_heads(q_ref[...] * jnp.asarray(CHUNK ** -0.5, BF16))
    else:
        qs = (q_ref[:, :LANE], q_ref[:, LANE:])
    m_ref[...] = jnp.full(m_ref.shape, NEG_INF, F32)
    acc_ref[...] = jnp.zeros(acc_ref.shape, F32)

    def scores(blk):
        start = pl.multiple_of(blk * tq, tq)
        kb = k_ref[pl.ds(start, tq), :]
        kks = (kb, kb) if mode == "diff" else (kb[:, :LANE], kb[:, LANE:])
        return [lax.dot_general(kks[mp], qs[mp], _NT, preferred_element_type=F32) for mp in range(2)]

    def fold(blk, ss, masked):
        vt = vt_ref[blk]
        for mp in range(2):
            s = ss[mp]
            if masked:
                kc = lax.broadcasted_iota(I32, s.shape, 0) >> 6
                qc = lax.broadcasted_iota(I32, s.shape, 1) >> 6
                s = jnp.where(kc <= qc, s, NEG_INF)
            m_old = m_ref[mp]
            m_new = jnp.maximum(m_old, jnp.max(s, axis=0, keepdims=True))
            p = jnp.exp(s - m_new).astype(BF16)
            pv = jnp.dot(vt, p, preferred_element_type=F32)
            acc_ref[mp] = jnp.exp(m_old - m_new) * acc_ref[mp] + pv
            m_ref[mp] = m_new

    first = scores(0)
    s_ref[0], s_ref[1] = first[0], first[1]

    def step(j, carry):
        cur = [s_ref[0], s_ref[1]]
        nxt1 = scores(2 * j + 1)
        nxt2 = scores(2 * j + 2)
        fold(2 * j, cur, False)
        fold(2 * j + 1, nxt1, False)
        s_ref[0], s_ref[1] = nxt2[0], nxt2[1]
        return carry

    lax.fori_loop(0, qi >> 1, step, 0)

    @pl.when((qi & 1) == 1)
    def _():
        cur = [s_ref[0], s_ref[1]]
        nxt = scores(qi)
        fold(qi - 1, cur, False)
        s_ref[0], s_ref[1] = nxt[0], nxt[1]

    fold(qi, [s_ref[0], s_ref[1]], True)

    acc_a, acc_b = acc_ref[0], acc_ref[1]
    oa = acc_a[:LANE] / acc_a[LANE:LANE + 1]
    ob = acc_b[:LANE] / acc_b[LANE:LANE + 1]
    if mode == "diff":
        lam = (jnp.exp(jnp.sum(lq1_ref[...] * lk1_ref[...], axis=-1, keepdims=True))
               - jnp.exp(jnp.sum(lq2_ref[...] * lk2_ref[...], axis=-1, keepdims=True)) + lambda_init)
        o = (oa - lam * ob).T
        o = _rms(o, subg_ref[...], 1e-5) * (1.0 - lambda_init)
    else:
        row = lax.broadcasted_iota(I32, oa.shape, 0)
        o = jnp.where(row < CHUNK, oa, ob).T
    o_ref[...] = o.astype(o_ref.dtype)


def _flash(mode, q_arr, k_arr, v_arr, *, batch, seq, groups, q_blk0, k_blk0, v_blk0, extra=(), lambda_init=0.0,
           tq=256):
    qw = LANE if mode == "diff" else 2 * LANE
    tq = min(tq, seq)
    nq = seq // tq
    n = batch * seq
    in_specs = [pl.BlockSpec(e.shape, lambda b, g, i: (0, 0)) for e in extra]
    in_specs += [
        pl.BlockSpec((tq, qw), lambda b, g, i: (b * nq + i, q_blk0 + g)),
        pl.BlockSpec((seq, qw), lambda b, g, i: (b, k_blk0 + g)),
        pl.BlockSpec((seq, LANE), lambda b, g, i: (b, v_blk0 + g)),
    ]
    return pl.pallas_call(
        functools.partial(_flash_kernel, mode=mode, tq=tq, lambda_init=lambda_init),
        grid=(batch, groups, nq),
        in_specs=in_specs,
        out_specs=pl.BlockSpec((tq, LANE), lambda b, g, i: (b * nq + i, g)),
        out_shape=jax.ShapeDtypeStruct((n, groups * LANE), BF16),
        scratch_shapes=[pltpu.VMEM((nq, ACC_ROWS, tq), BF16), pltpu.VMEM((2, 1, tq), F32),
                        pltpu.VMEM((2, ACC_ROWS, tq), F32), pltpu.VMEM((2, tq, tq), F32)],
        compiler_params=_cp(3),
        name="flash_" + mode,
    )(*extra, q_arr, k_arr, v_arr)


BAND_TQ = 4 * CHUNK
BAND_W = (CA_LEFT + 4) * CHUNK
BAND_PAD = CA_LEFT * CHUNK
BAND_NW = BAND_W // BAND_TQ


def _band_kernel(q_ref, k_ref, v_ref, bias_ref, o_ref, kpad_ref, vt_ref, *, seq):
    t = pl.program_id(2)
    npad = BAND_PAD // BAND_TQ

    @pl.when(t == 0)
    def _():
        kpad_ref[0:BAND_PAD, :] = jnp.zeros((BAND_PAD, LANE), BF16)
        kpad_ref[BAND_PAD:BAND_PAD + seq, :] = k_ref[...]
        for j in range(npad):
            vt_ref[j] = jnp.zeros((ACC_ROWS, BAND_TQ), BF16)
        for j in range(seq // BAND_TQ):
            vt_ref[npad + j] = _vt_ext(v_ref[j * BAND_TQ:(j + 1) * BAND_TQ, :])

    def run(front):
        qhs = _split_heads(q_ref[...] * jnp.asarray(CHUNK ** -0.5, BF16))
        ss = []
        for u in range(BAND_NW):
            start = pl.multiple_of((t + u) * BAND_TQ, BAND_TQ)
            kb = kpad_ref[pl.ds(start, BAND_TQ), :]
            ss.append([lax.dot_general(kb, qh, _NT, preferred_element_type=F32) for qh in qhs])
        outs = []
        for hh in range(2):
            blocks = []
            for u in range(BAND_NW):
                s = ss[u][hh] + bias_ref[0, hh, u * BAND_TQ:(u + 1) * BAND_TQ, :]
                if front:
                    invalid = lax.broadcasted_iota(I32, s.shape, 0) + (t + u) * BAND_TQ < BAND_PAD
                    s = jnp.where(invalid, NEG_INF, s)
                blocks.append(s)
            m = blocks[0].max(axis=0, keepdims=True)
            for s in blocks[1:]:
                m = jnp.maximum(m, s.max(axis=0, keepdims=True))
            acc = None
            for u, s in enumerate(blocks):
                pv = jnp.dot(vt_ref[t + u], jnp.exp(s - m).astype(BF16), preferred_element_type=F32)
                acc = pv if acc is None else acc + pv
            outs.append(acc[:LANE] / acc[LANE:LANE + 1])
        row = lax.broadcasted_iota(I32, outs[0].shape, 0)
        o_ref[...] = jnp.where(row < CHUNK, outs[0], outs[1]).T.astype(o_ref.dtype)

    pl.when(t < npad)(lambda: run(True))
    pl.when(t >= npad)(lambda: run(False))


def _band_attention(qkv, bias, *, batch, seq, pairs):
    n = batch * seq
    nt = seq // BAND_TQ
    return pl.pallas_call(
        functools.partial(_band_kernel, seq=seq),
        grid=(pairs, batch, nt),
        in_specs=[
            pl.BlockSpec((BAND_TQ, LANE), lambda g, b, t: (b * nt + t, g)),
            pl.BlockSpec((seq, LANE), lambda g, b, t: (b, pairs + g)),
            pl.BlockSpec((seq, LANE), lambda g, b, t: (b, 2 * pairs + g)),
            pl.BlockSpec((1, 2, BAND_W, BAND_TQ), lambda g, b, t: (g, 0, 0, 0)),
        ],
        out_specs=pl.BlockSpec((BAND_TQ, LANE), lambda g, b, t: (b * nt + t, g)),
        out_shape=jax.ShapeDtypeStruct((n, pairs * LANE), BF16),
        scratch_shapes=[pltpu.VMEM((seq + BAND_PAD, LANE), BF16),
                        pltpu.VMEM(((seq + BAND_PAD) // BAND_TQ, ACC_ROWS, BAND_TQ), BF16)],
        compiler_params=_cp(3),
        name="band_attention",
    )(qkv, qkv, qkv, bias)


def _band_bias_tiles(rel_bias):
    heads = rel_bias.shape[0]
    rel = np.arange(BAND_TQ + BAND_W - 1) - (BAND_W - 1) + BAND_PAD
    ext = rel_bias.astype(F32)[:, np.clip(rel, -CA_REL_CLIP, CA_REL_CLIP) + CA_REL_CLIP]
    rows = [ext[:, BAND_W - 1 - j:BAND_W - 1 - j + BAND_TQ] for j in range(BAND_W)]
    bias = jnp.stack(rows, axis=1)
    ci = (jnp.arange(BAND_TQ) // CHUNK)[None, :]
    cj = (jnp.arange(BAND_W) // CHUNK)[:, None]
    vis = (cj >= ci) & (cj <= ci + CA_LEFT)
    bias = jnp.where(vis[None], bias, NEG_INF)
    return bias.reshape(heads // 2, 2, BAND_W, BAND_TQ)


def _mla_cq_kernel(x_ref, w_ref, g_ref, o_ref):
    c = jnp.dot(x_ref[...], w_ref[...], preferred_element_type=F32)
    o_ref[...] = _rms(c, g_ref[...], RMS_EPS).astype(o_ref.dtype)


def _mla_q_kernel(c_ref, wa_ref, wb_ref, ct_ref, st_ref, o_ref, *, scale, heads):
    c = c_ref[...]
    a = jnp.dot(c, wa_ref[...], preferred_element_type=F32)
    b = jnp.dot(c, wb_ref[...], preferred_element_type=F32)
    ct = ct_ref[...] * scale
    st = st_ref[...] * scale
    for h in range(heads):
        sl = slice(h * LANE, (h + 1) * LANE)
        o_ref[:, sl] = (a[:, sl] * ct + b[:, sl] * st).astype(o_ref.dtype)


def _mla_ckv_kernel(x_ref, w_ref, g_ref, ct_ref, st_ref, o_ref, *, rank):
    y = jnp.dot(x_ref[...], w_ref[...], preferred_element_type=F32)
    o_ref[:, :rank] = _rms(y[:, :rank], g_ref[...], RMS_EPS).astype(o_ref.dtype)
    kr = y[:, rank:rank + LANE] * ct_ref[...] + y[:, rank + LANE:rank + 2 * LANE] * st_ref[...]
    o_ref[:, rank:] = kr.astype(o_ref.dtype)


GMLP_CHUNK = 128


def _sg_in_kernel(x_ref, w_ref, g_ref, b_ref, u_ref, v_ref):
    width = u_ref.shape[1]
    hh = jax.nn.gelu(jnp.dot(x_ref[...], w_ref[...], preferred_element_type=F32))
    u_ref[...] = hh[:, :width].astype(u_ref.dtype)
    v = hh[:, width:]
    mu = jnp.mean(v, axis=-1, keepdims=True)
    vc = v - mu
    var = jnp.mean(vc * vc, axis=-1, keepdims=True)
    v_ref[...] = (vc * lax.rsqrt(var + LN_EPS) * g_ref[...] + b_ref[...]).astype(v_ref.dtype)


def _sg_mix_kernel(u_ref, v_ref, ws_ref, bs_ref, wo_ref, h_ref, g_ref, b_ref, oh_ref, ohb_ref, gated_ref, *,
                   alpha, groups):
    tm = u_ref.shape[0]
    r = lax.broadcasted_iota(I32, (GMLP_CHUNK, GMLP_CHUNK), 0) >> 6
    c = lax.broadcasted_iota(I32, (GMLP_CHUNK, GMLP_CHUNK), 1) >> 6
    vis = c <= r
    for gi in range(groups):
        w = jnp.where(vis, ws_ref[gi], 0.0).astype(BF16)
        cs = slice(gi * LANE, (gi + 1) * LANE)
        for ch in range(tm // GMLP_CHUNK):
            rs = slice(ch * GMLP_CHUNK, (ch + 1) * GMLP_CHUNK)
            mixed = jnp.dot(w, v_ref[rs, cs], preferred_element_type=F32) + bs_ref[:, cs]
            gated_ref[rs, cs] = (u_ref[rs, cs].astype(F32) * mixed).astype(BF16)
    m = jnp.dot(gated_ref[...], wo_ref[...], preferred_element_type=F32)
    y = _res_ln(h_ref[...], m, g_ref[...], b_ref[...], alpha)
    oh_ref[...] = y
    ohb_ref[...] = y.astype(BF16)


INFO_IDX, INFO_RANK, INFO_GATE = 0, 2, 4
ROW_DMA_UNROLL = 8


def _router_kernel(h_ref, w_ref, info_ref, cnt_ref, run_ref):
    i = pl.program_id(0)
    tr = h_ref.shape[0]

    @pl.when(i == 0)
    def _():
        run_ref[...] = jnp.zeros_like(run_ref)

    x = h_ref[...]
    x1 = x.astype(BF16)
    r1 = x - x1.astype(F32)
    x2 = r1.astype(BF16)
    x3 = (r1 - x2.astype(F32)).astype(BF16)
    w = w_ref[...]
    lane = lax.broadcasted_iota(I32, (tr, LANE), 1)
    t = (jnp.dot(x1, w, preferred_element_type=F32)
         + jnp.where(lane < 2 * N_EXPERTS, jnp.dot(x2, w, preferred_element_type=F32), 0.0)
         + jnp.where(lane < N_EXPERTS, jnp.dot(x3, w, preferred_element_type=F32), 0.0))
    logits = t + pltpu.roll(t, LANE - N_EXPERTS, 1) + pltpu.roll(t, LANE - 2 * N_EXPERTS, 1)
    big = jnp.float32(3e38)
    logits = jnp.where(lane < N_EXPERTS, logits, -big)

    lane_f = lane.astype(F32)
    v1 = jnp.max(logits, axis=-1, keepdims=True)
    i1 = jnp.min(jnp.where(logits == v1, lane_f, float(LANE)), axis=-1, keepdims=True)
    rest = jnp.where(lane_f == i1, -big, logits)
    v2 = jnp.max(rest, axis=-1, keepdims=True)
    i2 = jnp.min(jnp.where(rest == v2, lane_f, float(LANE)), axis=-1, keepdims=True)
    e = jnp.exp(v2 - v1)
    g1 = 1.0 / (1.0 + e)
    g2 = e / (1.0 + e)

    oh1 = jnp.where(lane_f == i1, 1.0, 0.0)
    oh2 = jnp.where(lane_f == i2, 1.0, 0.0)
    oh = oh1 + oh2
    rr = lax.broadcasted_iota(I32, (tr, tr), 0)
    cc = lax.broadcasted_iota(I32, (tr, tr), 1)
    before = jnp.where(cc < rr, 1.0, 0.0).astype(BF16)
    prior = jnp.dot(before, oh.astype(BF16), preferred_element_type=F32) + run_ref[...]
    rank1 = jnp.sum(prior * oh1, axis=-1, keepdims=True)
    rank2 = jnp.sum(prior * oh2, axis=-1, keepdims=True)
    run_ref[...] += jnp.sum(oh, axis=0, keepdims=True)
    cnt_ref[...] = run_ref[...]

    info = jnp.where(lane == INFO_IDX, i1, 0.0)
    info = jnp.where(lane == INFO_IDX + 1, i2, info)
    info = jnp.where(lane == INFO_RANK, rank1, info)
    info = jnp.where(lane == INFO_RANK + 1, rank2, info)
    info = jnp.where(lane == INFO_GATE, g1, info)
    info = jnp.where(lane == INFO_GATE + 1, g2, info)
    info_ref[...] = info


def _dispatch_kernel(pos_ref, h_ref, init_ref, xs_ref, sem):
    del init_ref
    td = h_ref.shape[0]

    def start(r, carry):
        for k in range(2):
            p = pos_ref[0, 0, 2 * r + k]
            pltpu.make_async_copy(h_ref.at[pl.ds(r, 1), :], xs_ref.at[pl.ds(p, 1), :], sem).start(priority=k)
        return carry

    lax.fori_loop(0, td, start, 0, unroll=ROW_DMA_UNROLL)
    for _ in range(2):
        pltpu.make_async_copy(h_ref, xs_ref.at[pl.ds(0, td), :], sem).wait()


def _experts_kernel(te_ref, nu_ref, x_ref, w1_ref, w3_ref, w2_ref, o_ref, xb_ref, acc_ref, *, fc):
    i = pl.program_id(0)
    j = pl.program_id(1)
    nj = pl.num_programs(1)
    del te_ref

    @pl.when(i < nu_ref[0])
    def _():
        @pl.when(j == 0)
        def _():
            xb_ref[...] = x_ref[...].astype(BF16)

        x = xb_ref[...]
        tf = w1_ref.shape[1]
        part = None
        for c in range(tf // fc):
            sl = slice(c * fc, (c + 1) * fc)
            a = jnp.dot(x, w1_ref[:, sl].astype(BF16), preferred_element_type=F32)
            b3 = jnp.dot(x, w3_ref[:, sl].astype(BF16), preferred_element_type=F32)
            gated = (jax.nn.silu(a) * b3).astype(BF16)
            d = jnp.dot(gated, w2_ref[sl, :].astype(BF16), preferred_element_type=F32)
            part = d if part is None else part + d

        @pl.when(j == 0)
        def _():
            acc_ref[...] = part

        @pl.when(j > 0)
        def _():
            acc_ref[...] += part

        @pl.when(j == nj - 1)
        def _():
            o_ref[...] = acc_ref[...]

    @pl.when((i >= nu_ref[0]) & (j == nj - 1))
    def _():
        o_ref[...] = jnp.zeros_like(o_ref)


def _combine_kernel(pos_ref, info_ref, h_ref, g_ref, b_ref, ys_ref, oh_ref, ohb_ref, buf_ref, sem, *, alpha):
    tc = h_ref.shape[0]

    def start(r, carry):
        for k in range(2):
            p = pos_ref[0, 0, 2 * r + k]
            pltpu.make_async_copy(ys_ref.at[pl.ds(p, 1), :], buf_ref.at[k, pl.ds(r, 1), :], sem).start(priority=k)
        return carry

    lax.fori_loop(0, tc, start, 0, unroll=ROW_DMA_UNROLL)
    for k in range(2):
        pltpu.make_async_copy(ys_ref.at[pl.ds(0, tc), :], buf_ref.at[k], sem).wait()
    info = info_ref[...]
    g1 = info[:, INFO_GATE:INFO_GATE + 1]
    g2 = info[:, INFO_GATE + 1:INFO_GATE + 2]
    m = g1 * buf_ref[0] + g2 * buf_ref[1]
    y = _res_ln(h_ref[...], m, g_ref[...], b_ref[...], alpha)
    oh_ref[...] = y
    ohb_ref[...] = y.astype(BF16)


def _split3(w):
    hi = w.astype(BF16)
    r = w - hi.astype(F32)
    mid = r.astype(BF16)
    lo = (r - mid.astype(F32)).astype(BF16)
    return hi, mid, lo


def _moe(h, w_router, w1, w3, w2, g, b, *, layer, alpha, tm=1024, tf=512, fc=256, td=256):
    n, d = h.shape
    f = w1.shape[3]
    tm = min(tm, n)
    td = min(td, n)
    tf = min(tf, f)
    row = lambda i: (i, 0)

    hi, mid, lo = _split3(w_router.astype(F32))
    wr = jnp.concatenate([hi, mid, lo, jnp.zeros((d, LANE - 3 * N_EXPERTS), BF16)], axis=1)
    tr = min(512, n)
    info, cnt = pl.pallas_call(
        _router_kernel,
        grid=(n // tr,),
        in_specs=[pl.BlockSpec((tr, d), row), _resident((d, LANE))],
        out_specs=[pl.BlockSpec((tr, LANE), row), pl.BlockSpec((1, LANE), lambda i: (0, 0))],
        out_shape=[jax.ShapeDtypeStruct((n, LANE), F32), jax.ShapeDtypeStruct((1, LANE), F32)],
        scratch_shapes=[pltpu.VMEM((1, LANE), F32)],
        compiler_params=_cp(1),
        name="moe_router",
    )(h, wr)

    idx = info[:, INFO_IDX:INFO_IDX + 2].astype(I32)
    rank = info[:, INFO_RANK:INFO_RANK + 2].astype(I32)
    counts = cnt[0, :N_EXPERTS].astype(I32)
    padded = ((counts + tm - 1) // tm) * tm
    ends = jnp.cumsum(padded)
    starts = ends - padded
    pos = starts[idx] + rank
    n_tiles = (2 * n) // tm + N_EXPERTS
    n_used = (ends[-1] // tm).astype(I32)
    tile_start = jnp.arange(n_tiles, dtype=I32) * tm
    tile_e = jnp.sum(tile_start[:, None] >= ends[None, :], axis=1).astype(I32)
    tile_e = jnp.minimum(tile_e, tile_e[jnp.maximum(n_used - 1, 0)])
    rows = n_tiles * tm
    pos_blocks = pos.reshape(n // td, 1, 2 * td)

    xs = pl.pallas_call(
        _dispatch_kernel,
        grid=(n // td,),
        in_specs=[pl.BlockSpec((1, 1, 2 * td), lambda i: (i, 0, 0), memory_space=pltpu.SMEM),
                  pl.BlockSpec((td, d), row),
                  pl.BlockSpec(memory_space=pl.ANY)],
        out_specs=pl.BlockSpec(memory_space=pl.ANY),
        out_shape=jax.ShapeDtypeStruct((rows, d), F32),
        scratch_shapes=[pltpu.SemaphoreType.DMA(())],
        input_output_aliases={2: 0},
        compiler_params=_cp(1),
        name="moe_dispatch",
    )(pos_blocks, h, jnp.zeros((rows, d), F32))

    nj = f // tf

    def x_map(i, j, te, nu):
        return (jnp.minimum(i, nu[0] - 1), 0)

    def w13_map(i, j, te, nu):
        return (layer, te[i], 0, jnp.where(i < nu[0], j, nj - 1))

    def w2_map(i, j, te, nu):
        return (layer, te[i], jnp.where(i < nu[0], j, nj - 1), 0)

    ys = pl.pallas_call(
        functools.partial(_experts_kernel, fc=fc),
        grid_spec=pltpu.PrefetchScalarGridSpec(
            num_scalar_prefetch=2,
            grid=(n_tiles, nj),
            in_specs=[pl.BlockSpec((tm, d), x_map),
                      pl.BlockSpec((None, None, d, tf), w13_map),
                      pl.BlockSpec((None, None, d, tf), w13_map),
                      pl.BlockSpec((None, None, tf, d), w2_map)],
            out_specs=pl.BlockSpec((tm, d), lambda i, j, te, nu: (i, 0)),
            scratch_shapes=[pltpu.VMEM((tm, d), BF16), pltpu.VMEM((tm, d), F32)],
        ),
        out_shape=jax.ShapeDtypeStruct((rows, d), F32),
        compiler_params=_cp(2),
        name="moe_experts",
    )(tile_e, n_used.reshape(1), xs, w1, w3, w2)

    tc = td
    return pl.pallas_call(
        functools.partial(_combine_kernel, alpha=alpha),
        grid=(n // tc,),
        in_specs=[pl.BlockSpec((1, 1, 2 * tc), lambda i: (i, 0, 0), memory_space=pltpu.SMEM),
                  pl.BlockSpec((tc, LANE), row),
                  pl.BlockSpec((tc, d), row),
                  _resident((1, d)), _resident((1, d)),
                  pl.BlockSpec(memory_space=pl.ANY)],
        out_specs=[pl.BlockSpec((tc, d), row), pl.BlockSpec((tc, d), row)],
        out_shape=[jax.ShapeDtypeStruct((n, d), F32), jax.ShapeDtypeStruct((n, d), BF16)],
        scratch_shapes=[pltpu.VMEM((2, tc, d), F32), pltpu.SemaphoreType.DMA(())],
        compiler_params=_cp(1),
        name="moe_combine",
    )(pos_blocks, info, h, g, b, ys)


def _diff_mixer(h, hb, wq, wk, wv, lq1, lk1, lq2, lk2, sub_g, wo, g, b, *, batch, seq, alpha, lambda_init):
    d = wq.shape[0]
    heads = d // LANE
    qkv = _linear(hb, jnp.concatenate([wq, wk, wv], axis=1).astype(BF16), name="diff_qkv")
    extra = tuple(a.reshape(1, -1).astype(F32) for a in (lq1, lk1, lq2, lk2, sub_g))
    o = _flash("diff", qkv, qkv, qkv, batch=batch, seq=seq, groups=heads, q_blk0=0, k_blk0=heads, v_blk0=2 * heads,
               extra=extra, lambda_init=lambda_init)
    return _proj_res_ln(o, wo.astype(BF16), h, g, b, alpha=alpha, name="diff_out")


def _band_mixer(h, hb, w_qkv, rel_bias, wo, g, b, *, batch, seq, alpha):
    d = w_qkv.shape[0]
    qkv = _linear(hb, w_qkv.astype(BF16), name="band_qkv")
    o = _band_attention(qkv, _band_bias_tiles(rel_bias), batch=batch, seq=seq, pairs=d // LANE)
    return _proj_res_ln(o, wo.astype(BF16), h, g, b, alpha=alpha, name="band_out")


def _mla_mixer(h, hb, w_dq, q_g, w_uq, w_dkv, kv_g, w_ukv, wo, g, b, *, batch, seq, alpha, tm=512):
    n, d = hb.shape
    q_rank = w_dq.shape[1]
    kv_rank = kv_g.shape[-1]
    heads = wo.shape[0] // CHUNK
    nope, rope = 64, 2 * ROPE_HALF
    tm = min(tm, seq)
    row = lambda i: (i, 0)
    tab = lambda i: (i % (seq // tm), 0)

    inv_freq = ROPE_THETA ** (-jnp.arange(ROPE_HALF, dtype=F32) / ROPE_HALF)
    ang = jnp.arange(seq).astype(F32)[:, None] * inv_freq[None, :]
    cos, sin = jnp.cos(ang), jnp.sin(ang)
    z = lambda w: jnp.zeros((seq, w), F32)
    ct_q = jnp.concatenate([jnp.ones((seq, nope), F32), cos, cos, z(LANE - nope - rope)], axis=1)
    st_q = jnp.concatenate([z(nope), sin, sin, z(LANE - nope - rope)], axis=1)
    ct_k = jnp.concatenate([cos, cos, z(LANE - rope)], axis=1)
    st_k = jnp.concatenate([sin, sin, z(LANE - rope)], axis=1)

    wq3 = w_uq.reshape(q_rank, heads, nope + rope)
    qn, q1, q2 = wq3[..., :nope], wq3[..., nope:nope + ROPE_HALF], wq3[..., nope + ROPE_HALF:]
    zq = lambda w: jnp.zeros((q_rank, heads, w), w_uq.dtype)
    wa = jnp.concatenate([qn, q1, q2, zq(LANE - nope - rope)], axis=2).reshape(q_rank, heads * LANE)
    wb = jnp.concatenate([zq(nope), -q2, q1, zq(LANE - nope - rope)], axis=2).reshape(q_rank, heads * LANE)

    k1, k2 = w_dkv[:, kv_rank:kv_rank + ROPE_HALF], w_dkv[:, kv_rank + ROPE_HALF:]
    zd = lambda w: jnp.zeros((d, w), w_dkv.dtype)
    wd = jnp.concatenate([w_dkv[:, :kv_rank], k1, k2, zd(LANE - rope), -k2, k1, zd(LANE - rope)], axis=1)

    wkv3 = w_ukv.reshape(kv_rank, heads, nope + CHUNK)
    zk = lambda r, w: jnp.zeros((r, heads, w), w_ukv.dtype)
    k_top = jnp.concatenate([wkv3[..., :nope], zk(kv_rank, LANE - nope)], axis=2).reshape(kv_rank, heads * LANE)
    eye = jnp.broadcast_to(jnp.eye(rope, dtype=w_ukv.dtype)[:, None, :], (rope, heads, rope))
    k_rope = jnp.concatenate([zk(rope, nope), eye, zk(rope, LANE - nope - rope)], axis=2).reshape(rope, heads * LANE)
    v_top = wkv3[..., nope:].reshape(kv_rank, heads * CHUNK)
    wkv = jnp.concatenate([
        jnp.concatenate([k_top, v_top], axis=1),
        jnp.concatenate([k_rope, jnp.zeros((rope, heads * CHUNK), w_ukv.dtype)], axis=1),
        jnp.zeros((LANE - rope, heads * (LANE + CHUNK)), w_ukv.dtype)], axis=0)

    cq = pl.pallas_call(
        _mla_cq_kernel,
        grid=(n // tm,),
        in_specs=[pl.BlockSpec((tm, d), row), _resident((d, q_rank)), _resident((1, q_rank))],
        out_specs=pl.BlockSpec((tm, q_rank), row),
        out_shape=jax.ShapeDtypeStruct((n, q_rank), BF16),
        compiler_params=_cp(1),
        name="mla_cq",
    )(hb, w_dq.astype(BF16), q_g.reshape(1, -1))

    q = pl.pallas_call(
        functools.partial(_mla_q_kernel, scale=float((nope + rope) ** -0.5), heads=heads),
        grid=(n // tm,),
        in_specs=[pl.BlockSpec((tm, q_rank), row), _resident((q_rank, heads * LANE)),
                  _resident((q_rank, heads * LANE)), pl.BlockSpec((tm, LANE), tab), pl.BlockSpec((tm, LANE), tab)],
        out_specs=pl.BlockSpec((tm, heads * LANE), row),
        out_shape=jax.ShapeDtypeStruct((n, heads * LANE), BF16),
        compiler_params=_cp(1),
        name="mla_q",
    )(cq, wa.astype(BF16), wb.astype(BF16), ct_q, st_q)

    ckr = pl.pallas_call(
        functools.partial(_mla_ckv_kernel, rank=kv_rank),
        grid=(n // tm,),
        in_specs=[pl.BlockSpec((tm, d), row), _resident((d, kv_rank + 2 * LANE)), _resident((1, kv_rank)),
                  pl.BlockSpec((tm, LANE), tab), pl.BlockSpec((tm, LANE), tab)],
        out_specs=pl.BlockSpec((tm, kv_rank + LANE), row),
        out_shape=jax.ShapeDtypeStruct((n, kv_rank + LANE), BF16),
        compiler_params=_cp(1),
        name="mla_ckv",
    )(hb, wd.astype(BF16), kv_g.reshape(1, -1), ct_k, st_k)

    kv = _linear(ckr, wkv.astype(BF16), name="mla_kv")
    o = _flash("mla", q, kv, kv, batch=batch, seq=seq, groups=heads // 2, q_blk0=0, k_blk0=0, v_blk0=heads)
    return _proj_res_ln(o, wo.astype(BF16), h, g, b, alpha=alpha, name="mla_out")


def _sg_mixer(h, hb, w_in, vg, vb, w_s, b_s, w_out, g, b, *, alpha, tm=512):
    n, d = hb.shape
    width = w_out.shape[0]
    groups = w_s.shape[0]
    tm = min(tm, n)
    row = lambda i: (i, 0)
    u, v = pl.pallas_call(
        _sg_in_kernel,
        grid=(n // tm,),
        in_specs=[pl.BlockSpec((tm, d), row), _resident((d, 2 * width)), _resident((1, width)),
                  _resident((1, width))],
        out_specs=[pl.BlockSpec((tm, width), row), pl.BlockSpec((tm, width), row)],
        out_shape=[jax.ShapeDtypeStruct((n, width), BF16), jax.ShapeDtypeStruct((n, width), BF16)],
        compiler_params=_cp(1),
        name="sg_in",
    )(hb, w_in.astype(BF16), vg.reshape(1, -1), vb.reshape(1, -1))
    bs_full = jnp.repeat(b_s.T.astype(F32), width // groups, axis=1)
    return pl.pallas_call(
        functools.partial(_sg_mix_kernel, alpha=alpha, groups=groups),
        grid=(n // tm,),
        in_specs=[pl.BlockSpec((tm, width), row), pl.BlockSpec((tm, width), row),
                  _resident((groups, GMLP_CHUNK, GMLP_CHUNK)), _resident((GMLP_CHUNK, width)),
                  _resident((width, d)), pl.BlockSpec((tm, d), row), _resident((1, d)), _resident((1, d))],
        out_specs=[pl.BlockSpec((tm, d), row), pl.BlockSpec((tm, d), row)],
        out_shape=[jax.ShapeDtypeStruct((n, d), F32), jax.ShapeDtypeStruct((n, d), BF16)],
        scratch_shapes=[pltpu.VMEM((tm, width), BF16)],
        compiler_params=_cp(1),
        name="sg_mix",
    )(u, v, w_s.astype(F32), bs_full, w_out.astype(BF16), h, g, b)


def kernel(x, ln_mix_g, ln_mix_b, ln_ffn_g, ln_ffn_b, diff_wq, diff_wk, diff_wv, diff_lq1, diff_lk1, diff_lq2, diff_lk2, diff_sub_g, diff_wo, ca_w_qkv, ca_rel_bias, ca_wo, mla_w_dq, mla_q_norm_g, mla_w_uq, mla_w_dkv, mla_kv_norm_g, mla_w_ukv, mla_wo, sg_w_in, sg_v_norm_g, sg_v_norm_b, sg_w_s, sg_b_s, sg_w_out, ffn_w1, ffn_w3, ffn_w2, moe_w_router, moe_w1, moe_w3, moe_w2):
    batch, seq, d = x.shape
    depth = ln_mix_g.shape[0]
    alpha = float((2 * depth) ** 0.25)
    h = x.reshape(batch * seq, d).astype(F32)
    hb = h.astype(BF16)
    vec = lambda a: a.reshape(1, -1).astype(F32)
    for i in range(depth):
        kind, j = i % 4, i // 4
        mg, mb = vec(ln_mix_g[i]), vec(ln_mix_b[i])
        if kind == 0:
            h, hb = _diff_mixer(h, hb, diff_wq[j], diff_wk[j], diff_wv[j], diff_lq1[j], diff_lk1[j], diff_lq2[j],
                                diff_lk2[j], diff_sub_g[j], diff_wo[j], mg, mb, batch=batch, seq=seq, alpha=alpha,
                                lambda_init=0.8 - 0.6 * math.exp(-0.3 * i))
        elif kind == 1:
            h, hb = _band_mixer(h, hb, ca_w_qkv[j], ca_rel_bias[j], ca_wo[j], mg, mb, batch=batch, seq=seq,
                                alpha=alpha)
        elif kind == 2:
            h, hb = _mla_mixer(h, hb, mla_w_dq[j], mla_q_norm_g[j], mla_w_uq[j], mla_w_dkv[j], mla_kv_norm_g[j],
                               mla_w_ukv[j], mla_wo[j], mg, mb, batch=batch, seq=seq, alpha=alpha)
        else:
            h, hb = _sg_mixer(h, hb, sg_w_in[j], sg_v_norm_g[j], sg_v_norm_b[j], sg_w_s[j], sg_b_s[j], sg_w_out[j],
                              mg, mb, alpha=alpha)
        fg, fb = vec(ln_ffn_g[i]), vec(ln_ffn_b[i])
        if i % 2 == 0:
            h, hb = _ffn(hb, ffn_w1[i // 2].astype(BF16), ffn_w3[i // 2].astype(BF16), ffn_w2[i // 2].astype(BF16),
                         h, fg, fb, alpha=alpha)
        else:
            h, hb = _moe(h, moe_w_router[i // 2], moe_w1, moe_w3, moe_w2, fg, fb, layer=i // 2, alpha=alpha)
    return h.reshape(batch, seq, d).astype(x.dtype)
```

```python
import functools
import math

import jax
import jax.numpy as jnp
import numpy as np
from jax import lax
from jax.experimental import pallas as pl
from jax.experimental.pallas import tpu as pltpu

F32 = jnp.float32
BF16 = jnp.bfloat16
I32 = jnp.int32

LANE = 128
VMEM_LIMIT = 56 * 1024 * 1024

CHUNK = 64
N_EXPERTS = 8
LN_EPS = 1e-5
RMS_EPS = 1e-6
NEG_INF = -1e30
ROPE_THETA = 10000.0
CA_LEFT = 8
CA_REL_CLIP = 128
ROPE_HALF = 16


def _cp(n_axes, vmem=VMEM_LIMIT):
    return pltpu.CompilerParams(dimension_semantics=("arbitrary",) * n_axes, vmem_limit_bytes=vmem)


def _resident(shape):
    nd = len(shape)
    return pl.BlockSpec(shape, lambda *_: (0,) * nd, pipeline_mode=pl.Buffered(1))


def _res_ln(h, m, g, b, alpha):
    z = alpha * h + m
    mu = jnp.mean(z, axis=-1, keepdims=True)
    zc = z - mu
    var = jnp.mean(zc * zc, axis=-1, keepdims=True)
    return zc * lax.rsqrt(var + LN_EPS) * g + b


def _rms(x, g, eps):
    ms = jnp.mean(x * x, axis=-1, keepdims=True)
    return x * lax.rsqrt(ms + eps) * g


def _linear_kernel(x_ref, *refs):
    *w_refs, o_ref = refs
    x = x_ref[...].astype(BF16)
    col = 0
    for w_ref in w_refs:
        n = w_ref.shape[1]
        o_ref[:, col:col + n] = jnp.dot(x, w_ref[...], preferred_element_type=F32).astype(o_ref.dtype)
        col += n


def _linear(x, ws, *, tm=512, out_dtype=BF16, name="linear"):
    m, k = x.shape
    n = sum(w.shape[1] for w in ws)
    tm = min(tm, m)
    return pl.pallas_call(
        _linear_kernel,
        grid=(m // tm,),
        in_specs=[pl.BlockSpec((tm, k), lambda i: (i, 0))] + [_resident(w.shape) for w in ws],
        out_specs=pl.BlockSpec((tm, n), lambda i: (i, 0)),
        out_shape=jax.ShapeDtypeStruct((m, n), out_dtype),
        compiler_params=_cp(1),
        name=name,
    )(x, *ws)


def _proj_res_ln_kernel(a_ref, w_ref, h_ref, g_ref, b_ref, oh_ref, ohb_ref, *, alpha):
    m = jnp.dot(a_ref[...], w_ref[...], preferred_element_type=F32)
    y = _res_ln(h_ref[...], m, g_ref[...], b_ref[...], alpha)
    oh_ref[...] = y
    ohb_ref[...] = y.astype(BF16)


def _proj_res_ln(a, w, h, g, b, *, alpha, tm=512, name="proj_res_ln"):
    m, k = a.shape
    d = w.shape[1]
    tm = min(tm, m)
    row = lambda i: (i, 0)
    return pl.pallas_call(
        functools.partial(_proj_res_ln_kernel, alpha=alpha),
        grid=(m // tm,),
        in_specs=[pl.BlockSpec((tm, k), row), _resident((k, d)), pl.BlockSpec((tm, d), row),
                  _resident((1, d)), _resident((1, d))],
        out_specs=[pl.BlockSpec((tm, d), row), pl.BlockSpec((tm, d), row)],
        out_shape=[jax.ShapeDtypeStruct((m, d), F32), jax.ShapeDtypeStruct((m, d), BF16)],
        compiler_params=_cp(1),
        name=name,
    )(a, w, h, g, b)


def _ffn_kernel(x_ref, w1_ref, w3_ref, w2_ref, h_ref, g_ref, b_ref, oh_ref, ohb_ref, acc_ref, *, alpha, fc):
    x = x_ref[...]
    f = w1_ref.shape[1]
    for c in range(f // fc):
        sl = slice(c * fc, (c + 1) * fc)
        a = jnp.dot(x, w1_ref[:, sl], preferred_element_type=F32)
        b3 = jnp.dot(x, w3_ref[:, sl], preferred_element_type=F32)
        gated = (jax.nn.silu(a) * b3).astype(BF16)
        part = jnp.dot(gated, w2_ref[sl, :], preferred_element_type=F32)
        if c == 0:
            acc_ref[...] = part
        else:
            acc_ref[...] += part
    y = _res_ln(h_ref[...], acc_ref[...], g_ref[...], b_ref[...], alpha)
    oh_ref[...] = y
    ohb_ref[...] = y.astype(BF16)


def _ffn(xb, w1, w3, w2, h, g, b, *, alpha, tm=512, fc=256):
    m, d = xb.shape
    f = w1.shape[1]
    tm = min(tm, m)
    row = lambda i: (i, 0)
    return pl.pallas_call(
        functools.partial(_ffn_kernel, alpha=alpha, fc=fc),
        grid=(m // tm,),
        in_specs=[pl.BlockSpec((tm, d), row), _resident((d, f)), _resident((d, f)), _resident((f, d)),
                  pl.BlockSpec((tm, d), row), _resident((1, d)), _resident((1, d))],
        out_specs=[pl.BlockSpec((tm, d), row), pl.BlockSpec((tm, d), row)],
        out_shape=[jax.ShapeDtypeStruct((m, d), F32), jax.ShapeDtypeStruct((m, d), BF16)],
        scratch_shapes=[pltpu.VMEM((tm, d), F32)],
        compiler_params=_cp(1),
        name="ffn_swiglu",
    )(xb, w1, w3, w2, h, g, b)


ONES_ROWS = 16
ACC_ROWS = LANE + ONES_ROWS


def _vt_ext(v_blk):
    vt = v_blk.astype(F32).T.astype(BF16)
    return jnp.concatenate([vt, jnp.ones((ONES_ROWS, v_blk.shape[0]), BF16)], axis=0)


def _split_heads(x):
    low = lax.broadcasted_iota(I32, x.shape, 1) < CHUNK
    zero = jnp.zeros_like(x)
    return jnp.where(low, x, zero), jnp.where(low, zero, x)


_NT = (((1,), (1,)), ((), ()))


def _flash_kernel(*refs, mode, tq, lambda_init):
    if mode == "diff":
        lq1_ref, lk1_ref, lq2_ref, lk2_ref, subg_ref, q_ref, k_ref, v_ref, o_ref, *scratch = refs
    else:
        q_ref, k_ref, v_ref, o_ref, *scratch = refs
    vt_ref, m_ref, a_ref, bm_ref, acc_ref, s_ref, p_ref = scratch
    qi = pl.program_id(2)

    @pl.when(qi == 0)
    def _():
        for j in range(vt_ref.shape[0]):
            vt_ref[j] = _vt_ext(v_ref[j * tq:(j + 1) * tq, :])

    if mode == "diff":
        qs = _split_heads(q_ref[...] * jnp.asarray(CHUNK ** -0.5, BF16))
    else:
        qs = (q_ref[:, :LANE], q_ref[:, LANE:])
    m_ref[...] = jnp.full(m_ref.shape, NEG_INF, F32)
    acc_ref[...] = jnp.zeros(acc_ref.shape, F32)

    def scores(blk):
        start = pl.multiple_of(blk * tq, tq)
        kb = k_ref[pl.ds(start, tq), :]
        kks = (kb, kb) if mode == "diff" else (kb[:, :LANE], kb[:, LANE:])
        return [lax.dot_general(kks[mp], qs[mp], _NT, preferred_element_type=F32) for mp in range(2)]

    def values(blk, slot):
        vt = vt_ref[blk]
        return [jnp.dot(vt, p_ref[slot, mp], preferred_element_type=F32) for mp in range(2)]

    def stash(slot, ss):
        for mp in range(2):
            s_ref[slot, mp] = ss[mp]
            bm_ref[slot, mp] = jnp.max(ss[mp], axis=0, keepdims=True)

    def softmax(slot, masked):
        for mp in range(2):
            s = s_ref[slot, mp]
            if masked:
                kc = lax.broadcasted_iota(I32, s.shape, 0) >> 6
                qc = lax.broadcasted_iota(I32, s.shape, 1) >> 6
                s = jnp.where(kc <= qc, s, NEG_INF)
                blk_max = jnp.max(s, axis=0, keepdims=True)
            else:
                blk_max = bm_ref[slot, mp]
            m_old = m_ref[mp]
            m_new = jnp.maximum(m_old, blk_max)
            p_ref[slot, mp] = jnp.exp(s - m_new).astype(BF16)
            a_ref[slot, mp] = jnp.exp(m_old - m_new)
            m_ref[mp] = m_new

    def fold(pv, scale):
        for mp in range(2):
            acc_ref[mp] = scale[mp] * acc_ref[mp] + pv[mp]

    def trip(blk, masks, produce):
        scale = [[a_ref[u, 0], a_ref[u, 1]] for u in range(2)]
        pvs = [values(jnp.maximum(blk - 2 + u, 0), u) for u in range(2)]
        nxt = [scores(blk + len(masks) + u) for u in range(2)] if produce else []
        for u, masked in enumerate(masks):
            softmax(u, masked)
        for u in range(2):
            fold(pvs[u], scale[u])
        for u, s in enumerate(nxt):
            stash(u, s)

    def drain(blk, n):
        for u in range(n):
            fold(values(blk + u, u), [a_ref[u, 0], a_ref[u, 1]])

    for u in range(2):
        stash(u, scores(u))
    p_ref[...] = jnp.zeros(p_ref.shape, BF16)
    a_ref[...] = jnp.ones(a_ref.shape, F32)

    def step(j, carry):
        trip(2 * j, (False, False), True)
        return carry

    lax.fori_loop(0, qi >> 1, step, 0)

    @pl.when((qi & 1) == 0)
    def _():
        trip(qi, (True,), False)
        drain(qi, 1)

    @pl.when((qi & 1) == 1)
    def _():
        trip(qi - 1, (False, True), False)
        drain(qi - 1, 2)

    acc_a, acc_b = acc_ref[0], acc_ref[1]
    oa = acc_a[:LANE] / acc_a[LANE:LANE + 1]
    ob = acc_b[:LANE] / acc_b[LANE:LANE + 1]
    if mode == "diff":
        lam = (jnp.exp(jnp.sum(lq1_ref[...] * lk1_ref[...], axis=-1, keepdims=True))
               - jnp.exp(jnp.sum(lq2_ref[...] * lk2_ref[...], axis=-1, keepdims=True)) + lambda_init)
        o = (oa - lam * ob).T
        o = _rms(o, subg_ref[...], 1e-5) * (1.0 - lambda_init)
    else:
        row = lax.broadcasted_iota(I32, oa.shape, 0)
        o = jnp.where(row < CHUNK, oa, ob).T
    o_ref[...] = o.astype(o_ref.dtype)


def _flash(mode, q_arr, k_arr, v_arr, *, batch, seq, groups, q_blk0, k_blk0, v_blk0, extra=(), lambda_init=0.0,
           tq=256):
    qw = LANE if mode == "diff" else 2 * LANE
    tq = min(tq, seq // 2)
    nq = seq // tq
    assert nq % 2 == 0, "the key sweep runs two blocks ahead"
    n = batch * seq
    in_specs = [pl.BlockSpec(e.shape, lambda b, g, i: (0, 0)) for e in extra]
    in_specs += [
        pl.BlockSpec((tq, qw), lambda b, g, i: (b * nq + i, q_blk0 + g)),
        pl.BlockSpec((seq, qw), lambda b, g, i: (b, k_blk0 + g)),
        pl.BlockSpec((seq, LANE), lambda b, g, i: (b, v_blk0 + g)),
    ]
    return pl.pallas_call(
        functools.partial(_flash_kernel, mode=mode, tq=tq, lambda_init=lambda_init),
        grid=(batch, groups, nq),
        in_specs=in_specs,
        out_specs=pl.BlockSpec((tq, LANE), lambda b, g, i: (b * nq + i, g)),
        out_shape=jax.ShapeDtypeStruct((n, groups * LANE), BF16),
        scratch_shapes=[pltpu.VMEM((nq, ACC_ROWS, tq), BF16), pltpu.VMEM((2, 1, tq), F32),
                        pltpu.VMEM((2, 2, 1, tq), F32), pltpu.VMEM((2, 2, 1, tq), F32),
                        pltpu.VMEM((2, ACC_ROWS, tq), F32),
                        pltpu.VMEM((2, 2, tq, tq), F32), pltpu.VMEM((2, 2, tq, tq), BF16)],
        compiler_params=_cp(3),
        name="flash_" + mode,
    )(*extra, q_arr, k_arr, v_arr)


BAND_TQ = 4 * CHUNK
BAND_W = (CA_LEFT + 4) * CHUNK
BAND_PAD = CA_LEFT * CHUNK
BAND_NW = BAND_W // BAND_TQ


def _band_kernel(q_ref, k_ref, v_ref, bias_ref, o_ref, kpad_ref, vt_ref, *, seq):
    t = pl.program_id(2)
    npad = BAND_PAD // BAND_TQ

    @pl.when(t == 0)
    def _():
        kpad_ref[0:BAND_PAD, :] = jnp.zeros((BAND_PAD, LANE), BF16)
        kpad_ref[BAND_PAD:BAND_PAD + seq, :] = k_ref[...]
        for j in range(npad):
            vt_ref[j] = jnp.zeros((ACC_ROWS, BAND_TQ), BF16)
        for j in range(seq // BAND_TQ):
            vt_ref[npad + j] = _vt_ext(v_ref[j * BAND_TQ:(j + 1) * BAND_TQ, :])

    def run(front):
        qhs = _split_heads(q_ref[...] * jnp.asarray(CHUNK ** -0.5, BF16))
        ss = []
        for u in range(BAND_NW):
            start = pl.multiple_of((t + u) * BAND_TQ, BAND_TQ)
            kb = kpad_ref[pl.ds(start, BAND_TQ), :]
            ss.append([lax.dot_general(kb, qh, _NT, preferred_element_type=F32) for qh in qhs])
        outs = []
        for hh in range(2):
            blocks = []
            for u in range(BAND_NW):
                s = ss[u][hh] + bias_ref[0, hh, u * BAND_TQ:(u + 1) * BAND_TQ, :]
                if front:
                    invalid = lax.broadcasted_iota(I32, s.shape, 0) + (t + u) * BAND_TQ < BAND_PAD
                    s = jnp.where(invalid, NEG_INF, s)
                blocks.append(s)
            m = blocks[0].max(axis=0, keepdims=True)
            for s in blocks[1:]:
                m = jnp.maximum(m, s.max(axis=0, keepdims=True))
            acc = None
            for u, s in enumerate(blocks):
                pv = jnp.dot(vt_ref[t + u], jnp.exp(s - m).astype(BF16), preferred_element_type=F32)
                acc = pv if acc is None else acc + pv
            outs.append(acc[:LANE] / acc[LANE:LANE + 1])
        row = lax.broadcasted_iota(I32, outs[0].shape, 0)
        o_ref[...] = jnp.where(row < CHUNK, outs[0], outs[1]).T.astype(o_ref.dtype)

    pl.when(t < npad)(lambda: run(True))
    pl.when(t >= npad)(lambda: run(False))


def _band_attention(qkv, bias, *, batch, seq, pairs):
    n = batch * seq
    nt = seq // BAND_TQ
    return pl.pallas_call(
        functools.partial(_band_kernel, seq=seq),
        grid=(pairs, batch, nt),
        in_specs=[
            pl.BlockSpec((BAND_TQ, LANE), lambda g, b, t: (b * nt + t, g)),
            pl.BlockSpec((seq, LANE), lambda g, b, t: (b, pairs + g)),
            pl.BlockSpec((seq, LANE), lambda g, b, t: (b, 2 * pairs + g)),
            pl.BlockSpec((1, 2, BAND_W, BAND_TQ), lambda g, b, t: (g, 0, 0, 0)),
        ],
        out_specs=pl.BlockSpec((BAND_TQ, LANE), lambda g, b, t: (b * nt + t, g)),
        out_shape=jax.ShapeDtypeStruct((n, pairs * LANE), BF16),
        scratch_shapes=[pltpu.VMEM((seq + BAND_PAD, LANE), BF16),
                        pltpu.VMEM(((seq + BAND_PAD) // BAND_TQ, ACC_ROWS, BAND_TQ), BF16)],
        compiler_params=_cp(3),
        name="band_attention",
    )(qkv, qkv, qkv, bias)


def _band_bias_tiles(rel_bias):
    heads = rel_bias.shape[0]
    rel = np.arange(BAND_TQ + BAND_W - 1) - (BAND_W - 1) + BAND_PAD
    ext = rel_bias.astype(F32)[:, np.clip(rel, -CA_REL_CLIP, CA_REL_CLIP) + CA_REL_CLIP]
    length = BAND_TQ + BAND_W - 1
    extp = jnp.concatenate([ext, jnp.zeros((heads, 1), F32)], axis=1)
    skew = jnp.broadcast_to(extp[:, None, :], (heads, BAND_W, length + 1)).reshape(heads, -1)
    skew = skew[:, :BAND_W * length].reshape(heads, BAND_W, length)
    bias = skew[:, :, BAND_W - 1:BAND_W - 1 + BAND_TQ]
    ci = (jnp.arange(BAND_TQ) // CHUNK)[None, :]
    cj = (jnp.arange(BAND_W) // CHUNK)[:, None]
    vis = (cj >= ci) & (cj <= ci + CA_LEFT)
    bias = jnp.where(vis[None], bias, NEG_INF)
    return bias.reshape(heads // 2, 2, BAND_W, BAND_TQ)


def _mla_cq_kernel(x_ref, w_ref, g_ref, o_ref):
    c = jnp.dot(x_ref[...], w_ref[...], preferred_element_type=F32)
    o_ref[...] = _rms(c, g_ref[...], RMS_EPS).astype(o_ref.dtype)


def _mla_q_kernel(c_ref, wa_ref, wb_ref, ct_ref, st_ref, o_ref, *, scale, heads):
    c = c_ref[...]
    a = jnp.dot(c, wa_ref[...], preferred_element_type=F32)
    b = jnp.dot(c, wb_ref[...], preferred_element_type=F32)
    ct = ct_ref[...] * scale
    st = st_ref[...] * scale
    for h in range(heads):
        sl = slice(h * LANE, (h + 1) * LANE)
        o_ref[:, sl] = (a[:, sl] * ct + b[:, sl] * st).astype(o_ref.dtype)


def _mla_ckv_kernel(x_ref, w_ref, g_ref, ct_ref, st_ref, o_ref, *, rank):
    y = jnp.dot(x_ref[...], w_ref[...], preferred_element_type=F32)
    o_ref[:, :rank] = _rms(y[:, :rank], g_ref[...], RMS_EPS).astype(o_ref.dtype)
    kr = y[:, rank:rank + LANE] * ct_ref[...] + y[:, rank + LANE:rank + 2 * LANE] * st_ref[...]
    o_ref[:, rank:] = kr.astype(o_ref.dtype)


GMLP_CHUNK = 128


def _sg_in_kernel(x_ref, w_ref, g_ref, b_ref, u_ref, v_ref):
    width = u_ref.shape[1]
    hh = jax.nn.gelu(jnp.dot(x_ref[...], w_ref[...], preferred_element_type=F32))
    u_ref[...] = hh[:, :width].astype(u_ref.dtype)
    v = hh[:, width:]
    mu = jnp.mean(v, axis=-1, keepdims=True)
    vc = v - mu
    var = jnp.mean(vc * vc, axis=-1, keepdims=True)
    v_ref[...] = (vc * lax.rsqrt(var + LN_EPS) * g_ref[...] + b_ref[...]).astype(v_ref.dtype)


def _sg_mix_kernel(u_ref, v_ref, ws_ref, bs_ref, wo_ref, h_ref, g_ref, b_ref, oh_ref, ohb_ref, gated_ref, *,
                   alpha, groups):
    tm = u_ref.shape[0]
    r = lax.broadcasted_iota(I32, (GMLP_CHUNK, GMLP_CHUNK), 0) >> 6
    c = lax.broadcasted_iota(I32, (GMLP_CHUNK, GMLP_CHUNK), 1) >> 6
    vis = c <= r
    for gi in range(groups):
        w = jnp.where(vis, ws_ref[gi], 0.0).astype(BF16)
        cs = slice(gi * LANE, (gi + 1) * LANE)
        for ch in range(tm // GMLP_CHUNK):
            rs = slice(ch * GMLP_CHUNK, (ch + 1) * GMLP_CHUNK)
            mixed = jnp.dot(w, v_ref[rs, cs], preferred_element_type=F32) + bs_ref[:, cs]
            gated_ref[rs, cs] = (u_ref[rs, cs].astype(F32) * mixed).astype(BF16)
    m = jnp.dot(gated_ref[...], wo_ref[...], preferred_element_type=F32)
    y = _res_ln(h_ref[...], m, g_ref[...], b_ref[...], alpha)
    oh_ref[...] = y
    ohb_ref[...] = y.astype(BF16)


INFO_IDX, INFO_RANK, INFO_GATE = 0, 2, 4
ROW_DMA_UNROLL = 8


def _router_kernel(h_ref, w_ref, info_ref, cnt_ref, run_ref):
    i = pl.program_id(0)
    tr = h_ref.shape[0]

    @pl.when(i == 0)
    def _():
        run_ref[...] = jnp.zeros_like(run_ref)

    x = h_ref[...]
    x1 = x.astype(BF16)
    r1 = x - x1.astype(F32)
    x2 = r1.astype(BF16)
    x3 = (r1 - x2.astype(F32)).astype(BF16)
    w = w_ref[...]
    lane = lax.broadcasted_iota(I32, (tr, LANE), 1)
    t = (jnp.dot(x1, w, preferred_element_type=F32)
         + jnp.where(lane < 2 * N_EXPERTS, jnp.dot(x2, w, preferred_element_type=F32), 0.0)
         + jnp.where(lane < N_EXPERTS, jnp.dot(x3, w, preferred_element_type=F32), 0.0))
    logits = t + pltpu.roll(t, LANE - N_EXPERTS, 1) + pltpu.roll(t, LANE - 2 * N_EXPERTS, 1)
    big = jnp.float32(3e38)
    logits = jnp.where(lane < N_EXPERTS, logits, -big)

    lane_f = lane.astype(F32)
    v1 = jnp.max(logits, axis=-1, keepdims=True)
    i1 = jnp.min(jnp.where(logits == v1, lane_f, float(LANE)), axis=-1, keepdims=True)
    rest = jnp.where(lane_f == i1, -big, logits)
    v2 = jnp.max(rest, axis=-1, keepdims=True)
    i2 = jnp.min(jnp.where(rest == v2, lane_f, float(LANE)), axis=-1, keepdims=True)
    e = jnp.exp(v2 - v1)
    g1 = 1.0 / (1.0 + e)
    g2 = e / (1.0 + e)

    oh1 = jnp.where(lane_f == i1, 1.0, 0.0)
    oh2 = jnp.where(lane_f == i2, 1.0, 0.0)
    oh = oh1 + oh2
    rr = lax.broadcasted_iota(I32, (tr, tr), 0)
    cc = lax.broadcasted_iota(I32, (tr, tr), 1)
    before = jnp.where(cc < rr, 1.0, 0.0).astype(BF16)
    prior = jnp.dot(before, oh.astype(BF16), preferred_element_type=F32) + run_ref[...]
    rank1 = jnp.sum(prior * oh1, axis=-1, keepdims=True)
    rank2 = jnp.sum(prior * oh2, axis=-1, keepdims=True)
    run_ref[...] += jnp.sum(oh, axis=0, keepdims=True)
    cnt_ref[...] = run_ref[...]

    info = jnp.where(lane == INFO_IDX, i1, 0.0)
    info = jnp.where(lane == INFO_IDX + 1, i2, info)
    info = jnp.where(lane == INFO_RANK, rank1, info)
    info = jnp.where(lane == INFO_RANK + 1, rank2, info)
    info = jnp.where(lane == INFO_GATE, g1, info)
    info = jnp.where(lane == INFO_GATE + 1, g2, info)
    info_ref[...] = info


def _dispatch_kernel(pos_ref, h_ref, init_ref, xs_ref, sem):
    del init_ref
    td = h_ref.shape[0]

    def start(r, carry):
        for k in range(2):
            p = pos_ref[0, 0, 2 * r + k]
            pltpu.make_async_copy(h_ref.at[pl.ds(r, 1), :], xs_ref.at[pl.ds(p, 1), :], sem).start(priority=k)
        return carry

    lax.fori_loop(0, td, start, 0, unroll=ROW_DMA_UNROLL)
    for _ in range(2):
        pltpu.make_async_copy(h_ref, xs_ref.at[pl.ds(0, td), :], sem).wait()


def _experts_kernel(te_ref, nu_ref, x_ref, w1_ref, w3_ref, w2_ref, o_ref, xb_ref, *, fc):
    i = pl.program_id(0)
    j = pl.program_id(1)
    del te_ref

    @pl.when(j == 0)
    def _():
        o_ref[...] = jnp.zeros_like(o_ref)
        xb_ref[...] = x_ref[...].astype(BF16)

    @pl.when(i < nu_ref[0])
    def _():
        x = xb_ref[...]
        tf = w1_ref.shape[1]
        gated = []
        for c in range(tf // fc):
            sl = slice(c * fc, (c + 1) * fc)
            a = jnp.dot(x, w1_ref[:, sl].astype(BF16), preferred_element_type=F32)
            b3 = jnp.dot(x, w3_ref[:, sl].astype(BF16), preferred_element_type=F32)
            gated.append((jax.nn.silu(a) * b3).astype(BF16))
        o_ref[...] += jnp.dot(jnp.concatenate(gated, axis=1), w2_ref[...].astype(BF16),
                              preferred_element_type=F32)


def _combine_kernel(pos_ref, info_ref, h_ref, g_ref, b_ref, ys_ref, oh_ref, ohb_ref, buf_ref, sem, *, alpha):
    tc = h_ref.shape[0]

    def start(r, carry):
        for k in range(2):
            p = pos_ref[0, 0, 2 * r + k]
            pltpu.make_async_copy(ys_ref.at[pl.ds(p, 1), :], buf_ref.at[k, pl.ds(r, 1), :], sem).start(priority=k)
        return carry

    lax.fori_loop(0, tc, start, 0, unroll=ROW_DMA_UNROLL)
    for k in range(2):
        pltpu.make_async_copy(ys_ref.at[pl.ds(0, tc), :], buf_ref.at[k], sem).wait()
    info = info_ref[...]
    g1 = info[:, INFO_GATE:INFO_GATE + 1]
    g2 = info[:, INFO_GATE + 1:INFO_GATE + 2]
    m = g1 * buf_ref[0] + g2 * buf_ref[1]
    y = _res_ln(h_ref[...], m, g_ref[...], b_ref[...], alpha)
    oh_ref[...] = y
    ohb_ref[...] = y.astype(BF16)


def _split3(w):
    hi = w.astype(BF16)
    r = w - hi.astype(F32)
    mid = r.astype(BF16)
    lo = (r - mid.astype(F32)).astype(BF16)
    return hi, mid, lo


def _moe(h, w_router, w1, w3, w2, g, b, *, layer, alpha, tm=1024, tf=512, fc=256, td=256):
    n, d = h.shape
    f = w1.shape[3]
    tm = min(tm, n)
    td = min(td, n)
    tf = min(tf, f)
    row = lambda i: (i, 0)

    hi, mid, lo = _split3(w_router.astype(F32))
    wr = jnp.concatenate([hi, mid, lo, jnp.zeros((d, LANE - 3 * N_EXPERTS), BF16)], axis=1)
    tr = min(512, n)
    info, cnt = pl.pallas_call(
        _router_kernel,
        grid=(n // tr,),
        in_specs=[pl.BlockSpec((tr, d), row), _resident((d, LANE))],
        out_specs=[pl.BlockSpec((tr, LANE), row), pl.BlockSpec((1, LANE), lambda i: (0, 0))],
        out_shape=[jax.ShapeDtypeStruct((n, LANE), F32), jax.ShapeDtypeStruct((1, LANE), F32)],
        scratch_shapes=[pltpu.VMEM((1, LANE), F32)],
        compiler_params=_cp(1),
        name="moe_router",
    )(h, wr)

    idx = info[:, INFO_IDX:INFO_IDX + 2].astype(I32)
    rank = info[:, INFO_RANK:INFO_RANK + 2].astype(I32)
    counts = cnt[0, :N_EXPERTS].astype(I32)
    padded = ((counts + tm - 1) // tm) * tm
    ends = jnp.cumsum(padded)
    starts = ends - padded
    pos = starts[idx] + rank
    n_tiles = (2 * n) // tm + N_EXPERTS
    n_used = (ends[-1] // tm).astype(I32)
    tile_start = jnp.arange(n_tiles, dtype=I32) * tm
    tile_e = jnp.sum(tile_start[:, None] >= ends[None, :], axis=1).astype(I32)
    tile_e = jnp.minimum(tile_e, tile_e[jnp.maximum(n_used - 1, 0)])
    rows = n_tiles * tm
    pos_blocks = pos.reshape(n // td, 1, 2 * td)

    xs = pl.pallas_call(
        _dispatch_kernel,
        grid=(n // td,),
        in_specs=[pl.BlockSpec((1, 1, 2 * td), lambda i: (i, 0, 0), memory_space=pltpu.SMEM),
                  pl.BlockSpec((td, d), row),
                  pl.BlockSpec(memory_space=pl.ANY)],
        out_specs=pl.BlockSpec(memory_space=pl.ANY),
        out_shape=jax.ShapeDtypeStruct((rows, d), F32),
        scratch_shapes=[pltpu.SemaphoreType.DMA(())],
        input_output_aliases={2: 0},
        compiler_params=_cp(1),
        name="moe_dispatch",
    )(pos_blocks, h, jnp.zeros((rows, d), F32))

    nj = f // tf

    def x_map(i, j, te, nu):
        return (jnp.minimum(i, nu[0] - 1), 0)

    def w13_map(i, j, te, nu):
        return (layer, te[i], 0, jnp.where(i < nu[0], j, nj - 1))

    def w2_map(i, j, te, nu):
        return (layer, te[i], jnp.where(i < nu[0], j, nj - 1), 0)

    ys = pl.pallas_call(
        functools.partial(_experts_kernel, fc=fc),
        grid_spec=pltpu.PrefetchScalarGridSpec(
            num_scalar_prefetch=2,
            grid=(n_tiles, nj),
            in_specs=[pl.BlockSpec((tm, d), x_map),
                      pl.BlockSpec((None, None, d, tf), w13_map),
                      pl.BlockSpec((None, None, d, tf), w13_map),
                      pl.BlockSpec((None, None, tf, d), w2_map)],
            out_specs=pl.BlockSpec((tm, d), lambda i, j, te, nu: (i, 0)),
            scratch_shapes=[pltpu.VMEM((tm, d), BF16)],
        ),
        out_shape=jax.ShapeDtypeStruct((rows, d), F32),
        compiler_params=_cp(2),
        name="moe_experts",
    )(tile_e, n_used.reshape(1), xs, w1, w3, w2)

    tc = td
    return pl.pallas_call(
        functools.partial(_combine_kernel, alpha=alpha),
        grid=(n // tc,),
        in_specs=[pl.BlockSpec((1, 1, 2 * tc), lambda i: (i, 0, 0), memory_space=pltpu.SMEM),
                  pl.BlockSpec((tc, LANE), row),
                  pl.BlockSpec((tc, d), row),
                  _resident((1, d)), _resident((1, d)),
                  pl.BlockSpec(memory_space=pl.ANY)],
        out_specs=[pl.BlockSpec((tc, d), row), pl.BlockSpec((tc, d), row)],
        out_shape=[jax.ShapeDtypeStruct((n, d), F32), jax.ShapeDtypeStruct((n, d), BF16)],
        scratch_shapes=[pltpu.VMEM((2, tc, d), F32), pltpu.SemaphoreType.DMA(())],
        compiler_params=_cp(1),
        name="moe_combine",
    )(pos_blocks, info, h, g, b, ys)


def _diff_mixer(h, hb, wq, wk, wv, lq1, lk1, lq2, lk2, sub_g, wo, g, b, *, batch, seq, alpha, lambda_init):
    d = wq.shape[0]
    heads = d // LANE
    qkv = _linear(hb, (wq.astype(BF16), wk.astype(BF16), wv.astype(BF16)), name="diff_qkv")
    extra = tuple(a.reshape(1, -1).astype(F32) for a in (lq1, lk1, lq2, lk2, sub_g))
    o = _flash("diff", qkv, qkv, qkv, batch=batch, seq=seq, groups=heads, q_blk0=0, k_blk0=heads, v_blk0=2 * heads,
               extra=extra, lambda_init=lambda_init)
    return _proj_res_ln(o, wo.astype(BF16), h, g, b, alpha=alpha, name="diff_out")


def _band_mixer(h, hb, w_qkv, rel_bias, wo, g, b, *, batch, seq, alpha):
    d = w_qkv.shape[0]
    qkv = _linear(hb, (w_qkv.astype(BF16),), name="band_qkv")
    o = _band_attention(qkv, _band_bias_tiles(rel_bias), batch=batch, seq=seq, pairs=d // LANE)
    return _proj_res_ln(o, wo.astype(BF16), h, g, b, alpha=alpha, name="band_out")


def _mla_mixer(h, hb, w_dq, q_g, w_uq, w_dkv, kv_g, w_ukv, wo, g, b, *, batch, seq, alpha, tm=512):
    n, d = hb.shape
    q_rank = w_dq.shape[1]
    kv_rank = kv_g.shape[-1]
    heads = wo.shape[0] // CHUNK
    nope, rope = 64, 2 * ROPE_HALF
    tm = min(tm, seq)
    row = lambda i: (i, 0)
    tab = lambda i: (i % (seq // tm), 0)

    inv_freq = ROPE_THETA ** (-jnp.arange(ROPE_HALF, dtype=F32) / ROPE_HALF)
    ang = jnp.arange(seq).astype(F32)[:, None] * inv_freq[None, :]
    cos, sin = jnp.cos(ang), jnp.sin(ang)
    z = lambda w: jnp.zeros((seq, w), F32)
    ct_q = jnp.concatenate([jnp.ones((seq, nope), F32), cos, cos, z(LANE - nope - rope)], axis=1)
    st_q = jnp.concatenate([z(nope), sin, sin, z(LANE - nope - rope)], axis=1)
    ct_k = jnp.concatenate([cos, cos, z(LANE - rope)], axis=1)
    st_k = jnp.concatenate([sin, sin, z(LANE - rope)], axis=1)

    wq3 = w_uq.reshape(q_rank, heads, nope + rope)
    qn, q1, q2 = wq3[..., :nope], wq3[..., nope:nope + ROPE_HALF], wq3[..., nope + ROPE_HALF:]
    zq = lambda w: jnp.zeros((q_rank, heads, w), w_uq.dtype)
    wa = jnp.concatenate([qn, q1, q2, zq(LANE - nope - rope)], axis=2).reshape(q_rank, heads * LANE)
    wb = jnp.concatenate([zq(nope), -q2, q1, zq(LANE - nope - rope)], axis=2).reshape(q_rank, heads * LANE)

    k1, k2 = w_dkv[:, kv_rank:kv_rank + ROPE_HALF], w_dkv[:, kv_rank + ROPE_HALF:]
    zd = lambda w: jnp.zeros((d, w), w_dkv.dtype)
    wd = jnp.concatenate([w_dkv[:, :kv_rank], k1, k2, zd(LANE - rope), -k2, k1, zd(LANE - rope)], axis=1)

    wkv3 = w_ukv.reshape(kv_rank, heads, nope + CHUNK)
    zk = lambda r, w: jnp.zeros((r, heads, w), w_ukv.dtype)
    k_top = jnp.concatenate([wkv3[..., :nope], zk(kv_rank, LANE - nope)], axis=2).reshape(kv_rank, heads * LANE)
    eye = jnp.broadcast_to(jnp.eye(rope, dtype=w_ukv.dtype)[:, None, :], (rope, heads, rope))
    k_rope = jnp.concatenate([zk(rope, nope), eye, zk(rope, LANE - nope - rope)], axis=2).reshape(rope, heads * LANE)
    v_top = wkv3[..., nope:].reshape(kv_rank, heads * CHUNK)
    wkv = jnp.concatenate([
        jnp.concatenate([k_top, v_top], axis=1),
        jnp.concatenate([k_rope, jnp.zeros((rope, heads * CHUNK), w_ukv.dtype)], axis=1),
        jnp.zeros((LANE - rope, heads * (LANE + CHUNK)), w_ukv.dtype)], axis=0)

    cq = pl.pallas_call(
        _mla_cq_kernel,
        grid=(n // tm,),
        in_specs=[pl.BlockSpec((tm, d), row), _resident((d, q_rank)), _resident((1, q_rank))],
        out_specs=pl.BlockSpec((tm, q_rank), row),
        out_shape=jax.ShapeDtypeStruct((n, q_rank), BF16),
        compiler_params=_cp(1),
        name="mla_cq",
    )(hb, w_dq.astype(BF16), q_g.reshape(1, -1))

    q = pl.pallas_call(
        functools.partial(_mla_q_kernel, scale=float((nope + rope) ** -0.5), heads=heads),
        grid=(n // tm,),
        in_specs=[pl.BlockSpec((tm, q_rank), row), _resident((q_rank, heads * LANE)),
                  _resident((q_rank, heads * LANE)), pl.BlockSpec((tm, LANE), tab), pl.BlockSpec((tm, LANE), tab)],
        out_specs=pl.BlockSpec((tm, heads * LANE), row),
        out_shape=jax.ShapeDtypeStruct((n, heads * LANE), BF16),
        compiler_params=_cp(1),
        name="mla_q",
    )(cq, wa.astype(BF16), wb.astype(BF16), ct_q, st_q)

    ckr = pl.pallas_call(
        functools.partial(_mla_ckv_kernel, rank=kv_rank),
        grid=(n // tm,),
        in_specs=[pl.BlockSpec((tm, d), row), _resident((d, kv_rank + 2 * LANE)), _resident((1, kv_rank)),
                  pl.BlockSpec((tm, LANE), tab), pl.BlockSpec((tm, LANE), tab)],
        out_specs=pl.BlockSpec((tm, kv_rank + LANE), row),
        out_shape=jax.ShapeDtypeStruct((n, kv_rank + LANE), BF16),
        compiler_params=_cp(1),
        name="mla_ckv",
    )(hb, wd.astype(BF16), kv_g.reshape(1, -1), ct_k, st_k)

    kv = _linear(ckr, (wkv.astype(BF16),), name="mla_kv")
    o = _flash("mla", q, kv, kv, batch=batch, seq=seq, groups=heads // 2, q_blk0=0, k_blk0=0, v_blk0=heads)
    return _proj_res_ln(o, wo.astype(BF16), h, g, b, alpha=alpha, name="mla_out")


def _sg_mixer(h, hb, w_in, vg, vb, w_s, b_s, w_out, g, b, *, alpha, tm=512):
    n, d = hb.shape
    width = w_out.shape[0]
    groups = w_s.shape[0]
    tm = min(tm, n)
    row = lambda i: (i, 0)
    u, v = pl.pallas_call(
        _sg_in_kernel,
        grid=(n // tm,),
        in_specs=[pl.BlockSpec((tm, d), row), _resident((d, 2 * width)), _resident((1, width)),
                  _resident((1, width))],
        out_specs=[pl.BlockSpec((tm, width), row), pl.BlockSpec((tm, width), row)],
        out_shape=[jax.ShapeDtypeStruct((n, width), BF16), jax.ShapeDtypeStruct((n, width), BF16)],
        compiler_params=_cp(1),
        name="sg_in",
    )(hb, w_in.astype(BF16), vg.reshape(1, -1), vb.reshape(1, -1))
    bs_full = jnp.repeat(b_s.T.astype(F32), width // groups, axis=1)
    return pl.pallas_call(
        functools.partial(_sg_mix_kernel, alpha=alpha, groups=groups),
        grid=(n // tm,),
        in_specs=[pl.BlockSpec((tm, width), row), pl.BlockSpec((tm, width), row),
                  _resident((groups, GMLP_CHUNK, GMLP_CHUNK)), _resident((GMLP_CHUNK, width)),
                  _resident((width, d)), pl.BlockSpec((tm, d), row), _resident((1, d)), _resident((1, d))],
        out_specs=[pl.BlockSpec((tm, d), row), pl.BlockSpec((tm, d), row)],
        out_shape=[jax.ShapeDtypeStruct((n, d), F32), jax.ShapeDtypeStruct((n, d), BF16)],
        scratch_shapes=[pltpu.VMEM((tm, width), BF16)],
        compiler_params=_cp(1),
        name="sg_mix",
    )(u, v, w_s.astype(F32), bs_full, w_out.astype(BF16), h, g, b)


def kernel(x, ln_mix_g, ln_mix_b, ln_ffn_g, ln_ffn_b, diff_wq, diff_wk, diff_wv, diff_lq1, diff_lk1, diff_lq2, diff_lk2, diff_sub_g, diff_wo, ca_w_qkv, ca_rel_bias, ca_wo, mla_w_dq, mla_q_norm_g, mla_w_uq, mla_w_dkv, mla_kv_norm_g, mla_w_ukv, mla_wo, sg_w_in, sg_v_norm_g, sg_v_norm_b, sg_w_s, sg_b_s, sg_w_out, ffn_w1, ffn_w3, ffn_w2, moe_w_router, moe_w1, moe_w3, moe_w2):
    batch, seq, d = x.shape
    depth = ln_mix_g.shape[0]
    alpha = float((2 * depth) ** 0.25)
    h = x.reshape(batch * seq, d).astype(F32)
    hb = h
    vec = lambda a: a.reshape(1, -1).astype(F32)
    for i in range(depth):
        kind, j = i % 4, i // 4
        mg, mb = vec(ln_mix_g[i]), vec(ln_mix_b[i])
        if kind == 0:
            h, hb = _diff_mixer(h, hb, diff_wq[j], diff_wk[j], diff_wv[j], diff_lq1[j], diff_lk1[j], diff_lq2[j],
                                diff_lk2[j], diff_sub_g[j], diff_wo[j], mg, mb, batch=batch, seq=seq, alpha=alpha,
                                lambda_init=0.8 - 0.6 * math.exp(-0.3 * i))
        elif kind == 1:
            h, hb = _band_mixer(h, hb, ca_w_qkv[j], ca_rel_bias[j], ca_wo[j], mg, mb, batch=batch, seq=seq,
                                alpha=alpha)
        elif kind == 2:
            h, hb = _mla_mixer(h, hb, mla_w_dq[j], mla_q_norm_g[j], mla_w_uq[j], mla_w_dkv[j], mla_kv_norm_g[j],
                               mla_w_ukv[j], mla_wo[j], mg, mb, batch=batch, seq=seq, alpha=alpha)
        else:
            h, hb = _sg_mixer(h, hb, sg_w_in[j], sg_v_norm_g[j], sg_v_norm_b[j], sg_w_s[j], sg_b_s[j], sg_w_out[j],
                              mg, mb, alpha=alpha)
        fg, fb = vec(ln_ffn_g[i]), vec(ln_ffn_b[i])
        if i % 2 == 0:
            h, hb = _ffn(hb, ffn_w1[i // 2].astype(BF16), ffn_w3[i // 2].astype(BF16), ffn_w2[i // 2].astype(BF16),
                         h, fg, fb, alpha=alpha)
        else:
            h, hb = _moe(h, moe_w_router[i // 2], moe_w1, moe_w3, moe_w2, fg, fb, layer=i // 2, alpha=alpha)
    return h.reshape(batch, seq, d).astype(x.dtype)
```

```python
import functools
import math

import jax
import jax.numpy as jnp
import numpy as np
from jax import lax
from jax.experimental import pallas as pl
from jax.experimental.pallas import tpu as pltpu

F32 = jnp.float32
BF16 = jnp.bfloat16
I32 = jnp.int32

LANE = 128
VMEM_LIMIT = 56 * 1024 * 1024

CHUNK = 64
N_EXPERTS = 8
LN_EPS = 1e-5
RMS_EPS = 1e-6
NEG_INF = -1e30
ROPE_THETA = 10000.0
CA_LEFT = 8
CA_REL_CLIP = 128
ROPE_HALF = 16


def _cp(n_axes, vmem=VMEM_LIMIT):
    return pltpu.CompilerParams(dimension_semantics=("arbitrary",) * n_axes, vmem_limit_bytes=vmem)


def _resident(shape):
    nd = len(shape)
    return pl.BlockSpec(shape, lambda *_: (0,) * nd, pipeline_mode=pl.Buffered(1))


def _res_ln(h, m, g, b, alpha):
    z = alpha * h + m
    mu = jnp.mean(z, axis=-1, keepdims=True)
    zc = z - mu
    var = jnp.mean(zc * zc, axis=-1, keepdims=True)
    return zc * lax.rsqrt(var + LN_EPS) * g + b


def _rms(x, g, eps):
    ms = jnp.mean(x * x, axis=-1, keepdims=True)
    return x * lax.rsqrt(ms + eps) * g


def _linear_kernel(x_ref, *refs):
    *w_refs, o_ref = refs
    x = x_ref[...].astype(BF16)
    col = 0
    for w_ref in w_refs:
        n = w_ref.shape[1]
        o_ref[:, col:col + n] = jnp.dot(x, w_ref[...], preferred_element_type=F32).astype(o_ref.dtype)
        col += n


def _linear(x, ws, *, tm=512, out_dtype=BF16, name="linear"):
    m, k = x.shape
    n = sum(w.shape[1] for w in ws)
    tm = min(tm, m)
    return pl.pallas_call(
        _linear_kernel,
        grid=(m // tm,),
        in_specs=[pl.BlockSpec((tm, k), lambda i: (i, 0))] + [_resident(w.shape) for w in ws],
        out_specs=pl.BlockSpec((tm, n), lambda i: (i, 0)),
        out_shape=jax.ShapeDtypeStruct((m, n), out_dtype),
        compiler_params=_cp(1),
        name=name,
    )(x, *ws)


def _proj_res_ln_kernel(a_ref, w_ref, h_ref, g_ref, b_ref, oh_ref, ohb_ref, *, alpha):
    m = jnp.dot(a_ref[...], w_ref[...], preferred_element_type=F32)
    y = _res_ln(h_ref[...], m, g_ref[...], b_ref[...], alpha)
    oh_ref[...] = y
    ohb_ref[...] = y.astype(BF16)


def _proj_res_ln(a, w, h, g, b, *, alpha, tm=512, name="proj_res_ln"):
    m, k = a.shape
    d = w.shape[1]
    tm = min(tm, m)
    row = lambda i: (i, 0)
    return pl.pallas_call(
        functools.partial(_proj_res_ln_kernel, alpha=alpha),
        grid=(m // tm,),
        in_specs=[pl.BlockSpec((tm, k), row), _resident((k, d)), pl.BlockSpec((tm, d), row),
                  _resident((1, d)), _resident((1, d))],
        out_specs=[pl.BlockSpec((tm, d), row), pl.BlockSpec((tm, d), row)],
        out_shape=[jax.ShapeDtypeStruct((m, d), F32), jax.ShapeDtypeStruct((m, d), BF16)],
        compiler_params=_cp(1),
        name=name,
    )(a, w, h, g, b)


def _ffn_kernel(x_ref, w1_ref, w3_ref, w2_ref, h_ref, g_ref, b_ref, oh_ref, ohb_ref, acc_ref, *, alpha, fc):
    x = x_ref[...]
    f = w1_ref.shape[1]
    for c in range(f // fc):
        sl = slice(c * fc, (c + 1) * fc)
        a = jnp.dot(x, w1_ref[:, sl], preferred_element_type=F32)
        b3 = jnp.dot(x, w3_ref[:, sl], preferred_element_type=F32)
        gated = (jax.nn.silu(a) * b3).astype(BF16)
        part = jnp.dot(gated, w2_ref[sl, :], preferred_element_type=F32)
        if c == 0:
            acc_ref[...] = part
        else:
            acc_ref[...] += part
    y = _res_ln(h_ref[...], acc_ref[...], g_ref[...], b_ref[...], alpha)
    oh_ref[...] = y
    ohb_ref[...] = y.astype(BF16)


def _ffn(xb, w1, w3, w2, h, g, b, *, alpha, tm=512, fc=256):
    m, d = xb.shape
    f = w1.shape[1]
    tm = min(tm, m)
    row = lambda i: (i, 0)
    return pl.pallas_call(
        functools.partial(_ffn_kernel, alpha=alpha, fc=fc),
        grid=(m // tm,),
        in_specs=[pl.BlockSpec((tm, d), row), _resident((d, f)), _resident((d, f)), _resident((f, d)),
                  pl.BlockSpec((tm, d), row), _resident((1, d)), _resident((1, d))],
        out_specs=[pl.BlockSpec((tm, d), row), pl.BlockSpec((tm, d), row)],
        out_shape=[jax.ShapeDtypeStruct((m, d), F32), jax.ShapeDtypeStruct((m, d), BF16)],
        scratch_shapes=[pltpu.VMEM((tm, d), F32)],
        compiler_params=_cp(1),
        name="ffn_swiglu",
    )(xb, w1, w3, w2, h, g, b)


ONES_ROWS = 16
ACC_ROWS = LANE + ONES_ROWS


def _vt_ext(v_blk):
    vt = v_blk.astype(F32).T.astype(BF16)
    return jnp.concatenate([vt, jnp.ones((ONES_ROWS, v_blk.shape[0]), BF16)], axis=0)


def _split_heads(x):
    low = lax.broadcasted_iota(I32, x.shape, 1) < CHUNK
    zero = jnp.zeros_like(x)
    return jnp.where(low, x, zero), jnp.where(low, zero, x)


_NT = (((1,), (1,)), ((), ()))


def _flash_kernel(*refs, mode, tq, lambda_init):
    if mode == "diff":
        lq1_ref, lk1_ref, lq2_ref, lk2_ref, subg_ref, q_ref, k_ref, v_ref, o_ref, *scratch = refs
    else:
        q_ref, k_ref, v_ref, o_ref, *scratch = refs
    vt_ref, m_ref, a_ref, bm_ref, acc_ref, s_ref, p_ref = scratch
    qi = pl.program_id(2)
    tk = tq // 2

    @pl.when(qi == 0)
    def _():
        for j in range(vt_ref.shape[0]):
            vt_ref[j] = _vt_ext(v_ref[j * tk:(j + 1) * tk, :])

    if mode == "diff":
        qs = _split_heads(q_ref[...] * jnp.asarray(CHUNK ** -0.5, BF16))
    else:
        qs = (q_ref[:, :LANE], q_ref[:, LANE:])
    m_ref[...] = jnp.full(m_ref.shape, NEG_INF, F32)
    acc_ref[...] = jnp.zeros(acc_ref.shape, F32)

    def scores(blk):
        start = pl.multiple_of(blk * tk, tk)
        kb = k_ref[pl.ds(start, tk), :]
        kks = (kb, kb) if mode == "diff" else (kb[:, :LANE], kb[:, LANE:])
        return [lax.dot_general(kks[mp], qs[mp], _NT, preferred_element_type=F32) for mp in range(2)]

    def values(blk, slot):
        vt = vt_ref[blk]
        return [jnp.dot(vt, p_ref[slot, mp], preferred_element_type=F32) for mp in range(2)]

    def stash(slot, ss):
        for mp in range(2):
            s_ref[slot, mp] = ss[mp]
            bm_ref[slot, mp] = jnp.max(ss[mp], axis=0, keepdims=True)

    def softmax(slot, masked):
        for mp in range(2):
            s = s_ref[slot, mp]
            if masked:
                kc = (lax.broadcasted_iota(I32, s.shape, 0) + slot * tk) >> 6
                qc = lax.broadcasted_iota(I32, s.shape, 1) >> 6
                s = jnp.where(kc <= qc, s, NEG_INF)
                blk_max = jnp.max(s, axis=0, keepdims=True)
            else:
                blk_max = bm_ref[slot, mp]
            m_old = m_ref[mp]
            m_new = jnp.maximum(m_old, blk_max)
            p_ref[slot, mp] = jnp.exp(s - m_new).astype(BF16)
            a_ref[slot, mp] = jnp.exp(m_old - m_new)
            m_ref[mp] = m_new

    def fold(pv, scale):
        for mp in range(2):
            acc_ref[mp] = scale[mp] * acc_ref[mp] + pv[mp]

    def trip(blk, masked, produce):
        scale = [[a_ref[u, 0], a_ref[u, 1]] for u in range(2)]
        pvs = [values(jnp.maximum(blk - 2 + u, 0), u) for u in range(2)]
        nxt = [scores(blk + 2 + u) for u in range(2)] if produce else []
        for u in range(2):
            softmax(u, masked)
        for u in range(2):
            fold(pvs[u], scale[u])
        for u, s in enumerate(nxt):
            stash(u, s)

    for u in range(2):
        stash(u, scores(u))
    p_ref[...] = jnp.zeros(p_ref.shape, BF16)
    a_ref[...] = jnp.ones(a_ref.shape, F32)

    def step(j, carry):
        trip(2 * j, False, True)
        return carry

    lax.fori_loop(0, qi, step, 0)
    trip(2 * qi, True, False)
    for u in range(2):
        fold(values(2 * qi + u, u), [a_ref[u, 0], a_ref[u, 1]])

    acc_a, acc_b = acc_ref[0], acc_ref[1]
    oa = acc_a[:LANE] / acc_a[LANE:LANE + 1]
    ob = acc_b[:LANE] / acc_b[LANE:LANE + 1]
    if mode == "diff":
        lam = (jnp.exp(jnp.sum(lq1_ref[...] * lk1_ref[...], axis=-1, keepdims=True))
               - jnp.exp(jnp.sum(lq2_ref[...] * lk2_ref[...], axis=-1, keepdims=True)) + lambda_init)
        o = (oa - lam * ob).T
        o = _rms(o, subg_ref[...], 1e-5) * (1.0 - lambda_init)
    else:
        row = lax.broadcasted_iota(I32, oa.shape, 0)
        o = jnp.where(row < CHUNK, oa, ob).T
    o_ref[...] = o.astype(o_ref.dtype)


def _flash(mode, q_arr, k_arr, v_arr, *, batch, seq, groups, q_blk0, k_blk0, v_blk0, extra=(), lambda_init=0.0,
           tq=512):
    qw = LANE if mode == "diff" else 2 * LANE
    tq = min(tq, seq)
    tk = tq // 2
    nq = seq // tq
    n = batch * seq
    in_specs = [pl.BlockSpec(e.shape, lambda b, g, i: (0, 0)) for e in extra]
    in_specs += [
        pl.BlockSpec((tq, qw), lambda b, g, i: (b * nq + i, q_blk0 + g)),
        pl.BlockSpec((seq, qw), lambda b, g, i: (b, k_blk0 + g)),
        pl.BlockSpec((seq, LANE), lambda b, g, i: (b, v_blk0 + g)),
    ]
    return pl.pallas_call(
        functools.partial(_flash_kernel, mode=mode, tq=tq, lambda_init=lambda_init),
        grid=(batch, groups, nq),
        in_specs=in_specs,
        out_specs=pl.BlockSpec((tq, LANE), lambda b, g, i: (b * nq + i, g)),
        out_shape=jax.ShapeDtypeStruct((n, groups * LANE), BF16),
        scratch_shapes=[pltpu.VMEM((seq // tk, ACC_ROWS, tk), BF16), pltpu.VMEM((2, 1, tq), F32),
                        pltpu.VMEM((2, 2, 1, tq), F32), pltpu.VMEM((2, 2, 1, tq), F32),
                        pltpu.VMEM((2, ACC_ROWS, tq), F32),
                        pltpu.VMEM((2, 2, tk, tq), F32), pltpu.VMEM((2, 2, tk, tq), BF16)],
        compiler_params=_cp(3),
        name="flash_" + mode,
    )(*extra, q_arr, k_arr, v_arr)


BAND_TQ = 4 * CHUNK
BAND_W = (CA_LEFT + 4) * CHUNK
BAND_PAD = CA_LEFT * CHUNK
BAND_NW = BAND_W // BAND_TQ


def _band_kernel(q_ref, qn_ref, k_ref, v_ref, bias_ref, o_ref, kpad_ref, vt_ref, s_ref, bm_ref, *, seq):
    t = pl.program_id(2)
    nt = pl.num_programs(2)
    npad = BAND_PAD // BAND_TQ

    def produce(qr, tile, slot, front):
        qhs = _split_heads(qr[...] * jnp.asarray(CHUNK ** -0.5, BF16))
        tops = [None, None]
        for u in range(BAND_NW):
            start = pl.multiple_of((tile + u) * BAND_TQ, BAND_TQ)
            kb = kpad_ref[pl.ds(start, BAND_TQ), :]
            for hh in range(2):
                s = lax.dot_general(kb, qhs[hh], _NT, preferred_element_type=F32)
                s = s + bias_ref[0, hh, u * BAND_TQ:(u + 1) * BAND_TQ, :]
                if front:
                    invalid = lax.broadcasted_iota(I32, s.shape, 0) + (tile + u) * BAND_TQ < BAND_PAD
                    s = jnp.where(invalid, NEG_INF, s)
                s_ref[slot, hh, u] = s
                top = jnp.max(s, axis=0, keepdims=True)
                tops[hh] = top if tops[hh] is None else jnp.maximum(tops[hh], top)
        for hh in range(2):
            bm_ref[slot, hh] = tops[hh]

    @pl.when(t == 0)
    def _():
        kpad_ref[0:BAND_PAD, :] = jnp.zeros((BAND_PAD, LANE), BF16)
        kpad_ref[BAND_PAD:BAND_PAD + seq, :] = k_ref[...]
        for j in range(npad):
            vt_ref[j] = jnp.zeros((ACC_ROWS, BAND_TQ), BF16)
        for j in range(seq // BAND_TQ):
            vt_ref[npad + j] = _vt_ext(v_ref[j * BAND_TQ:(j + 1) * BAND_TQ, :])
        produce(q_ref, 0, 0, True)

    cur = t & 1
    nxt_tile = jnp.minimum(t + 1, nt - 1)

    def consume():
        outs = []
        for hh in range(2):
            m = bm_ref[cur, hh]
            acc = None
            for u in range(BAND_NW):
                p = jnp.exp(s_ref[cur, hh, u] - m).astype(BF16)
                pv = jnp.dot(vt_ref[t + u], p, preferred_element_type=F32)
                acc = pv if acc is None else acc + pv
            outs.append(acc[:LANE] / acc[LANE:LANE + 1])
        row = lax.broadcasted_iota(I32, outs[0].shape, 0)
        o_ref[...] = jnp.where(row < CHUNK, outs[0], outs[1]).T.astype(o_ref.dtype)

    def run(front):
        consume()
        produce(qn_ref, nxt_tile, 1 - cur, front)

    pl.when(t + 1 < npad)(lambda: run(True))
    pl.when(t + 1 >= npad)(lambda: run(False))


def _band_attention(qkv, bias, *, batch, seq, pairs):
    n = batch * seq
    nt = seq // BAND_TQ
    return pl.pallas_call(
        functools.partial(_band_kernel, seq=seq),
        grid=(pairs, batch, nt),
        in_specs=[
            pl.BlockSpec((BAND_TQ, LANE), lambda g, b, t: (b * nt + t, g)),
            pl.BlockSpec((BAND_TQ, LANE), lambda g, b, t: (b * nt + jnp.minimum(t + 1, nt - 1), g)),
            pl.BlockSpec((seq, LANE), lambda g, b, t: (b, pairs + g)),
            pl.BlockSpec((seq, LANE), lambda g, b, t: (b, 2 * pairs + g)),
            pl.BlockSpec((1, 2, BAND_W, BAND_TQ), lambda g, b, t: (g, 0, 0, 0)),
        ],
        out_specs=pl.BlockSpec((BAND_TQ, LANE), lambda g, b, t: (b * nt + t, g)),
        out_shape=jax.ShapeDtypeStruct((n, pairs * LANE), BF16),
        scratch_shapes=[pltpu.VMEM((seq + BAND_PAD, LANE), BF16),
                        pltpu.VMEM(((seq + BAND_PAD) // BAND_TQ, ACC_ROWS, BAND_TQ), BF16),
                        pltpu.VMEM((2, 2, BAND_NW, BAND_TQ, BAND_TQ), F32),
                        pltpu.VMEM((2, 2, 1, BAND_TQ), F32)],
        compiler_params=_cp(3),
        name="band_attention",
    )(qkv, qkv, qkv, qkv, bias)


def _band_bias_tiles(rel_bias):
    heads = rel_bias.shape[0]
    rel = np.arange(BAND_TQ + BAND_W - 1) - (BAND_W - 1) + BAND_PAD
    ext = rel_bias.astype(F32)[:, np.clip(rel, -CA_REL_CLIP, CA_REL_CLIP) + CA_REL_CLIP]
    length = BAND_TQ + BAND_W - 1
    extp = jnp.concatenate([ext, jnp.zeros((heads, 1), F32)], axis=1)
    skew = jnp.broadcast_to(extp[:, None, :], (heads, BAND_W, length + 1)).reshape(heads, -1)
    skew = skew[:, :BAND_W * length].reshape(heads, BAND_W, length)
    bias = skew[:, :, BAND_W - 1:BAND_W - 1 + BAND_TQ]
    ci = (jnp.arange(BAND_TQ) // CHUNK)[None, :]
    cj = (jnp.arange(BAND_W) // CHUNK)[:, None]
    vis = (cj >= ci) & (cj <= ci + CA_LEFT)
    bias = jnp.where(vis[None], bias, NEG_INF)
    return bias.reshape(heads // 2, 2, BAND_W, BAND_TQ)


def _mla_cq_kernel(x_ref, w_ref, g_ref, o_ref):
    c = jnp.dot(x_ref[...], w_ref[...], preferred_element_type=F32)
    o_ref[...] = _rms(c, g_ref[...], RMS_EPS).astype(o_ref.dtype)


def _mla_q_kernel(c_ref, wa_ref, wb_ref, ct_ref, st_ref, o_ref, *, scale, heads):
    c = c_ref[...]
    a = jnp.dot(c, wa_ref[...], preferred_element_type=F32)
    b = jnp.dot(c, wb_ref[...], preferred_element_type=F32)
    ct = ct_ref[...] * scale
    st = st_ref[...] * scale
    for h in range(heads):
        sl = slice(h * LANE, (h + 1) * LANE)
        o_ref[:, sl] = (a[:, sl] * ct + b[:, sl] * st).astype(o_ref.dtype)


def _mla_ckv_kernel(x_ref, w_ref, g_ref, ct_ref, st_ref, o_ref, *, rank):
    y = jnp.dot(x_ref[...], w_ref[...], preferred_element_type=F32)
    o_ref[:, :rank] = _rms(y[:, :rank], g_ref[...], RMS_EPS).astype(o_ref.dtype)
    kr = y[:, rank:rank + LANE] * ct_ref[...] + y[:, rank + LANE:rank + 2 * LANE] * st_ref[...]
    o_ref[:, rank:] = kr.astype(o_ref.dtype)


GMLP_CHUNK = 128


def _sg_in_kernel(x_ref, w_ref, g_ref, b_ref, u_ref, v_ref):
    width = u_ref.shape[1]
    hh = jax.nn.gelu(jnp.dot(x_ref[...], w_ref[...], preferred_element_type=F32))
    u_ref[...] = hh[:, :width].astype(u_ref.dtype)
    v = hh[:, width:]
    mu = jnp.mean(v, axis=-1, keepdims=True)
    vc = v - mu
    var = jnp.mean(vc * vc, axis=-1, keepdims=True)
    v_ref[...] = (vc * lax.rsqrt(var + LN_EPS) * g_ref[...] + b_ref[...]).astype(v_ref.dtype)


def _sg_mix_kernel(u_ref, v_ref, ws_ref, bs_ref, wo_ref, h_ref, g_ref, b_ref, oh_ref, ohb_ref, gated_ref, *,
                   alpha, groups):
    tm = u_ref.shape[0]
    r = lax.broadcasted_iota(I32, (GMLP_CHUNK, GMLP_CHUNK), 0) >> 6
    c = lax.broadcasted_iota(I32, (GMLP_CHUNK, GMLP_CHUNK), 1) >> 6
    vis = c <= r
    for gi in range(groups):
        w = jnp.where(vis, ws_ref[gi], 0.0).astype(BF16)
        cs = slice(gi * LANE, (gi + 1) * LANE)
        for ch in range(tm // GMLP_CHUNK):
            rs = slice(ch * GMLP_CHUNK, (ch + 1) * GMLP_CHUNK)
            mixed = jnp.dot(w, v_ref[rs, cs], preferred_element_type=F32) + bs_ref[:, cs]
            gated_ref[rs, cs] = (u_ref[rs, cs].astype(F32) * mixed).astype(BF16)
    m = jnp.dot(gated_ref[...], wo_ref[...], preferred_element_type=F32)
    y = _res_ln(h_ref[...], m, g_ref[...], b_ref[...], alpha)
    oh_ref[...] = y
    ohb_ref[...] = y.astype(BF16)


INFO_IDX, INFO_RANK, INFO_GATE = 0, 2, 4
ROW_DMA_UNROLL = 8


def _router_kernel(h_ref, w_ref, info_ref, cnt_ref, run_ref):
    i = pl.program_id(0)
    tr = h_ref.shape[0]

    @pl.when(i == 0)
    def _():
        run_ref[...] = jnp.zeros_like(run_ref)

    x = h_ref[...]
    x1 = x.astype(BF16)
    r1 = x - x1.astype(F32)
    x2 = r1.astype(BF16)
    x3 = (r1 - x2.astype(F32)).astype(BF16)
    w = w_ref[...]
    lane = lax.broadcasted_iota(I32, (tr, LANE), 1)
    t = (jnp.dot(x1, w, preferred_element_type=F32)
         + jnp.where(lane < 2 * N_EXPERTS, jnp.dot(x2, w, preferred_element_type=F32), 0.0)
         + jnp.where(lane < N_EXPERTS, jnp.dot(x3, w, preferred_element_type=F32), 0.0))
    logits = t + pltpu.roll(t, LANE - N_EXPERTS, 1) + pltpu.roll(t, LANE - 2 * N_EXPERTS, 1)
    big = jnp.float32(3e38)
    logits = jnp.where(lane < N_EXPERTS, logits, -big)

    lane_f = lane.astype(F32)
    v1 = jnp.max(logits, axis=-1, keepdims=True)
    i1 = jnp.min(jnp.where(logits == v1, lane_f, float(LANE)), axis=-1, keepdims=True)
    rest = jnp.where(lane_f == i1, -big, logits)
    v2 = jnp.max(rest, axis=-1, keepdims=True)
    i2 = jnp.min(jnp.where(rest == v2, lane_f, float(LANE)), axis=-1, keepdims=True)
    e = jnp.exp(v2 - v1)
    g1 = 1.0 / (1.0 + e)
    g2 = e / (1.0 + e)

    oh1 = jnp.where(lane_f == i1, 1.0, 0.0)
    oh2 = jnp.where(lane_f == i2, 1.0, 0.0)
    oh = oh1 + oh2
    rr = lax.broadcasted_iota(I32, (tr, tr), 0)
    cc = lax.broadcasted_iota(I32, (tr, tr), 1)
    before = jnp.where(cc < rr, 1.0, 0.0).astype(BF16)
    prior = jnp.dot(before, oh.astype(BF16), preferred_element_type=F32) + run_ref[...]
    rank1 = jnp.sum(prior * oh1, axis=-1, keepdims=True)
    rank2 = jnp.sum(prior * oh2, axis=-1, keepdims=True)
    run_ref[...] += jnp.sum(oh, axis=0, keepdims=True)
    cnt_ref[...] = run_ref[...]

    info = jnp.where(lane == INFO_IDX, i1, 0.0)
    info = jnp.where(lane == INFO_IDX + 1, i2, info)
    info = jnp.where(lane == INFO_RANK, rank1, info)
    info = jnp.where(lane == INFO_RANK + 1, rank2, info)
    info = jnp.where(lane == INFO_GATE, g1, info)
    info = jnp.where(lane == INFO_GATE + 1, g2, info)
    info_ref[...] = info


def _dispatch_kernel(pos_ref, h_ref, init_ref, xs_ref, sem):
    del init_ref
    td = h_ref.shape[0]

    def start(r, carry):
        for k in range(2):
            p = pos_ref[0, 0, 2 * r + k]
            pltpu.make_async_copy(h_ref.at[pl.ds(r, 1), :], xs_ref.at[pl.ds(p, 1), :], sem).start(priority=k)
        return carry

    lax.fori_loop(0, td, start, 0, unroll=ROW_DMA_UNROLL)
    for _ in range(2):
        pltpu.make_async_copy(h_ref, xs_ref.at[pl.ds(0, td), :], sem).wait()


def _experts_kernel(te_ref, nu_ref, x_ref, w1_ref, w3_ref, w2_ref, o_ref, xb_ref, *, fc):
    i = pl.program_id(0)
    j = pl.program_id(1)
    del te_ref

    @pl.when(j == 0)
    def _():
        o_ref[...] = jnp.zeros_like(o_ref)
        xb_ref[...] = x_ref[...].astype(BF16)

    @pl.when(i < nu_ref[0])
    def _():
        x = xb_ref[...]
        tf = w1_ref.shape[1]
        gated = []
        for c in range(tf // fc):
            sl = slice(c * fc, (c + 1) * fc)
            a = jnp.dot(x, w1_ref[:, sl].astype(BF16), preferred_element_type=F32)
            b3 = jnp.dot(x, w3_ref[:, sl].astype(BF16), preferred_element_type=F32)
            gated.append((jax.nn.silu(a) * b3).astype(BF16))
        o_ref[...] += jnp.dot(jnp.concatenate(gated, axis=1), w2_ref[...].astype(BF16),
                              preferred_element_type=F32)


def _combine_kernel(pos_ref, info_ref, h_ref, g_ref, b_ref, ys_ref, oh_ref, ohb_ref, buf_ref, sem, *, alpha):
    tc = h_ref.shape[0]

    def start(r, carry):
        for k in range(2):
            p = pos_ref[0, 0, 2 * r + k]
            pltpu.make_async_copy(ys_ref.at[pl.ds(p, 1), :], buf_ref.at[k, pl.ds(r, 1), :], sem).start(priority=k)
        return carry

    lax.fori_loop(0, tc, start, 0, unroll=ROW_DMA_UNROLL)
    for k in range(2):
        pltpu.make_async_copy(ys_ref.at[pl.ds(0, tc), :], buf_ref.at[k], sem).wait()
    info = info_ref[...]
    g1 = info[:, INFO_GATE:INFO_GATE + 1]
    g2 = info[:, INFO_GATE + 1:INFO_GATE + 2]
    m = g1 * buf_ref[0] + g2 * buf_ref[1]
    y = _res_ln(h_ref[...], m, g_ref[...], b_ref[...], alpha)
    oh_ref[...] = y
    ohb_ref[...] = y.astype(BF16)


def _split3(w):
    hi = w.astype(BF16)
    r = w - hi.astype(F32)
    mid = r.astype(BF16)
    lo = (r - mid.astype(F32)).astype(BF16)
    return hi, mid, lo


def _moe(h, w_router, w1, w3, w2, g, b, *, layer, alpha, tm=1024, tf=512, fc=256, td=256):
    n, d = h.shape
    f = w1.shape[3]
    tm = min(tm, n)
    td = min(td, n)
    tf = min(tf, f)
    row = lambda i: (i, 0)

    hi, mid, lo = _split3(w_router.astype(F32))
    wr = jnp.concatenate([hi, mid, lo, jnp.zeros((d, LANE - 3 * N_EXPERTS), BF16)], axis=1)
    tr = min(512, n)
    info, cnt = pl.pallas_call(
        _router_kernel,
        grid=(n // tr,),
        in_specs=[pl.BlockSpec((tr, d), row), _resident((d, LANE))],
        out_specs=[pl.BlockSpec((tr, LANE), row), pl.BlockSpec((1, LANE), lambda i: (0, 0))],
        out_shape=[jax.ShapeDtypeStruct((n, LANE), F32), jax.ShapeDtypeStruct((1, LANE), F32)],
        scratch_shapes=[pltpu.VMEM((1, LANE), F32)],
        compiler_params=_cp(1),
        name="moe_router",
    )(h, wr)

    idx = info[:, INFO_IDX:INFO_IDX + 2].astype(I32)
    rank = info[:, INFO_RANK:INFO_RANK + 2].astype(I32)
    counts = cnt[0, :N_EXPERTS].astype(I32)
    padded = ((counts + tm - 1) // tm) * tm
    ends = jnp.cumsum(padded)
    starts = ends - padded
    pos = starts[idx] + rank
    n_tiles = (2 * n) // tm + N_EXPERTS
    n_used = (ends[-1] // tm).astype(I32)
    tile_start = jnp.arange(n_tiles, dtype=I32) * tm
    tile_e = jnp.sum(tile_start[:, None] >= ends[None, :], axis=1).astype(I32)
    tile_e = jnp.minimum(tile_e, tile_e[jnp.maximum(n_used - 1, 0)])
    rows = n_tiles * tm
    pos_blocks = pos.reshape(n // td, 1, 2 * td)

    xs = pl.pallas_call(
        _dispatch_kernel,
        grid=(n // td,),
        in_specs=[pl.BlockSpec((1, 1, 2 * td), lambda i: (i, 0, 0), memory_space=pltpu.SMEM),
                  pl.BlockSpec((td, d), row),
                  pl.BlockSpec(memory_space=pl.ANY)],
        out_specs=pl.BlockSpec(memory_space=pl.ANY),
        out_shape=jax.ShapeDtypeStruct((rows, d), F32),
        scratch_shapes=[pltpu.SemaphoreType.DMA(())],
        input_output_aliases={2: 0},
        compiler_params=_cp(1),
        name="moe_dispatch",
    )(pos_blocks, h, jnp.zeros((rows, d), F32))

    nj = f // tf

    def x_map(i, j, te, nu):
        return (jnp.minimum(i, nu[0] - 1), 0)

    def w13_map(i, j, te, nu):
        return (layer, te[i], 0, jnp.where(i < nu[0], j, nj - 1))

    def w2_map(i, j, te, nu):
        return (layer, te[i], jnp.where(i < nu[0], j, nj - 1), 0)

    ys = pl.pallas_call(
        functools.partial(_experts_kernel, fc=fc),
        grid_spec=pltpu.PrefetchScalarGridSpec(
            num_scalar_prefetch=2,
            grid=(n_tiles, nj),
            in_specs=[pl.BlockSpec((tm, d), x_map),
                      pl.BlockSpec((None, None, d, tf), w13_map),
                      pl.BlockSpec((None, None, d, tf), w13_map),
                      pl.BlockSpec((None, None, tf, d), w2_map)],
            out_specs=pl.BlockSpec((tm, d), lambda i, j, te, nu: (i, 0)),
            scratch_shapes=[pltpu.VMEM((tm, d), BF16)],
        ),
        out_shape=jax.ShapeDtypeStruct((rows, d), F32),
        compiler_params=_cp(2),
        name="moe_experts",
    )(tile_e, n_used.reshape(1), xs, w1, w3, w2)

    tc = td
    return pl.pallas_call(
        functools.partial(_combine_kernel, alpha=alpha),
        grid=(n // tc,),
        in_specs=[pl.BlockSpec((1, 1, 2 * tc), lambda i: (i, 0, 0), memory_space=pltpu.SMEM),
                  pl.BlockSpec((tc, LANE), row),
                  pl.BlockSpec((tc, d), row),
                  _resident((1, d)), _resident((1, d)),
                  pl.BlockSpec(memory_space=pl.ANY)],
        out_specs=[pl.BlockSpec((tc, d), row), pl.BlockSpec((tc, d), row)],
        out_shape=[jax.ShapeDtypeStruct((n, d), F32), jax.ShapeDtypeStruct((n, d), BF16)],
        scratch_shapes=[pltpu.VMEM((2, tc, d), F32), pltpu.SemaphoreType.DMA(())],
        compiler_params=_cp(1),
        name="moe_combine",
    )(pos_blocks, info, h, g, b, ys)


def _diff_mixer(h, hb, wq, wk, wv, lq1, lk1, lq2, lk2, sub_g, wo, g, b, *, batch, seq, alpha, lambda_init):
    d = wq.shape[0]
    heads = d // LANE
    qkv = _linear(hb, (wq.astype(BF16), wk.astype(BF16), wv.astype(BF16)), name="diff_qkv")
    extra = tuple(a.reshape(1, -1).astype(F32) for a in (lq1, lk1, lq2, lk2, sub_g))
    o = _flash("diff", qkv, qkv, qkv, batch=batch, seq=seq, groups=heads, q_blk0=0, k_blk0=heads, v_blk0=2 * heads,
               extra=extra, lambda_init=lambda_init)
    return _proj_res_ln(o, wo.astype(BF16), h, g, b, alpha=alpha, name="diff_out")


def _band_mixer(h, hb, w_qkv, rel_bias, wo, g, b, *, batch, seq, alpha):
    d = w_qkv.shape[0]
    qkv = _linear(hb, (w_qkv.astype(BF16),), name="band_qkv")
    o = _band_attention(qkv, _band_bias_tiles(rel_bias), batch=batch, seq=seq, pairs=d // LANE)
    return _proj_res_ln(o, wo.astype(BF16), h, g, b, alpha=alpha, name="band_out")


def _mla_mixer(h, hb, w_dq, q_g, w_uq, w_dkv, kv_g, w_ukv, wo, g, b, *, batch, seq, alpha, tm=512):
    n, d = hb.shape
    q_rank = w_dq.shape[1]
    kv_rank = kv_g.shape[-1]
    heads = wo.shape[0] // CHUNK
    nope, rope = 64, 2 * ROPE_HALF
    tm = min(tm, seq)
    row = lambda i: (i, 0)
    tab = lambda i: (i % (seq // tm), 0)

    inv_freq = ROPE_THETA ** (-jnp.arange(ROPE_HALF, dtype=F32) / ROPE_HALF)
    ang = jnp.arange(seq).astype(F32)[:, None] * inv_freq[None, :]
    cos, sin = jnp.cos(ang), jnp.sin(ang)
    z = lambda w: jnp.zeros((seq, w), F32)
    ct_q = jnp.concatenate([jnp.ones((seq, nope), F32), cos, cos, z(LANE - nope - rope)], axis=1)
    st_q = jnp.concatenate([z(nope), sin, sin, z(LANE - nope - rope)], axis=1)
    ct_k = jnp.concatenate([cos, cos, z(LANE - rope)], axis=1)
    st_k = jnp.concatenate([sin, sin, z(LANE - rope)], axis=1)

    wq3 = w_uq.reshape(q_rank, heads, nope + rope)
    qn, q1, q2 = wq3[..., :nope], wq3[..., nope:nope + ROPE_HALF], wq3[..., nope + ROPE_HALF:]
    zq = lambda w: jnp.zeros((q_rank, heads, w), w_uq.dtype)
    wa = jnp.concatenate([qn, q1, q2, zq(LANE - nope - rope)], axis=2).reshape(q_rank, heads * LANE)
    wb = jnp.concatenate([zq(nope), -q2, q1, zq(LANE - nope - rope)], axis=2).reshape(q_rank, heads * LANE)

    k1, k2 = w_dkv[:, kv_rank:kv_rank + ROPE_HALF], w_dkv[:, kv_rank + ROPE_HALF:]
    zd = lambda w: jnp.zeros((d, w), w_dkv.dtype)
    wd = jnp.concatenate([w_dkv[:, :kv_rank], k1, k2, zd(LANE - rope), -k2, k1, zd(LANE - rope)], axis=1)

    wkv3 = w_ukv.reshape(kv_rank, heads, nope + CHUNK)
    zk = lambda r, w: jnp.zeros((r, heads, w), w_ukv.dtype)
    k_top = jnp.concatenate([wkv3[..., :nope], zk(kv_rank, LANE - nope)], axis=2).reshape(kv_rank, heads * LANE)
    eye = jnp.broadcast_to(jnp.eye(rope, dtype=w_ukv.dtype)[:, None, :], (rope, heads, rope))
    k_rope = jnp.concatenate([zk(rope, nope), eye, zk(rope, LANE - nope - rope)], axis=2).reshape(rope, heads * LANE)
    v_top = wkv3[..., nope:].reshape(kv_rank, heads * CHUNK)
    wkv = jnp.concatenate([
        jnp.concatenate([k_top, v_top], axis=1),
        jnp.concatenate([k_rope, jnp.zeros((rope, heads * CHUNK), w_ukv.dtype)], axis=1),
        jnp.zeros((LANE - rope, heads * (LANE + CHUNK)), w_ukv.dtype)], axis=0)

    cq = pl.pallas_call(
        _mla_cq_kernel,
        grid=(n // tm,),
        in_specs=[pl.BlockSpec((tm, d), row), _resident((d, q_rank)), _resident((1, q_rank))],
        out_specs=pl.BlockSpec((tm, q_rank), row),
        out_shape=jax.ShapeDtypeStruct((n, q_rank), BF16),
        compiler_params=_cp(1),
        name="mla_cq",
    )(hb, w_dq.astype(BF16), q_g.reshape(1, -1))

    q = pl.pallas_call(
        functools.partial(_mla_q_kernel, scale=float((nope + rope) ** -0.5), heads=heads),
        grid=(n // tm,),
        in_specs=[pl.BlockSpec((tm, q_rank), row), _resident((q_rank, heads * LANE)),
                  _resident((q_rank, heads * LANE)), pl.BlockSpec((tm, LANE), tab), pl.BlockSpec((tm, LANE), tab)],
        out_specs=pl.BlockSpec((tm, heads * LANE), row),
        out_shape=jax.ShapeDtypeStruct((n, heads * LANE), BF16),
        compiler_params=_cp(1),
        name="mla_q",
    )(cq, wa.astype(BF16), wb.astype(BF16), ct_q, st_q)

    ckr = pl.pallas_call(
        functools.partial(_mla_ckv_kernel, rank=kv_rank),
        grid=(n // tm,),
        in_specs=[pl.BlockSpec((tm, d), row), _resident((d, kv_rank + 2 * LANE)), _resident((1, kv_rank)),
                  pl.BlockSpec((tm, LANE), tab), pl.BlockSpec((tm, LANE), tab)],
        out_specs=pl.BlockSpec((tm, kv_rank + LANE), row),
        out_shape=jax.ShapeDtypeStruct((n, kv_rank + LANE), BF16),
        compiler_params=_cp(1),
        name="mla_ckv",
    )(hb, wd.astype(BF16), kv_g.reshape(1, -1), ct_k, st_k)

    kv = _linear(ckr, (wkv.astype(BF16),), name="mla_kv")
    o = _flash("mla", q, kv, kv, batch=batch, seq=seq, groups=heads // 2, q_blk0=0, k_blk0=0, v_blk0=heads)
    return _proj_res_ln(o, wo.astype(BF16), h, g, b, alpha=alpha, name="mla_out")


def _sg_mixer(h, hb, w_in, vg, vb, w_s, b_s, w_out, g, b, *, alpha, tm=512):
    n, d = hb.shape
    width = w_out.shape[0]
    groups = w_s.shape[0]
    tm = min(tm, n)
    row = lambda i: (i, 0)
    u, v = pl.pallas_call(
        _sg_in_kernel,
        grid=(n // tm,),
        in_specs=[pl.BlockSpec((tm, d), row), _resident((d, 2 * width)), _resident((1, width)),
                  _resident((1, width))],
        out_specs=[pl.BlockSpec((tm, width), row), pl.BlockSpec((tm, width), row)],
        out_shape=[jax.ShapeDtypeStruct((n, width), BF16), jax.ShapeDtypeStruct((n, width), BF16)],
        compiler_params=_cp(1),
        name="sg_in",
    )(hb, w_in.astype(BF16), vg.reshape(1, -1), vb.reshape(1, -1))
    bs_full = jnp.repeat(b_s.T.astype(F32), width // groups, axis=1)
    return pl.pallas_call(
        functools.partial(_sg_mix_kernel, alpha=alpha, groups=groups),
        grid=(n // tm,),
        in_specs=[pl.BlockSpec((tm, width), row), pl.BlockSpec((tm, width), row),
                  _resident((groups, GMLP_CHUNK, GMLP_CHUNK)), _resident((GMLP_CHUNK, width)),
                  _resident((width, d)), pl.BlockSpec((tm, d), row), _resident((1, d)), _resident((1, d))],
        out_specs=[pl.BlockSpec((tm, d), row), pl.BlockSpec((tm, d), row)],
        out_shape=[jax.ShapeDtypeStruct((n, d), F32), jax.ShapeDtypeStruct((n, d), BF16)],
        scratch_shapes=[pltpu.VMEM((tm, width), BF16)],
        compiler_params=_cp(1),
        name="sg_mix",
    )(u, v, w_s.astype(F32), bs_full, w_out.astype(BF16), h, g, b)


def kernel(x, ln_mix_g, ln_mix_b, ln_ffn_g, ln_ffn_b, diff_wq, diff_wk, diff_wv, diff_lq1, diff_lk1, diff_lq2, diff_lk2, diff_sub_g, diff_wo, ca_w_qkv, ca_rel_bias, ca_wo, mla_w_dq, mla_q_norm_g, mla_w_uq, mla_w_dkv, mla_kv_norm_g, mla_w_ukv, mla_wo, sg_w_in, sg_v_norm_g, sg_v_norm_b, sg_w_s, sg_b_s, sg_w_out, ffn_w1, ffn_w3, ffn_w2, moe_w_router, moe_w1, moe_w3, moe_w2):
    batch, seq, d = x.shape
    depth = ln_mix_g.shape[0]
    alpha = float((2 * depth) ** 0.25)
    h = x.reshape(batch * seq, d).astype(F32)
    hb = h
    vec = lambda a: a.reshape(1, -1).astype(F32)
    for i in range(depth):
        kind, j = i % 4, i // 4
        mg, mb = vec(ln_mix_g[i]), vec(ln_mix_b[i])
        if kind == 0:
            h, hb = _diff_mixer(h, hb, diff_wq[j], diff_wk[j], diff_wv[j], diff_lq1[j], diff_lk1[j], diff_lq2[j],
                                diff_lk2[j], diff_sub_g[j], diff_wo[j], mg, mb, batch=batch, seq=seq, alpha=alpha,
                                lambda_init=0.8 - 0.6 * math.exp(-0.3 * i))
        elif kind == 1:
            h, hb = _band_mixer(h, hb, ca_w_qkv[j], ca_rel_bias[j], ca_wo[j], mg, mb, batch=batch, seq=seq,
                                alpha=alpha)
        elif kind == 2:
            h, hb = _mla_mixer(h, hb, mla_w_dq[j], mla_q_norm_g[j], mla_w_uq[j], mla_w_dkv[j], mla_kv_norm_g[j],
                               mla_w_ukv[j], mla_wo[j], mg, mb, batch=batch, seq=seq, alpha=alpha)
        else:
            h, hb = _sg_mixer(h, hb, sg_w_in[j], sg_v_norm_g[j], sg_v_norm_b[j], sg_w_s[j], sg_b_s[j], sg_w_out[j],
                              mg, mb, alpha=alpha)
        fg, fb = vec(ln_ffn_g[i]), vec(ln_ffn_b[i])
        if i % 2 == 0:
            h, hb = _ffn(hb, ffn_w1[i // 2].astype(BF16), ffn_w3[i // 2].astype(BF16), ffn_w2[i // 2].astype(BF16),
                         h, fg, fb, alpha=alpha)
        else:
            h, hb = _moe(h, moe_w_router[i // 2], moe_w1, moe_w3, moe_w2, fg, fb, layer=i // 2, alpha=alpha)
    return h.reshape(batch, seq, d).astype(x.dtype)
```

```python
import functools
import math

import jax
import jax.numpy as jnp
import numpy as np
from jax import lax
from jax.experimental import pallas as pl
from jax.experimental.pallas import tpu as pltpu

F32 = jnp.float32
BF16 = jnp.bfloat16
I32 = jnp.int32

LANE = 128
VMEM_LIMIT = 56 * 1024 * 1024

CHUNK = 64
N_EXPERTS = 8
LN_EPS = 1e-5
RMS_EPS = 1e-6
NEG_INF = -1e30
ROPE_THETA = 10000.0
CA_LEFT = 8
CA_REL_CLIP = 128
ROPE_HALF = 16


def _cp(n_axes, vmem=VMEM_LIMIT):
    return pltpu.CompilerParams(dimension_semantics=("arbitrary",) * n_axes, vmem_limit_bytes=vmem)


def _resident(shape):
    nd = len(shape)
    return pl.BlockSpec(shape, lambda *_: (0,) * nd, pipeline_mode=pl.Buffered(1))


def _res_ln(h, m, g, b, alpha):
    z = alpha * h + m
    mu = jnp.mean(z, axis=-1, keepdims=True)
    zc = z - mu
    var = jnp.mean(zc * zc, axis=-1, keepdims=True)
    return zc * lax.rsqrt(var + LN_EPS) * g + b


def _rms(x, g, eps):
    ms = jnp.mean(x * x, axis=-1, keepdims=True)
    return x * lax.rsqrt(ms + eps) * g


def _linear_kernel(x_ref, *refs):
    *w_refs, o_ref = refs
    x = x_ref[...].astype(BF16)
    col = 0
    for w_ref in w_refs:
        n = w_ref.shape[1]
        o_ref[:, col:col + n] = jnp.dot(x, w_ref[...], preferred_element_type=F32).astype(o_ref.dtype)
        col += n


def _linear(x, ws, *, tm=512, out_dtype=BF16, name="linear"):
    m, k = x.shape
    n = sum(w.shape[1] for w in ws)
    tm = min(tm, m)
    return pl.pallas_call(
        _linear_kernel,
        grid=(m // tm,),
        in_specs=[pl.BlockSpec((tm, k), lambda i: (i, 0))] + [_resident(w.shape) for w in ws],
        out_specs=pl.BlockSpec((tm, n), lambda i: (i, 0)),
        out_shape=jax.ShapeDtypeStruct((m, n), out_dtype),
        compiler_params=_cp(1),
        name=name,
    )(x, *ws)


def _proj_res_ln_kernel(a_ref, w_ref, h_ref, g_ref, b_ref, oh_ref, ohb_ref, *, alpha):
    m = jnp.dot(a_ref[...], w_ref[...], preferred_element_type=F32)
    y = _res_ln(h_ref[...], m, g_ref[...], b_ref[...], alpha)
    oh_ref[...] = y
    ohb_ref[...] = y.astype(BF16)


def _proj_res_ln(a, w, h, g, b, *, alpha, tm=512, name="proj_res_ln"):
    m, k = a.shape
    d = w.shape[1]
    tm = min(tm, m)
    row = lambda i: (i, 0)
    return pl.pallas_call(
        functools.partial(_proj_res_ln_kernel, alpha=alpha),
        grid=(m // tm,),
        in_specs=[pl.BlockSpec((tm, k), row), _resident((k, d)), pl.BlockSpec((tm, d), row),
                  _resident((1, d)), _resident((1, d))],
        out_specs=[pl.BlockSpec((tm, d), row), pl.BlockSpec((tm, d), row)],
        out_shape=[jax.ShapeDtypeStruct((m, d), F32), jax.ShapeDtypeStruct((m, d), BF16)],
        compiler_params=_cp(1),
        name=name,
    )(a, w, h, g, b)


def _mix_ffn_kernel(a_ref, wo_ref, mg_ref, mb_ref, w1_ref, w3_ref, w2_ref, h_ref, g_ref, b_ref, oh_ref, ohb_ref,
                    hm_ref, xb_ref, acc_ref, *, alpha, fc):
    mix = jnp.dot(a_ref[...], wo_ref[...], preferred_element_type=F32)
    hm = _res_ln(h_ref[...], mix, mg_ref[...], mb_ref[...], alpha)
    hm_ref[...] = hm
    xb_ref[...] = hm.astype(BF16)
    x = xb_ref[...]
    f = w1_ref.shape[1]
    for c in range(f // fc):
        sl = slice(c * fc, (c + 1) * fc)
        a = jnp.dot(x, w1_ref[:, sl], preferred_element_type=F32)
        b3 = jnp.dot(x, w3_ref[:, sl], preferred_element_type=F32)
        gated = (jax.nn.silu(a) * b3).astype(BF16)
        part = jnp.dot(gated, w2_ref[sl, :], preferred_element_type=F32)
        if c == 0:
            acc_ref[...] = part
        else:
            acc_ref[...] += part
    y = _res_ln(hm_ref[...], acc_ref[...], g_ref[...], b_ref[...], alpha)
    oh_ref[...] = y
    ohb_ref[...] = y.astype(BF16)


def _mix_ffn(a, wo, mg, mb, w1, w3, w2, h, g, b, *, alpha, tm=512, fc=256):
    m, d = h.shape
    k = a.shape[1]
    f = w1.shape[1]
    tm = min(tm, m)
    row = lambda i: (i, 0)
    vecs = [_resident((1, d)), _resident((1, d))]
    return pl.pallas_call(
        functools.partial(_mix_ffn_kernel, alpha=alpha, fc=fc),
        grid=(m // tm,),
        in_specs=[pl.BlockSpec((tm, k), row), _resident((k, d))] + vecs
                 + [_resident((d, f)), _resident((d, f)), _resident((f, d)), pl.BlockSpec((tm, d), row)] + vecs,
        out_specs=[pl.BlockSpec((tm, d), row), pl.BlockSpec((tm, d), row)],
        out_shape=[jax.ShapeDtypeStruct((m, d), F32), jax.ShapeDtypeStruct((m, d), BF16)],
        scratch_shapes=[pltpu.VMEM((tm, d), F32), pltpu.VMEM((tm, d), BF16), pltpu.VMEM((tm, d), F32)],
        compiler_params=_cp(1),
        name="mix_ffn",
    )(a, wo, mg, mb, w1, w3, w2, h, g, b)


ONES_ROWS = 16
ACC_ROWS = LANE + ONES_ROWS


def _vt_ext(v_blk):
    vt = v_blk.astype(F32).T.astype(BF16)
    return jnp.concatenate([vt, jnp.ones((ONES_ROWS, v_blk.shape[0]), BF16)], axis=0)


def _split_heads(x):
    low = lax.broadcasted_iota(I32, x.shape, 1) < CHUNK
    zero = jnp.zeros_like(x)
    return jnp.where(low, x, zero), jnp.where(low, zero, x)


_NT = (((1,), (1,)), ((), ()))


def _flash_kernel(*refs, mode, tq, gs, lambda_init):
    if mode == "diff":
        lq1_ref, lk1_ref, lq2_ref, lk2_ref, subg_ref, q_ref, k_ref, v_ref, o_ref, *scratch = refs
    else:
        q_ref, k_ref, v_ref, o_ref, *scratch = refs
    vt_ref, m_ref, a_ref, bm_ref, acc_ref, s_ref, p_ref = scratch
    qi = pl.program_id(2)
    tk = tq // 2
    nmap = 2 * gs
    lanes = lambda i: slice(i * LANE, (i + 1) * LANE)

    @pl.when(qi == 0)
    def _():
        for j in range(vt_ref.shape[0]):
            for g in range(gs):
                vt_ref[j, g] = _vt_ext(v_ref[j * tk:(j + 1) * tk, lanes(g)])

    qs = []
    for g in range(gs):
        if mode == "diff":
            qs += list(_split_heads(q_ref[:, lanes(g)] * jnp.asarray(CHUNK ** -0.5, BF16)))
        else:
            qs += [q_ref[:, lanes(2 * g)], q_ref[:, lanes(2 * g + 1)]]
    m_ref[...] = jnp.full(m_ref.shape, NEG_INF, F32)
    acc_ref[...] = jnp.zeros(acc_ref.shape, F32)

    def scores(blk):
        start = pl.multiple_of(blk * tk, tk)
        kb = k_ref[pl.ds(start, tk), :]
        kks = [kb[:, lanes(mp // 2 if mode == "diff" else mp)] for mp in range(nmap)]
        return [lax.dot_general(kks[mp], qs[mp], _NT, preferred_element_type=F32) for mp in range(nmap)]

    def values(blk, slot):
        return [jnp.dot(vt_ref[blk, mp // 2], p_ref[slot, mp], preferred_element_type=F32) for mp in range(nmap)]

    def stash(slot, ss):
        for mp in range(nmap):
            s_ref[slot, mp] = ss[mp]
            bm_ref[slot, mp] = jnp.max(ss[mp], axis=0, keepdims=True)

    def softmax(slot, masked):
        for mp in range(nmap):
            s = s_ref[slot, mp]
            if masked:
                kc = (lax.broadcasted_iota(I32, s.shape, 0) + slot * tk) >> 6
                qc = lax.broadcasted_iota(I32, s.shape, 1) >> 6
                s = jnp.where(kc <= qc, s, NEG_INF)
                blk_max = jnp.max(s, axis=0, keepdims=True)
            else:
                blk_max = bm_ref[slot, mp]
            m_old = m_ref[mp]
            m_new = jnp.maximum(m_old, blk_max)
            p_ref[slot, mp] = jnp.exp(s - m_new).astype(BF16)
            a_ref[slot, mp] = jnp.exp(m_old - m_new)
            m_ref[mp] = m_new

    def fold(pv, scale):
        for mp in range(nmap):
            acc_ref[mp] = scale[mp] * acc_ref[mp] + pv[mp]

    def trip(blk, masked, produce):
        scale = [[a_ref[u, mp] for mp in range(nmap)] for u in range(2)]
        pvs = [values(jnp.maximum(blk - 2 + u, 0), u) for u in range(2)]
        nxt = [scores(blk + 2 + u) for u in range(2)] if produce else []
        for u in range(2):
            softmax(u, masked)
        for u in range(2):
            fold(pvs[u], scale[u])
        for u, s in enumerate(nxt):
            stash(u, s)

    for u in range(2):
        stash(u, scores(u))
    p_ref[...] = jnp.zeros(p_ref.shape, BF16)
    a_ref[...] = jnp.ones(a_ref.shape, F32)

    def step(j, carry):
        trip(2 * j, False, True)
        return carry

    lax.fori_loop(0, qi, step, 0)
    trip(2 * qi, True, False)
    for u in range(2):
        fold(values(2 * qi + u, u), [a_ref[u, mp] for mp in range(nmap)])

    for g in range(gs):
        acc_a, acc_b = acc_ref[2 * g], acc_ref[2 * g + 1]
        oa = acc_a[:LANE] / acc_a[LANE:LANE + 1]
        ob = acc_b[:LANE] / acc_b[LANE:LANE + 1]
        if mode == "diff":
            lam = (jnp.exp(jnp.sum(lq1_ref[...] * lk1_ref[...], axis=-1, keepdims=True))
                   - jnp.exp(jnp.sum(lq2_ref[...] * lk2_ref[...], axis=-1, keepdims=True)) + lambda_init)
            o = (oa - lam * ob).T
            o = _rms(o, subg_ref[...], 1e-5) * (1.0 - lambda_init)
        else:
            row = lax.broadcasted_iota(I32, oa.shape, 0)
            o = jnp.where(row < CHUNK, oa, ob).T
        o_ref[:, lanes(g)] = o.astype(o_ref.dtype)


def _flash(mode, q_arr, k_arr, v_arr, *, batch, seq, groups, q_blk0, k_blk0, v_blk0, extra=(), lambda_init=0.0,
           tq=512, gs=2):
    qw = (LANE if mode == "diff" else 2 * LANE) * gs
    tq = min(tq, seq)
    tk = tq // 2
    nq = seq // tq
    n = batch * seq
    nmap = 2 * gs
    assert groups % gs == 0 and q_blk0 % gs == 0 and k_blk0 % gs == 0 and v_blk0 % gs == 0
    qb, kb, vb = q_blk0 // gs, k_blk0 // gs, v_blk0 // gs
    in_specs = [pl.BlockSpec(e.shape, lambda b, g, i: (0, 0)) for e in extra]
    in_specs += [
        pl.BlockSpec((tq, qw), lambda b, g, i: (b * nq + i, qb + g)),
        pl.BlockSpec((seq, qw), lambda b, g, i: (b, kb + g)),
        pl.BlockSpec((seq, gs * LANE), lambda b, g, i: (b, vb + g)),
    ]
    return pl.pallas_call(
        functools.partial(_flash_kernel, mode=mode, tq=tq, gs=gs, lambda_init=lambda_init),
        grid=(batch, groups // gs, nq),
        in_specs=in_specs,
        out_specs=pl.BlockSpec((tq, gs * LANE), lambda b, g, i: (b * nq + i, g)),
        out_shape=jax.ShapeDtypeStruct((n, groups * LANE), BF16),
        scratch_shapes=[pltpu.VMEM((seq // tk, gs, ACC_ROWS, tk), BF16), pltpu.VMEM((nmap, 1, tq), F32),
                        pltpu.VMEM((2, nmap, 1, tq), F32), pltpu.VMEM((2, nmap, 1, tq), F32),
                        pltpu.VMEM((nmap, ACC_ROWS, tq), F32),
                        pltpu.VMEM((2, nmap, tk, tq), F32), pltpu.VMEM((2, nmap, tk, tq), BF16)],
        compiler_params=_cp(3),
        name="flash_" + mode,
    )(*extra, q_arr, k_arr, v_arr)


BAND_TQ = 4 * CHUNK
BAND_W = (CA_LEFT + 4) * CHUNK
BAND_PAD = CA_LEFT * CHUNK
BAND_NW = BAND_W // BAND_TQ
BAND_EXT = 1024


def _band_kernel(q_ref, qn_ref, k_ref, v_ref, ext_ref, o_ref, kpad_ref, vt_ref, s_ref, bm_ref, bias_ref, *, seq):
    t = pl.program_id(2)
    nt = pl.num_programs(2)
    npad = BAND_PAD // BAND_TQ

    def produce(qr, tile, slot, front):
        qhs = _split_heads(qr[...] * jnp.asarray(CHUNK ** -0.5, BF16))
        tops = [None, None]
        for u in range(BAND_NW):
            start = pl.multiple_of((tile + u) * BAND_TQ, BAND_TQ)
            kb = kpad_ref[pl.ds(start, BAND_TQ), :]
            for hh in range(2):
                s = lax.dot_general(kb, qhs[hh], _NT, preferred_element_type=F32)
                s = s + bias_ref[hh, u * BAND_TQ:(u + 1) * BAND_TQ, :]
                if front:
                    invalid = lax.broadcasted_iota(I32, s.shape, 0) + (tile + u) * BAND_TQ < BAND_PAD
                    s = jnp.where(invalid, NEG_INF, s)
                s_ref[slot, hh, u] = s
                top = jnp.max(s, axis=0, keepdims=True)
                tops[hh] = top if tops[hh] is None else jnp.maximum(tops[hh], top)
        for hh in range(2):
            bm_ref[slot, hh] = tops[hh]

    @pl.when((t == 0) & (pl.program_id(1) == 0))
    def _():
        ci = lax.broadcasted_iota(I32, (BAND_W, BAND_TQ), 1) >> 6
        cj = lax.broadcasted_iota(I32, (BAND_W, BAND_TQ), 0) >> 6
        vis = (cj >= ci) & (cj <= ci + CA_LEFT)
        for hh in range(2):
            rows = jnp.broadcast_to(ext_ref[0, hh], (BAND_W, BAND_EXT))
            skew = pltpu.roll(rows, BAND_EXT - (BAND_W - 1), 1, stride=1, stride_axis=0)
            bias_ref[hh] = jnp.where(vis, skew[:, :BAND_TQ], NEG_INF)

    @pl.when(t == 0)
    def _():
        kpad_ref[0:BAND_PAD, :] = jnp.zeros((BAND_PAD, LANE), BF16)
        kpad_ref[BAND_PAD:BAND_PAD + seq, :] = k_ref[...]
        for j in range(npad):
            vt_ref[j] = jnp.zeros((ACC_ROWS, BAND_TQ), BF16)
        for j in range(seq // BAND_TQ):
            vt_ref[npad + j] = _vt_ext(v_ref[j * BAND_TQ:(j + 1) * BAND_TQ, :])
        produce(q_ref, 0, 0, True)

    cur = t & 1
    nxt_tile = jnp.minimum(t + 1, nt - 1)

    def consume():
        outs = []
        for hh in range(2):
            m = bm_ref[cur, hh]
            acc = None
            for u in range(BAND_NW):
                p = jnp.exp(s_ref[cur, hh, u] - m).astype(BF16)
                pv = jnp.dot(vt_ref[t + u], p, preferred_element_type=F32)
                acc = pv if acc is None else acc + pv
            outs.append(acc[:LANE] / acc[LANE:LANE + 1])
        row = lax.broadcasted_iota(I32, outs[0].shape, 0)
        o_ref[...] = jnp.where(row < CHUNK, outs[0], outs[1]).T.astype(o_ref.dtype)

    def run(front):
        consume()
        produce(qn_ref, nxt_tile, 1 - cur, front)

    pl.when(t + 1 < npad)(lambda: run(True))
    pl.when(t + 1 >= npad)(lambda: run(False))


def _band_attention(qkv, ext, *, batch, seq, pairs):
    n = batch * seq
    nt = seq // BAND_TQ
    return pl.pallas_call(
        functools.partial(_band_kernel, seq=seq),
        grid=(pairs, batch, nt),
        in_specs=[
            pl.BlockSpec((BAND_TQ, LANE), lambda g, b, t: (b * nt + t, g)),
            pl.BlockSpec((BAND_TQ, LANE), lambda g, b, t: (b * nt + jnp.minimum(t + 1, nt - 1), g)),
            pl.BlockSpec((seq, LANE), lambda g, b, t: (b, pairs + g)),
            pl.BlockSpec((seq, LANE), lambda g, b, t: (b, 2 * pairs + g)),
            pl.BlockSpec((1, 2, 1, BAND_EXT), lambda g, b, t: (g, 0, 0, 0)),
        ],
        out_specs=pl.BlockSpec((BAND_TQ, LANE), lambda g, b, t: (b * nt + t, g)),
        out_shape=jax.ShapeDtypeStruct((n, pairs * LANE), BF16),
        scratch_shapes=[pltpu.VMEM((seq + BAND_PAD, LANE), BF16),
                        pltpu.VMEM(((seq + BAND_PAD) // BAND_TQ, ACC_ROWS, BAND_TQ), BF16),
                        pltpu.VMEM((2, 2, BAND_NW, BAND_TQ, BAND_TQ), F32),
                        pltpu.VMEM((2, 2, 1, BAND_TQ), F32),
                        pltpu.VMEM((2, BAND_W, BAND_TQ), F32)],
        compiler_params=_cp(3),
        name="band_attention",
    )(qkv, qkv, qkv, qkv, ext)


def _band_bias_table(rel_bias):
    heads = rel_bias.shape[0]
    rel = np.minimum(np.arange(BAND_EXT), BAND_TQ + BAND_W - 2) - (BAND_W - 1) + BAND_PAD
    ext = rel_bias.astype(F32)[:, np.clip(rel, -CA_REL_CLIP, CA_REL_CLIP) + CA_REL_CLIP]
    return ext.reshape(heads // 2, 2, 1, BAND_EXT)


def _mla_cq_kernel(x_ref, w_ref, g_ref, o_ref):
    c = jnp.dot(x_ref[...], w_ref[...], preferred_element_type=F32)
    o_ref[...] = _rms(c, g_ref[...], RMS_EPS).astype(o_ref.dtype)


def _mla_q_kernel(c_ref, wa_ref, wb_ref, ct_ref, st_ref, o_ref, *, scale, heads):
    c = c_ref[...]
    a = jnp.dot(c, wa_ref[...], preferred_element_type=F32)
    b = jnp.dot(c, wb_ref[...], preferred_element_type=F32)
    ct = ct_ref[...] * scale
    st = st_ref[...] * scale
    for h in range(heads):
        sl = slice(h * LANE, (h + 1) * LANE)
        o_ref[:, sl] = (a[:, sl] * ct + b[:, sl] * st).astype(o_ref.dtype)


def _mla_ckv_kernel(x_ref, w_ref, g_ref, ct_ref, st_ref, o_ref, *, rank):
    y = jnp.dot(x_ref[...], w_ref[...], preferred_element_type=F32)
    o_ref[:, :rank] = _rms(y[:, :rank], g_ref[...], RMS_EPS).astype(o_ref.dtype)
    kr = y[:, rank:rank + LANE] * ct_ref[...] + y[:, rank + LANE:rank + 2 * LANE] * st_ref[...]
    o_ref[:, rank:] = kr.astype(o_ref.dtype)


GMLP_CHUNK = 128


def _sg_in_kernel(x_ref, w_ref, g_ref, b_ref, u_ref, v_ref):
    width = u_ref.shape[1]
    hh = jax.nn.gelu(jnp.dot(x_ref[...], w_ref[...], preferred_element_type=F32))
    u_ref[...] = hh[:, :width].astype(u_ref.dtype)
    v = hh[:, width:]
    mu = jnp.mean(v, axis=-1, keepdims=True)
    vc = v - mu
    var = jnp.mean(vc * vc, axis=-1, keepdims=True)
    v_ref[...] = (vc * lax.rsqrt(var + LN_EPS) * g_ref[...] + b_ref[...]).astype(v_ref.dtype)


def _sg_mix_kernel(u_ref, v_ref, ws_ref, bs_ref, wo_ref, h_ref, g_ref, b_ref, oh_ref, ohb_ref, gated_ref, *,
                   alpha, groups):
    tm = u_ref.shape[0]
    r = lax.broadcasted_iota(I32, (GMLP_CHUNK, GMLP_CHUNK), 0) >> 6
    c = lax.broadcasted_iota(I32, (GMLP_CHUNK, GMLP_CHUNK), 1) >> 6
    vis = c <= r
    for gi in range(groups):
        w = jnp.where(vis, ws_ref[gi], 0.0).astype(BF16)
        cs = slice(gi * LANE, (gi + 1) * LANE)
        for ch in range(tm // GMLP_CHUNK):
            rs = slice(ch * GMLP_CHUNK, (ch + 1) * GMLP_CHUNK)
            mixed = jnp.dot(w, v_ref[rs, cs], preferred_element_type=F32) + bs_ref[:, cs]
            gated_ref[rs, cs] = (u_ref[rs, cs].astype(F32) * mixed).astype(BF16)
    m = jnp.dot(gated_ref[...], wo_ref[...], preferred_element_type=F32)
    y = _res_ln(h_ref[...], m, g_ref[...], b_ref[...], alpha)
    oh_ref[...] = y
    ohb_ref[...] = y.astype(BF16)


INFO_IDX, INFO_RANK, INFO_GATE = 0, 2, 4
ROW_DMA_UNROLL = 8


def _router_kernel(h_ref, w_ref, info_ref, cnt_ref, run_ref):
    i = pl.program_id(0)
    tr = h_ref.shape[0]

    @pl.when(i == 0)
    def _():
        run_ref[...] = jnp.zeros_like(run_ref)

    x = h_ref[...]
    x1 = x.astype(BF16)
    r1 = x - x1.astype(F32)
    x2 = r1.astype(BF16)
    x3 = (r1 - x2.astype(F32)).astype(BF16)
    w = w_ref[...]
    lane = lax.broadcasted_iota(I32, (tr, LANE), 1)
    t = (jnp.dot(x1, w, preferred_element_type=F32)
         + jnp.where(lane < 2 * N_EXPERTS, jnp.dot(x2, w, preferred_element_type=F32), 0.0)
         + jnp.where(lane < N_EXPERTS, jnp.dot(x3, w, preferred_element_type=F32), 0.0))
    logits = t + pltpu.roll(t, LANE - N_EXPERTS, 1) + pltpu.roll(t, LANE - 2 * N_EXPERTS, 1)
    big = jnp.float32(3e38)
    logits = jnp.where(lane < N_EXPERTS, logits, -big)

    lane_f = lane.astype(F32)
    v1 = jnp.max(logits, axis=-1, keepdims=True)
    i1 = jnp.min(jnp.where(logits == v1, lane_f, float(LANE)), axis=-1, keepdims=True)
    rest = jnp.where(lane_f == i1, -big, logits)
    v2 = jnp.max(rest, axis=-1, keepdims=True)
    i2 = jnp.min(jnp.where(rest == v2, lane_f, float(LANE)), axis=-1, keepdims=True)
    e = jnp.exp(v2 - v1)
    g1 = 1.0 / (1.0 + e)
    g2 = e / (1.0 + e)

    oh1 = jnp.where(lane_f == i1, 1.0, 0.0)
    oh2 = jnp.where(lane_f == i2, 1.0, 0.0)
    oh = oh1 + oh2
    rr = lax.broadcasted_iota(I32, (tr, tr), 0)
    cc = lax.broadcasted_iota(I32, (tr, tr), 1)
    before = jnp.where(cc < rr, 1.0, 0.0).astype(BF16)
    prior = jnp.dot(before, oh.astype(BF16), preferred_element_type=F32) + run_ref[...]
    rank1 = jnp.sum(prior * oh1, axis=-1, keepdims=True)
    rank2 = jnp.sum(prior * oh2, axis=-1, keepdims=True)
    run_ref[...] += jnp.sum(oh, axis=0, keepdims=True)
    cnt_ref[...] = run_ref[...]

    info = jnp.where(lane == INFO_IDX, i1, 0.0)
    info = jnp.where(lane == INFO_IDX + 1, i2, info)
    info = jnp.where(lane == INFO_RANK, rank1, info)
    info = jnp.where(lane == INFO_RANK + 1, rank2, info)
    info = jnp.where(lane == INFO_GATE, g1, info)
    info = jnp.where(lane == INFO_GATE + 1, g2, info)
    info_ref[...] = info


def _dispatch_kernel(pos_ref, h_ref, init_ref, xs_ref, sem):
    del init_ref
    td = h_ref.shape[0]

    def start(r, carry):
        for k in range(2):
            p = pos_ref[0, 0, 2 * r + k]
            pltpu.make_async_copy(h_ref.at[pl.ds(r, 1), :], xs_ref.at[pl.ds(p, 1), :], sem).start(priority=k)
        return carry

    lax.fori_loop(0, td, start, 0, unroll=ROW_DMA_UNROLL)
    for _ in range(2):
        pltpu.make_async_copy(h_ref, xs_ref.at[pl.ds(0, td), :], sem).wait()


def _experts_kernel(te_ref, nu_ref, x_ref, w1_ref, w3_ref, w2_ref, o_ref, xb_ref, *, fc):
    i = pl.program_id(0)
    j = pl.program_id(1)
    del te_ref

    @pl.when(j == 0)
    def _():
        o_ref[...] = jnp.zeros_like(o_ref)
        xb_ref[...] = x_ref[...].astype(BF16)

    @pl.when(i < nu_ref[0])
    def _():
        x = xb_ref[...]
        tf = w1_ref.shape[1]
        gated = []
        for c in range(tf // fc):
            sl = slice(c * fc, (c + 1) * fc)
            a = jnp.dot(x, w1_ref[:, sl].astype(BF16), preferred_element_type=F32)
            b3 = jnp.dot(x, w3_ref[:, sl].astype(BF16), preferred_element_type=F32)
            gated.append((jax.nn.silu(a) * b3).astype(BF16))
        o_ref[...] += jnp.dot(jnp.concatenate(gated, axis=1), w2_ref[...].astype(BF16),
                              preferred_element_type=F32)


def _combine_kernel(pos_ref, info_ref, h_ref, g_ref, b_ref, ys_ref, oh_ref, ohb_ref, buf_ref, sem, *, alpha):
    tc = h_ref.shape[0]

    def start(r, carry):
        for k in range(2):
            p = pos_ref[0, 0, 2 * r + k]
            pltpu.make_async_copy(ys_ref.at[pl.ds(p, 1), :], buf_ref.at[k, pl.ds(r, 1), :], sem).start(priority=k)
        return carry

    lax.fori_loop(0, tc, start, 0, unroll=ROW_DMA_UNROLL)
    for k in range(2):
        pltpu.make_async_copy(ys_ref.at[pl.ds(0, tc), :], buf_ref.at[k], sem).wait()
    info = info_ref[...]
    g1 = info[:, INFO_GATE:INFO_GATE + 1]
    g2 = info[:, INFO_GATE + 1:INFO_GATE + 2]
    m = g1 * buf_ref[0] + g2 * buf_ref[1]
    y = _res_ln(h_ref[...], m, g_ref[...], b_ref[...], alpha)
    oh_ref[...] = y
    ohb_ref[...] = y.astype(BF16)


def _split3(w):
    hi = w.astype(BF16)
    r = w - hi.astype(F32)
    mid = r.astype(BF16)
    lo = (r - mid.astype(F32)).astype(BF16)
    return hi, mid, lo


def _moe(h, w_router, w1, w3, w2, g, b, *, layer, alpha, tm=1024, tf=512, fc=256, td=256):
    n, d = h.shape
    f = w1.shape[3]
    tm = min(tm, n)
    td = min(td, n)
    tf = min(tf, f)
    row = lambda i: (i, 0)

    hi, mid, lo = _split3(w_router.astype(F32))
    wr = jnp.concatenate([hi, mid, lo, jnp.zeros((d, LANE - 3 * N_EXPERTS), BF16)], axis=1)
    tr = min(512, n)
    info, cnt = pl.pallas_call(
        _router_kernel,
        grid=(n // tr,),
        in_specs=[pl.BlockSpec((tr, d), row), _resident((d, LANE))],
        out_specs=[pl.BlockSpec((tr, LANE), row), pl.BlockSpec((1, LANE), lambda i: (0, 0))],
        out_shape=[jax.ShapeDtypeStruct((n, LANE), F32), jax.ShapeDtypeStruct((1, LANE), F32)],
        scratch_shapes=[pltpu.VMEM((1, LANE), F32)],
        compiler_params=_cp(1),
        name="moe_router",
    )(h, wr)

    idx = info[:, INFO_IDX:INFO_IDX + 2].astype(I32)
    rank = info[:, INFO_RANK:INFO_RANK + 2].astype(I32)
    counts = cnt[0, :N_EXPERTS].astype(I32)
    padded = ((counts + tm - 1) // tm) * tm
    ends = jnp.cumsum(padded)
    starts = ends - padded
    pos = starts[idx] + rank
    n_tiles = (2 * n) // tm + N_EXPERTS
    n_used = (ends[-1] // tm).astype(I32)
    tile_start = jnp.arange(n_tiles, dtype=I32) * tm
    tile_e = jnp.sum(tile_start[:, None] >= ends[None, :], axis=1).astype(I32)
    tile_e = jnp.minimum(tile_e, tile_e[jnp.maximum(n_used - 1, 0)])
    rows = n_tiles * tm
    pos_blocks = pos.reshape(n // td, 1, 2 * td)

    xs = pl.pallas_call(
        _dispatch_kernel,
        grid=(n // td,),
        in_specs=[pl.BlockSpec((1, 1, 2 * td), lambda i: (i, 0, 0), memory_space=pltpu.SMEM),
                  pl.BlockSpec((td, d), row),
                  pl.BlockSpec(memory_space=pl.ANY)],
        out_specs=pl.BlockSpec(memory_space=pl.ANY),
        out_shape=jax.ShapeDtypeStruct((rows, d), F32),
        scratch_shapes=[pltpu.SemaphoreType.DMA(())],
        input_output_aliases={2: 0},
        compiler_params=_cp(1),
        name="moe_dispatch",
    )(pos_blocks, h, jnp.zeros((rows, d), F32))

    nj = f // tf

    def x_map(i, j, te, nu):
        return (jnp.minimum(i, nu[0] - 1), 0)

    def w13_map(i, j, te, nu):
        return (layer, te[i], 0, jnp.where(i < nu[0], j, nj - 1))

    def w2_map(i, j, te, nu):
        return (layer, te[i], jnp.where(i < nu[0], j, nj - 1), 0)

    ys = pl.pallas_call(
        functools.partial(_experts_kernel, fc=fc),
        grid_spec=pltpu.PrefetchScalarGridSpec(
            num_scalar_prefetch=2,
            grid=(n_tiles, nj),
            in_specs=[pl.BlockSpec((tm, d), x_map),
                      pl.BlockSpec((None, None, d, tf), w13_map),
                      pl.BlockSpec((None, None, d, tf), w13_map),
                      pl.BlockSpec((None, None, tf, d), w2_map)],
            out_specs=pl.BlockSpec((tm, d), lambda i, j, te, nu: (i, 0)),
            scratch_shapes=[pltpu.VMEM((tm, d), BF16)],
        ),
        out_shape=jax.ShapeDtypeStruct((rows, d), F32),
        compiler_params=_cp(2),
        name="moe_experts",
    )(tile_e, n_used.reshape(1), xs, w1, w3, w2)

    tc = td
    return pl.pallas_call(
        functools.partial(_combine_kernel, alpha=alpha),
        grid=(n // tc,),
        in_specs=[pl.BlockSpec((1, 1, 2 * tc), lambda i: (i, 0, 0), memory_space=pltpu.SMEM),
                  pl.BlockSpec((tc, LANE), row),
                  pl.BlockSpec((tc, d), row),
                  _resident((1, d)), _resident((1, d)),
                  pl.BlockSpec(memory_space=pl.ANY)],
        out_specs=[pl.BlockSpec((tc, d), row), pl.BlockSpec((tc, d), row)],
        out_shape=[jax.ShapeDtypeStruct((n, d), F32), jax.ShapeDtypeStruct((n, d), BF16)],
        scratch_shapes=[pltpu.VMEM((2, tc, d), F32), pltpu.SemaphoreType.DMA(())],
        compiler_params=_cp(1),
        name="moe_combine",
    )(pos_blocks, info, h, g, b, ys)


def _diff_mixer(hb, wq, wk, wv, lq1, lk1, lq2, lk2, sub_g, *, batch, seq, lambda_init):
    d = wq.shape[0]
    heads = d // LANE
    qkv = _linear(hb, (wq.astype(BF16), wk.astype(BF16), wv.astype(BF16)), name="diff_qkv")
    extra = tuple(a.reshape(1, -1).astype(F32) for a in (lq1, lk1, lq2, lk2, sub_g))
    return _flash("diff", qkv, qkv, qkv, batch=batch, seq=seq, groups=heads, q_blk0=0, k_blk0=heads,
                  v_blk0=2 * heads, extra=extra, lambda_init=lambda_init)


def _band_mixer(h, hb, w_qkv, rel_bias, wo, g, b, *, batch, seq, alpha):
    d = w_qkv.shape[0]
    qkv = _linear(hb, (w_qkv.astype(BF16),), name="band_qkv")
    o = _band_attention(qkv, _band_bias_table(rel_bias), batch=batch, seq=seq, pairs=d // LANE)
    return _proj_res_ln(o, wo.astype(BF16), h, g, b, alpha=alpha, name="band_out")


def _mla_mixer(hb, w_dq, q_g, w_uq, w_dkv, kv_g, w_ukv, *, batch, seq, tm=512):
    n, d = hb.shape
    q_rank = w_dq.shape[1]
    kv_rank = kv_g.shape[-1]
    heads = w_ukv.shape[1] // (2 * CHUNK)
    nope, rope = 64, 2 * ROPE_HALF
    tm = min(tm, seq)
    row = lambda i: (i, 0)
    tab = lambda i: (i % (seq // tm), 0)

    inv_freq = ROPE_THETA ** (-jnp.arange(ROPE_HALF, dtype=F32) / ROPE_HALF)
    ang = jnp.arange(seq).astype(F32)[:, None] * inv_freq[None, :]
    cos, sin = jnp.cos(ang), jnp.sin(ang)
    z = lambda w: jnp.zeros((seq, w), F32)
    ct_q = jnp.concatenate([jnp.ones((seq, nope), F32), cos, cos, z(LANE - nope - rope)], axis=1)
    st_q = jnp.concatenate([z(nope), sin, sin, z(LANE - nope - rope)], axis=1)
    ct_k = jnp.concatenate([cos, cos, z(LANE - rope)], axis=1)
    st_k = jnp.concatenate([sin, sin, z(LANE - rope)], axis=1)

    wq3 = w_uq.reshape(q_rank, heads, nope + rope)
    qn, q1, q2 = wq3[..., :nope], wq3[..., nope:nope + ROPE_HALF], wq3[..., nope + ROPE_HALF:]
    zq = lambda w: jnp.zeros((q_rank, heads, w), w_uq.dtype)
    wa = jnp.concatenate([qn, q1, q2, zq(LANE - nope - rope)], axis=2).reshape(q_rank, heads * LANE)
    wb = jnp.concatenate([zq(nope), -q2, q1, zq(LANE - nope - rope)], axis=2).reshape(q_rank, heads * LANE)

    k1, k2 = w_dkv[:, kv_rank:kv_rank + ROPE_HALF], w_dkv[:, kv_rank + ROPE_HALF:]
    zd = lambda w: jnp.zeros((d, w), w_dkv.dtype)
    wd = jnp.concatenate([w_dkv[:, :kv_rank], k1, k2, zd(LANE - rope), -k2, k1, zd(LANE - rope)], axis=1)

    wkv3 = w_ukv.reshape(kv_rank, heads, nope + CHUNK)
    zk = lambda r, w: jnp.zeros((r, heads, w), w_ukv.dtype)
    k_top = jnp.concatenate([wkv3[..., :nope], zk(kv_rank, LANE - nope)], axis=2).reshape(kv_rank, heads * LANE)
    eye = jnp.broadcast_to(jnp.eye(rope, dtype=w_ukv.dtype)[:, None, :], (rope, heads, rope))
    k_rope = jnp.concatenate([zk(rope, nope), eye, zk(rope, LANE - nope - rope)], axis=2).reshape(rope, heads * LANE)
    v_top = wkv3[..., nope:].reshape(kv_rank, heads * CHUNK)
    wkv = jnp.concatenate([
        jnp.concatenate([k_top, v_top], axis=1),
        jnp.concatenate([k_rope, jnp.zeros((rope, heads * CHUNK), w_ukv.dtype)], axis=1),
        jnp.zeros((LANE - rope, heads * (LANE + CHUNK)), w_ukv.dtype)], axis=0)

    cq = pl.pallas_call(
        _mla_cq_kernel,
        grid=(n // tm,),
        in_specs=[pl.BlockSpec((tm, d), row), _resident((d, q_rank)), _resident((1, q_rank))],
        out_specs=pl.BlockSpec((tm, q_rank), row),
        out_shape=jax.ShapeDtypeStruct((n, q_rank), BF16),
        compiler_params=_cp(1),
        name="mla_cq",
    )(hb, w_dq.astype(BF16), q_g.reshape(1, -1))

    q = pl.pallas_call(
        functools.partial(_mla_q_kernel, scale=float((nope + rope) ** -0.5), heads=heads),
        grid=(n // tm,),
        in_specs=[pl.BlockSpec((tm, q_rank), row), _resident((q_rank, heads * LANE)),
                  _resident((q_rank, heads * LANE)), pl.BlockSpec((tm, LANE), tab), pl.BlockSpec((tm, LANE), tab)],
        out_specs=pl.BlockSpec((tm, heads * LANE), row),
        out_shape=jax.ShapeDtypeStruct((n, heads * LANE), BF16),
        compiler_params=_cp(1),
        name="mla_q",
    )(cq, wa.astype(BF16), wb.astype(BF16), ct_q, st_q)

    ckr = pl.pallas_call(
        functools.partial(_mla_ckv_kernel, rank=kv_rank),
        grid=(n // tm,),
        in_specs=[pl.BlockSpec((tm, d), row), _resident((d, kv_rank + 2 * LANE)), _resident((1, kv_rank)),
                  pl.BlockSpec((tm, LANE), tab), pl.BlockSpec((tm, LANE), tab)],
        out_specs=pl.BlockSpec((tm, kv_rank + LANE), row),
        out_shape=jax.ShapeDtypeStruct((n, kv_rank + LANE), BF16),
        compiler_params=_cp(1),
        name="mla_ckv",
    )(hb, wd.astype(BF16), kv_g.reshape(1, -1), ct_k, st_k)

    kv = _linear(ckr, (wkv.astype(BF16),), name="mla_kv")
    return _flash("mla", q, kv, kv, batch=batch, seq=seq, groups=heads // 2, q_blk0=0, k_blk0=0, v_blk0=heads)


def _sg_mixer(h, hb, w_in, vg, vb, w_s, b_s, w_out, g, b, *, alpha, tm=512):
    n, d = hb.shape
    width = w_out.shape[0]
    groups = w_s.shape[0]
    tm = min(tm, n)
    row = lambda i: (i, 0)
    u, v = pl.pallas_call(
        _sg_in_kernel,
        grid=(n // tm,),
        in_specs=[pl.BlockSpec((tm, d), row), _resident((d, 2 * width)), _resident((1, width)),
                  _resident((1, width))],
        out_specs=[pl.BlockSpec((tm, width), row), pl.BlockSpec((tm, width), row)],
        out_shape=[jax.ShapeDtypeStruct((n, width), BF16), jax.ShapeDtypeStruct((n, width), BF16)],
        compiler_params=_cp(1),
        name="sg_in",
    )(hb, w_in.astype(BF16), vg.reshape(1, -1), vb.reshape(1, -1))
    bs_full = jnp.repeat(b_s.T.astype(F32), width // groups, axis=1)
    return pl.pallas_call(
        functools.partial(_sg_mix_kernel, alpha=alpha, groups=groups),
        grid=(n // tm,),
        in_specs=[pl.BlockSpec((tm, width), row), pl.BlockSpec((tm, width), row),
                  _resident((groups, GMLP_CHUNK, GMLP_CHUNK)), _resident((GMLP_CHUNK, width)),
                  _resident((width, d)), pl.BlockSpec((tm, d), row), _resident((1, d)), _resident((1, d))],
        out_specs=[pl.BlockSpec((tm, d), row), pl.BlockSpec((tm, d), row)],
        out_shape=[jax.ShapeDtypeStruct((n, d), F32), jax.ShapeDtypeStruct((n, d), BF16)],
        scratch_shapes=[pltpu.VMEM((tm, width), BF16)],
        compiler_params=_cp(1),
        name="sg_mix",
    )(u, v, w_s.astype(F32), bs_full, w_out.astype(BF16), h, g, b)


def kernel(x, ln_mix_g, ln_mix_b, ln_ffn_g, ln_ffn_b, diff_wq, diff_wk, diff_wv, diff_lq1, diff_lk1, diff_lq2, diff_lk2, diff_sub_g, diff_wo, ca_w_qkv, ca_rel_bias, ca_wo, mla_w_dq, mla_q_norm_g, mla_w_uq, mla_w_dkv, mla_kv_norm_g, mla_w_ukv, mla_wo, sg_w_in, sg_v_norm_g, sg_v_norm_b, sg_w_s, sg_b_s, sg_w_out, ffn_w1, ffn_w3, ffn_w2, moe_w_router, moe_w1, moe_w3, moe_w2):
    batch, seq, d = x.shape
    depth = ln_mix_g.shape[0]
    alpha = float((2 * depth) ** 0.25)
    h = x.reshape(batch * seq, d).astype(F32)
    hb = h
    vec = lambda a: a.reshape(1, -1).astype(F32)
    for i in range(depth):
        kind, j = i % 4, i // 4
        mg, mb = vec(ln_mix_g[i]), vec(ln_mix_b[i])
        fg, fb = vec(ln_ffn_g[i]), vec(ln_ffn_b[i])
        if kind in (0, 2):
            if kind == 0:
                a = _diff_mixer(hb, diff_wq[j], diff_wk[j], diff_wv[j], diff_lq1[j], diff_lk1[j], diff_lq2[j],
                                diff_lk2[j], diff_sub_g[j], batch=batch, seq=seq,
                                lambda_init=0.8 - 0.6 * math.exp(-0.3 * i))
                wo = diff_wo[j]
            else:
                a = _mla_mixer(hb, mla_w_dq[j], mla_q_norm_g[j], mla_w_uq[j], mla_w_dkv[j], mla_kv_norm_g[j],
                               mla_w_ukv[j], batch=batch, seq=seq)
                wo = mla_wo[j]
            h, hb = _mix_ffn(a, wo.astype(BF16), mg, mb, ffn_w1[i // 2].astype(BF16), ffn_w3[i // 2].astype(BF16),
                             ffn_w2[i // 2].astype(BF16), h, fg, fb, alpha=alpha)
        else:
            if kind == 1:
                h, hb = _band_mixer(h, hb, ca_w_qkv[j], ca_rel_bias[j], ca_wo[j], mg, mb, batch=batch, seq=seq,
                                    alpha=alpha)
            else:
                h, hb = _sg_mixer(h, hb, sg_w_in[j], sg_v_norm_g[j], sg_v_norm_b[j], sg_w_s[j], sg_b_s[j],
                                  sg_w_out[j], mg, mb, alpha=alpha)
            h, hb = _moe(h, moe_w_router[i // 2], moe_w1, moe_w3, moe_w2, fg, fb, layer=i // 2, alpha=alpha)
    return h.reshape(batch, seq, d).astype(x.dtype)
```

```python
import functools
import math

import jax
import jax.numpy as jnp
import numpy as np
from jax import lax
from jax.experimental import pallas as pl
from jax.experimental.pallas import tpu as pltpu

F32 = jnp.float32
BF16 = jnp.bfloat16
I32 = jnp.int32

LANE = 128
VMEM_LIMIT = 56 * 1024 * 1024

CHUNK = 64
N_EXPERTS = 8
LN_EPS = 1e-5
RMS_EPS = 1e-6
NEG_INF = -1e30
ROPE_THETA = 10000.0
CA_LEFT = 8
CA_REL_CLIP = 128
ROPE_HALF = 16


def _cp(n_axes, vmem=VMEM_LIMIT):
    return pltpu.CompilerParams(dimension_semantics=("arbitrary",) * n_axes, vmem_limit_bytes=vmem)


def _resident(shape):
    nd = len(shape)
    return pl.BlockSpec(shape, lambda *_: (0,) * nd, pipeline_mode=pl.Buffered(1))


def _res_ln(h, m, g, b, alpha):
    z = alpha * h + m
    mu = jnp.mean(z, axis=-1, keepdims=True)
    zc = z - mu
    var = jnp.mean(zc * zc, axis=-1, keepdims=True)
    return zc * lax.rsqrt(var + LN_EPS) * g + b


def _rms(x, g, eps):
    ms = jnp.mean(x * x, axis=-1, keepdims=True)
    return x * lax.rsqrt(ms + eps) * g


def _linear_kernel(x_ref, *refs):
    *w_refs, o_ref = refs
    x = x_ref[...].astype(BF16)
    col = 0
    for w_ref in w_refs:
        n = w_ref.shape[1]
        o_ref[:, col:col + n] = jnp.dot(x, w_ref[...], preferred_element_type=F32).astype(o_ref.dtype)
        col += n


def _linear(x, ws, *, tm=512, out_dtype=BF16, name="linear"):
    m, k = x.shape
    n = sum(w.shape[1] for w in ws)
    tm = min(tm, m)
    return pl.pallas_call(
        _linear_kernel,
        grid=(m // tm,),
        in_specs=[pl.BlockSpec((tm, k), lambda i: (i, 0))] + [_resident(w.shape) for w in ws],
        out_specs=pl.BlockSpec((tm, n), lambda i: (i, 0)),
        out_shape=jax.ShapeDtypeStruct((m, n), out_dtype),
        compiler_params=_cp(1),
        name=name,
    )(x, *ws)


def _proj_res_ln_kernel(a_ref, w_ref, h_ref, g_ref, b_ref, oh_ref, ohb_ref, *, alpha):
    m = jnp.dot(a_ref[...], w_ref[...], preferred_element_type=F32)
    y = _res_ln(h_ref[...], m, g_ref[...], b_ref[...], alpha)
    oh_ref[...] = y
    ohb_ref[...] = y.astype(BF16)


def _proj_res_ln(a, w, h, g, b, *, alpha, tm=512, name="proj_res_ln"):
    m, k = a.shape
    d = w.shape[1]
    tm = min(tm, m)
    row = lambda i: (i, 0)
    return pl.pallas_call(
        functools.partial(_proj_res_ln_kernel, alpha=alpha),
        grid=(m // tm,),
        in_specs=[pl.BlockSpec((tm, k), row), _resident((k, d)), pl.BlockSpec((tm, d), row),
                  _resident((1, d)), _resident((1, d))],
        out_specs=[pl.BlockSpec((tm, d), row), pl.BlockSpec((tm, d), row)],
        out_shape=[jax.ShapeDtypeStruct((m, d), F32), jax.ShapeDtypeStruct((m, d), BF16)],
        compiler_params=_cp(1),
        name=name,
    )(a, w, h, g, b)


def _mix_ffn_kernel(a_ref, wo_ref, mg_ref, mb_ref, w1_ref, w3_ref, w2_ref, h_ref, g_ref, b_ref, oh_ref, ohb_ref,
                    hm_ref, xb_ref, acc_ref, *, alpha, fc):
    mix = jnp.dot(a_ref[...], wo_ref[...], preferred_element_type=F32)
    hm = _res_ln(h_ref[...], mix, mg_ref[...], mb_ref[...], alpha)
    hm_ref[...] = hm
    xb_ref[...] = hm.astype(BF16)
    x = xb_ref[...]
    f = w1_ref.shape[1]
    for c in range(f // fc):
        sl = slice(c * fc, (c + 1) * fc)
        a = jnp.dot(x, w1_ref[:, sl], preferred_element_type=F32)
        b3 = jnp.dot(x, w3_ref[:, sl], preferred_element_type=F32)
        gated = (jax.nn.silu(a) * b3).astype(BF16)
        part = jnp.dot(gated, w2_ref[sl, :], preferred_element_type=F32)
        if c == 0:
            acc_ref[...] = part
        else:
            acc_ref[...] += part
    y = _res_ln(hm_ref[...], acc_ref[...], g_ref[...], b_ref[...], alpha)
    oh_ref[...] = y
    ohb_ref[...] = y.astype(BF16)


def _mix_ffn(a, wo, mg, mb, w1, w3, w2, h, g, b, *, alpha, tm=512, fc=256):
    m, d = h.shape
    k = a.shape[1]
    f = w1.shape[1]
    tm = min(tm, m)
    row = lambda i: (i, 0)
    vecs = [_resident((1, d)), _resident((1, d))]
    return pl.pallas_call(
        functools.partial(_mix_ffn_kernel, alpha=alpha, fc=fc),
        grid=(m // tm,),
        in_specs=[pl.BlockSpec((tm, k), row), _resident((k, d))] + vecs
                 + [_resident((d, f)), _resident((d, f)), _resident((f, d)), pl.BlockSpec((tm, d), row)] + vecs,
        out_specs=[pl.BlockSpec((tm, d), row), pl.BlockSpec((tm, d), row)],
        out_shape=[jax.ShapeDtypeStruct((m, d), F32), jax.ShapeDtypeStruct((m, d), BF16)],
        scratch_shapes=[pltpu.VMEM((tm, d), F32), pltpu.VMEM((tm, d), BF16), pltpu.VMEM((tm, d), F32)],
        compiler_params=_cp(1),
        name="mix_ffn",
    )(a, wo, mg, mb, w1, w3, w2, h, g, b)


ONES_ROWS = 16
ACC_ROWS = LANE + ONES_ROWS


def _vt_ext(v_blk):
    vt = v_blk.astype(F32).T.astype(BF16)
    return jnp.concatenate([vt, jnp.ones((ONES_ROWS, v_blk.shape[0]), BF16)], axis=0)


def _split_heads(x):
    low = lax.broadcasted_iota(I32, x.shape, 1) < CHUNK
    zero = jnp.zeros_like(x)
    return jnp.where(low, x, zero), jnp.where(low, zero, x)


_NT = (((1,), (1,)), ((), ()))


def _flash_kernel(*refs, mode, tq, gs, lambda_init):
    if mode == "diff":
        lq1_ref, lk1_ref, lq2_ref, lk2_ref, subg_ref, q_ref, k_ref, v_ref, o_ref, *scratch = refs
    else:
        q_ref, k_ref, v_ref, o_ref, *scratch = refs
    vt_ref, m_ref, a_ref, bm_ref, acc_ref, s_ref, p_ref = scratch
    qi = pl.program_id(2)
    tk = tq // 2
    nmap = 2 * gs
    lanes = lambda i: slice(i * LANE, (i + 1) * LANE)

    @pl.when(qi == 0)
    def _():
        for j in range(vt_ref.shape[0]):
            for g in range(gs):
                vt_ref[j, g] = _vt_ext(v_ref[j * tk:(j + 1) * tk, lanes(g)])

    qs = []
    for g in range(gs):
        if mode == "diff":
            qs += list(_split_heads(q_ref[:, lanes(g)] * jnp.asarray(CHUNK ** -0.5, BF16)))
        else:
            qs += [q_ref[:, lanes(2 * g)], q_ref[:, lanes(2 * g + 1)]]
    m_ref[...] = jnp.full(m_ref.shape, NEG_INF, F32)
    acc_ref[...] = jnp.zeros(acc_ref.shape, F32)

    def scores(blk):
        start = pl.multiple_of(blk * tk, tk)
        kb = k_ref[pl.ds(start, tk), :]
        kks = [kb[:, lanes(mp // 2 if mode == "diff" else mp)] for mp in range(nmap)]
        return [lax.dot_general(kks[mp], qs[mp], _NT, preferred_element_type=F32) for mp in range(nmap)]

    def values(blk, slot):
        return [jnp.dot(vt_ref[blk, mp // 2], p_ref[slot, mp], preferred_element_type=F32) for mp in range(nmap)]

    def stash(slot, ss):
        for mp in range(nmap):
            s_ref[slot, mp] = ss[mp]
            bm_ref[slot, mp] = jnp.max(ss[mp], axis=0, keepdims=True)

    def softmax(slot, masked):
        for mp in range(nmap):
            s = s_ref[slot, mp]
            if masked:
                kc = (lax.broadcasted_iota(I32, s.shape, 0) + slot * tk) >> 6
                qc = lax.broadcasted_iota(I32, s.shape, 1) >> 6
                s = jnp.where(kc <= qc, s, NEG_INF)
                blk_max = jnp.max(s, axis=0, keepdims=True)
            else:
                blk_max = bm_ref[slot, mp]
            m_old = m_ref[mp]
            m_new = jnp.maximum(m_old, blk_max)
            p_ref[slot, mp] = jnp.exp(s - m_new).astype(BF16)
            a_ref[slot, mp] = jnp.exp(m_old - m_new)
            m_ref[mp] = m_new

    def fold(pv, scale):
        for mp in range(nmap):
            acc_ref[mp] = scale[mp] * acc_ref[mp] + pv[mp]

    def trip(blk, masked, produce):
        scale = [[a_ref[u, mp] for mp in range(nmap)] for u in range(2)]
        pvs = [values(jnp.maximum(blk - 2 + u, 0), u) for u in range(2)]
        nxt = [scores(blk + 2 + u) for u in range(2)] if produce else []
        for u in range(2):
            softmax(u, masked)
        for u in range(2):
            fold(pvs[u], scale[u])
        for u, s in enumerate(nxt):
            stash(u, s)

    for u in range(2):
        stash(u, scores(u))
    p_ref[...] = jnp.zeros(p_ref.shape, BF16)
    a_ref[...] = jnp.ones(a_ref.shape, F32)

    def step(j, carry):
        trip(2 * j, False, True)
        return carry

    lax.fori_loop(0, qi, step, 0)
    trip(2 * qi, True, False)
    for u in range(2):
        fold(values(2 * qi + u, u), [a_ref[u, mp] for mp in range(nmap)])

    for g in range(gs):
        acc_a, acc_b = acc_ref[2 * g], acc_ref[2 * g + 1]
        oa = acc_a[:LANE] / acc_a[LANE:LANE + 1]
        ob = acc_b[:LANE] / acc_b[LANE:LANE + 1]
        if mode == "diff":
            lam = (jnp.exp(jnp.sum(lq1_ref[...] * lk1_ref[...], axis=-1, keepdims=True))
                   - jnp.exp(jnp.sum(lq2_ref[...] * lk2_ref[...], axis=-1, keepdims=True)) + lambda_init)
            o = (oa - lam * ob).T
            o = _rms(o, subg_ref[...], 1e-5) * (1.0 - lambda_init)
        else:
            row = lax.broadcasted_iota(I32, oa.shape, 0)
            o = jnp.where(row < CHUNK, oa, ob).T
        o_ref[:, lanes(g)] = o.astype(o_ref.dtype)


def _flash(mode, q_arr, k_arr, v_arr, *, batch, seq, groups, q_blk0, k_blk0, v_blk0, extra=(), lambda_init=0.0,
           tq=512, gs=2):
    qw = (LANE if mode == "diff" else 2 * LANE) * gs
    tq = min(tq, seq)
    tk = tq // 2
    nq = seq // tq
    n = batch * seq
    nmap = 2 * gs
    assert groups % gs == 0 and q_blk0 % gs == 0 and k_blk0 % gs == 0 and v_blk0 % gs == 0
    qb, kb, vb = q_blk0 // gs, k_blk0 // gs, v_blk0 // gs
    in_specs = [pl.BlockSpec(e.shape, lambda b, g, i: (0, 0)) for e in extra]
    in_specs += [
        pl.BlockSpec((tq, qw), lambda b, g, i: (b * nq + i, qb + g)),
        pl.BlockSpec((seq, qw), lambda b, g, i: (b, kb + g)),
        pl.BlockSpec((seq, gs * LANE), lambda b, g, i: (b, vb + g)),
    ]
    return pl.pallas_call(
        functools.partial(_flash_kernel, mode=mode, tq=tq, gs=gs, lambda_init=lambda_init),
        grid=(batch, groups // gs, nq),
        in_specs=in_specs,
        out_specs=pl.BlockSpec((tq, gs * LANE), lambda b, g, i: (b * nq + i, g)),
        out_shape=jax.ShapeDtypeStruct((n, groups * LANE), BF16),
        scratch_shapes=[pltpu.VMEM((seq // tk, gs, ACC_ROWS, tk), BF16), pltpu.VMEM((nmap, 1, tq), F32),
                        pltpu.VMEM((2, nmap, 1, tq), F32), pltpu.VMEM((2, nmap, 1, tq), F32),
                        pltpu.VMEM((nmap, ACC_ROWS, tq), F32),
                        pltpu.VMEM((2, nmap, tk, tq), F32), pltpu.VMEM((2, nmap, tk, tq), BF16)],
        compiler_params=_cp(3),
        name="flash_" + mode,
    )(*extra, q_arr, k_arr, v_arr)


BAND_TQ = 4 * CHUNK
BAND_W = (CA_LEFT + 4) * CHUNK
BAND_PAD = CA_LEFT * CHUNK
BAND_NW = BAND_W // BAND_TQ
BAND_EXT = 1024


def _band_kernel(q_ref, qn_ref, k_ref, v_ref, ext_ref, o_ref, kpad_ref, vt_ref, s_ref, bm_ref, bias_ref, *, seq,
                 gp):
    t = pl.program_id(2)
    nt = pl.num_programs(2)
    npad = BAND_PAD // BAND_TQ

    nh = 2 * gp
    lanes = lambda i: slice(i * LANE, (i + 1) * LANE)

    def produce(qr, tile, slot, front):
        qhs = []
        for g in range(gp):
            qhs += list(_split_heads(qr[:, lanes(g)] * jnp.asarray(CHUNK ** -0.5, BF16)))
        tops = [None] * nh
        for u in range(BAND_NW):
            start = pl.multiple_of((tile + u) * BAND_TQ, BAND_TQ)
            kb = kpad_ref[pl.ds(start, BAND_TQ), :]
            for hh in range(nh):
                s = lax.dot_general(kb[:, lanes(hh // 2)], qhs[hh], _NT, preferred_element_type=F32)
                s = s + bias_ref[hh, u * BAND_TQ:(u + 1) * BAND_TQ, :]
                if front:
                    invalid = lax.broadcasted_iota(I32, s.shape, 0) + (tile + u) * BAND_TQ < BAND_PAD
                    s = jnp.where(invalid, NEG_INF, s)
                s_ref[slot, hh, u] = s
                top = jnp.max(s, axis=0, keepdims=True)
                tops[hh] = top if tops[hh] is None else jnp.maximum(tops[hh], top)
        for hh in range(nh):
            bm_ref[slot, hh] = tops[hh]

    @pl.when((t == 0) & (pl.program_id(1) == 0))
    def _():
        ci = lax.broadcasted_iota(I32, (BAND_W, BAND_TQ), 1) >> 6
        cj = lax.broadcasted_iota(I32, (BAND_W, BAND_TQ), 0) >> 6
        vis = (cj >= ci) & (cj <= ci + CA_LEFT)
        for hh in range(nh):
            rows = jnp.broadcast_to(ext_ref[hh // 2, hh % 2], (BAND_W, BAND_EXT))
            skew = pltpu.roll(rows, BAND_EXT - (BAND_W - 1), 1, stride=1, stride_axis=0)
            bias_ref[hh] = jnp.where(vis, skew[:, :BAND_TQ], NEG_INF)

    @pl.when(t == 0)
    def _():
        kpad_ref[0:BAND_PAD, :] = jnp.zeros((BAND_PAD, gp * LANE), BF16)
        kpad_ref[BAND_PAD:BAND_PAD + seq, :] = k_ref[...]
        for j in range(npad):
            vt_ref[j] = jnp.zeros((gp, ACC_ROWS, BAND_TQ), BF16)
        for j in range(seq // BAND_TQ):
            for g in range(gp):
                vt_ref[npad + j, g] = _vt_ext(v_ref[j * BAND_TQ:(j + 1) * BAND_TQ, lanes(g)])
        produce(q_ref, 0, 0, True)

    cur = t & 1
    nxt_tile = jnp.minimum(t + 1, nt - 1)

    def consume():
        outs = []
        for hh in range(nh):
            m = bm_ref[cur, hh]
            acc = None
            for u in range(BAND_NW):
                p = jnp.exp(s_ref[cur, hh, u] - m).astype(BF16)
                pv = jnp.dot(vt_ref[t + u, hh // 2], p, preferred_element_type=F32)
                acc = pv if acc is None else acc + pv
            outs.append(acc[:LANE] / acc[LANE:LANE + 1])
        row = lax.broadcasted_iota(I32, outs[0].shape, 0)
        for g in range(gp):
            o_ref[:, lanes(g)] = jnp.where(row < CHUNK, outs[2 * g], outs[2 * g + 1]).T.astype(o_ref.dtype)

    def run(front):
        consume()
        produce(qn_ref, nxt_tile, 1 - cur, front)

    pl.when(t + 1 < npad)(lambda: run(True))
    pl.when(t + 1 >= npad)(lambda: run(False))


def _band_attention(qkv, ext, *, batch, seq, pairs, gp=2):
    n = batch * seq
    nt = seq // BAND_TQ
    assert pairs % gp == 0
    steps, w, nh = pairs // gp, gp * LANE, 2 * gp
    return pl.pallas_call(
        functools.partial(_band_kernel, seq=seq, gp=gp),
        grid=(steps, batch, nt),
        in_specs=[
            pl.BlockSpec((BAND_TQ, w), lambda g, b, t: (b * nt + t, g)),
            pl.BlockSpec((BAND_TQ, w), lambda g, b, t: (b * nt + jnp.minimum(t + 1, nt - 1), g)),
            pl.BlockSpec((seq, w), lambda g, b, t: (b, steps + g)),
            pl.BlockSpec((seq, w), lambda g, b, t: (b, 2 * steps + g)),
            pl.BlockSpec((gp, 2, 1, BAND_EXT), lambda g, b, t: (g, 0, 0, 0)),
        ],
        out_specs=pl.BlockSpec((BAND_TQ, w), lambda g, b, t: (b * nt + t, g)),
        out_shape=jax.ShapeDtypeStruct((n, pairs * LANE), BF16),
        scratch_shapes=[pltpu.VMEM((seq + BAND_PAD, w), BF16),
                        pltpu.VMEM(((seq + BAND_PAD) // BAND_TQ, gp, ACC_ROWS, BAND_TQ), BF16),
                        pltpu.VMEM((2, nh, BAND_NW, BAND_TQ, BAND_TQ), F32),
                        pltpu.VMEM((2, nh, 1, BAND_TQ), F32),
                        pltpu.VMEM((nh, BAND_W, BAND_TQ), F32)],
        compiler_params=_cp(3),
        name="band_attention",
    )(qkv, qkv, qkv, qkv, ext)


def _band_bias_table(rel_bias):
    heads = rel_bias.shape[0]
    rel = np.minimum(np.arange(BAND_EXT), BAND_TQ + BAND_W - 2) - (BAND_W - 1) + BAND_PAD
    ext = rel_bias.astype(F32)[:, np.clip(rel, -CA_REL_CLIP, CA_REL_CLIP) + CA_REL_CLIP]
    return ext.reshape(heads // 2, 2, 1, BAND_EXT)


def _mla_cq_kernel(x_ref, w_ref, g_ref, o_ref):
    c = jnp.dot(x_ref[...], w_ref[...], preferred_element_type=F32)
    o_ref[...] = _rms(c, g_ref[...], RMS_EPS).astype(o_ref.dtype)


def _mla_q_kernel(c_ref, wa_ref, wb_ref, ct_ref, st_ref, o_ref, *, scale, heads):
    c = c_ref[...]
    a = jnp.dot(c, wa_ref[...], preferred_element_type=F32)
    b = jnp.dot(c, wb_ref[...], preferred_element_type=F32)
    ct = ct_ref[...] * scale
    st = st_ref[...] * scale
    for h in range(heads):
        sl = slice(h * LANE, (h + 1) * LANE)
        o_ref[:, sl] = (a[:, sl] * ct + b[:, sl] * st).astype(o_ref.dtype)


def _mla_ckv_kernel(x_ref, w_ref, g_ref, ct_ref, st_ref, o_ref, *, rank):
    y = jnp.dot(x_ref[...], w_ref[...], preferred_element_type=F32)
    o_ref[:, :rank] = _rms(y[:, :rank], g_ref[...], RMS_EPS).astype(o_ref.dtype)
    kr = y[:, rank:rank + LANE] * ct_ref[...] + y[:, rank + LANE:rank + 2 * LANE] * st_ref[...]
    o_ref[:, rank:] = kr.astype(o_ref.dtype)


GMLP_CHUNK = 128


def _sg_in_kernel(x_ref, w_ref, g_ref, b_ref, u_ref, v_ref):
    width = u_ref.shape[1]
    hh = jax.nn.gelu(jnp.dot(x_ref[...], w_ref[...], preferred_element_type=F32))
    u_ref[...] = hh[:, :width].astype(u_ref.dtype)
    v = hh[:, width:]
    mu = jnp.mean(v, axis=-1, keepdims=True)
    vc = v - mu
    var = jnp.mean(vc * vc, axis=-1, keepdims=True)
    v_ref[...] = (vc * lax.rsqrt(var + LN_EPS) * g_ref[...] + b_ref[...]).astype(v_ref.dtype)


def _sg_mix_kernel(u_ref, v_ref, ws_ref, bs_ref, wo_ref, h_ref, g_ref, b_ref, oh_ref, ohb_ref, gated_ref, *,
                   alpha, groups):
    tm = u_ref.shape[0]
    r = lax.broadcasted_iota(I32, (GMLP_CHUNK, GMLP_CHUNK), 0) >> 6
    c = lax.broadcasted_iota(I32, (GMLP_CHUNK, GMLP_CHUNK), 1) >> 6
    vis = c <= r
    for gi in range(groups):
        w = jnp.where(vis, ws_ref[gi], 0.0).astype(BF16)
        cs = slice(gi * LANE, (gi + 1) * LANE)
        for ch in range(tm // GMLP_CHUNK):
            rs = slice(ch * GMLP_CHUNK, (ch + 1) * GMLP_CHUNK)
            mixed = jnp.dot(w, v_ref[rs, cs], preferred_element_type=F32) + bs_ref[:, cs]
            gated_ref[rs, cs] = (u_ref[rs, cs].astype(F32) * mixed).astype(BF16)
    m = jnp.dot(gated_ref[...], wo_ref[...], preferred_element_type=F32)
    y = _res_ln(h_ref[...], m, g_ref[...], b_ref[...], alpha)
    oh_ref[...] = y
    ohb_ref[...] = y.astype(BF16)


INFO_IDX, INFO_RANK, INFO_GATE = 0, 2, 4
ROW_DMA_UNROLL = 8


def _router_kernel(h_ref, w_ref, info_ref, cnt_ref, run_ref):
    i = pl.program_id(0)
    tr = h_ref.shape[0]

    @pl.when(i == 0)
    def _():
        run_ref[...] = jnp.zeros_like(run_ref)

    x = h_ref[...]
    x1 = x.astype(BF16)
    r1 = x - x1.astype(F32)
    x2 = r1.astype(BF16)
    x3 = (r1 - x2.astype(F32)).astype(BF16)
    w = w_ref[...]
    lane = lax.broadcasted_iota(I32, (tr, LANE), 1)
    t = (jnp.dot(x1, w, preferred_element_type=F32)
         + jnp.where(lane < 2 * N_EXPERTS, jnp.dot(x2, w, preferred_element_type=F32), 0.0)
         + jnp.where(lane < N_EXPERTS, jnp.dot(x3, w, preferred_element_type=F32), 0.0))
    logits = t + pltpu.roll(t, LANE - N_EXPERTS, 1) + pltpu.roll(t, LANE - 2 * N_EXPERTS, 1)
    big = jnp.float32(3e38)
    logits = jnp.where(lane < N_EXPERTS, logits, -big)

    lane_f = lane.astype(F32)
    v1 = jnp.max(logits, axis=-1, keepdims=True)
    i1 = jnp.min(jnp.where(logits == v1, lane_f, float(LANE)), axis=-1, keepdims=True)
    rest = jnp.where(lane_f == i1, -big, logits)
    v2 = jnp.max(rest, axis=-1, keepdims=True)
    i2 = jnp.min(jnp.where(rest == v2, lane_f, float(LANE)), axis=-1, keepdims=True)
    e = jnp.exp(v2 - v1)
    g1 = 1.0 / (1.0 + e)
    g2 = e / (1.0 + e)

    oh1 = jnp.where(lane_f == i1, 1.0, 0.0)
    oh2 = jnp.where(lane_f == i2, 1.0, 0.0)
    oh = oh1 + oh2
    rr = lax.broadcasted_iota(I32, (tr, tr), 0)
    cc = lax.broadcasted_iota(I32, (tr, tr), 1)
    before = jnp.where(cc < rr, 1.0, 0.0).astype(BF16)
    prior = jnp.dot(before, oh.astype(BF16), preferred_element_type=F32) + run_ref[...]
    rank1 = jnp.sum(prior * oh1, axis=-1, keepdims=True)
    rank2 = jnp.sum(prior * oh2, axis=-1, keepdims=True)
    run_ref[...] += jnp.sum(oh, axis=0, keepdims=True)
    cnt_ref[...] = run_ref[...]

    info = jnp.where(lane == INFO_IDX, i1, 0.0)
    info = jnp.where(lane == INFO_IDX + 1, i2, info)
    info = jnp.where(lane == INFO_RANK, rank1, info)
    info = jnp.where(lane == INFO_RANK + 1, rank2, info)
    info = jnp.where(lane == INFO_GATE, g1, info)
    info = jnp.where(lane == INFO_GATE + 1, g2, info)
    info_ref[...] = info


def _dispatch_kernel(pos_ref, h_ref, init_ref, xs_ref, sem):
    del init_ref
    td = h_ref.shape[0]

    def start(r, carry):
        for k in range(2):
            p = pos_ref[0, 0, 2 * r + k]
            pltpu.make_async_copy(h_ref.at[pl.ds(r, 1), :], xs_ref.at[pl.ds(p, 1), :], sem).start(priority=k)
        return carry

    lax.fori_loop(0, td, start, 0, unroll=ROW_DMA_UNROLL)
    for _ in range(2):
        pltpu.make_async_copy(h_ref, xs_ref.at[pl.ds(0, td), :], sem).wait()


def _experts_kernel(te_ref, nv_ref, nu_ref, x_ref, w1_ref, w3_ref, w2_ref, o_ref, xb_ref, wb1_ref, wb3_ref,
                    wb2_ref, *, fc):
    i = pl.program_id(0)
    j = pl.program_id(1)
    del te_ref, nu_ref
    half = x_ref.shape[0] // 2
    tf = w1_ref.shape[1]

    @pl.when(j == 0)
    def _():
        o_ref[...] = jnp.zeros_like(o_ref)
        xb_ref[...] = x_ref[...].astype(BF16)

    def swiglu(rows, w1, w3, w2):
        x = xb_ref[rows, :]
        gated = []
        for c in range(tf // fc):
            sl = slice(c * fc, (c + 1) * fc)
            a = jnp.dot(x, w1[:, sl], preferred_element_type=F32)
            b3 = jnp.dot(x, w3[:, sl], preferred_element_type=F32)
            gated.append((jax.nn.silu(a) * b3).astype(BF16))
        o_ref[rows, :] += jnp.dot(jnp.concatenate(gated, axis=1), w2, preferred_element_type=F32)

    @pl.when(nv_ref[i] > 0)
    def _():
        w1 = w1_ref[...].astype(BF16)
        w3 = w3_ref[...].astype(BF16)
        w2 = w2_ref[...].astype(BF16)
        wb1_ref[...] = w1
        wb3_ref[...] = w3
        wb2_ref[...] = w2
        swiglu(slice(0, half), w1, w3, w2)

    @pl.when(nv_ref[i] > half)
    def _():
        swiglu(slice(half, 2 * half), wb1_ref[...], wb3_ref[...], wb2_ref[...])


def _combine_kernel(pos_ref, posn_ref, info_ref, h_ref, g_ref, b_ref, ys_ref, oh_ref, ohb_ref, buf_ref, sem, *,
                    alpha):
    i = pl.program_id(0)
    tc = h_ref.shape[0]

    def gather(p_ref, slot):
        def start(r, carry):
            for k in range(2):
                p = p_ref[0, 0, 2 * r + k]
                pltpu.make_async_copy(ys_ref.at[pl.ds(p, 1), :], buf_ref.at[slot, k, pl.ds(r, 1), :],
                                      sem.at[slot]).start(priority=k)
            return carry

        lax.fori_loop(0, tc, start, 0, unroll=ROW_DMA_UNROLL)

    @pl.when(i == 0)
    def _():
        gather(pos_ref, 0)

    @pl.when(i + 1 < pl.num_programs(0))
    def _():
        gather(posn_ref, (i + 1) & 1)

    cur = i & 1
    for k in range(2):
        pltpu.make_async_copy(ys_ref.at[pl.ds(0, tc), :], buf_ref.at[cur, k], sem.at[cur]).wait()
    info = info_ref[...]
    g1 = info[:, INFO_GATE:INFO_GATE + 1]
    g2 = info[:, INFO_GATE + 1:INFO_GATE + 2]
    m = g1 * buf_ref[cur, 0] + g2 * buf_ref[cur, 1]
    y = _res_ln(h_ref[...], m, g_ref[...], b_ref[...], alpha)
    oh_ref[...] = y
    ohb_ref[...] = y.astype(BF16)


def _split3(w):
    hi = w.astype(BF16)
    r = w - hi.astype(F32)
    mid = r.astype(BF16)
    lo = (r - mid.astype(F32)).astype(BF16)
    return hi, mid, lo


def _moe(h, w_router, w1, w3, w2, g, b, *, layer, alpha, tm=1024, tf=512, fc=256, td=256):
    n, d = h.shape
    f = w1.shape[3]
    tm = min(tm, n)
    td = min(td, n)
    tf = min(tf, f)
    row = lambda i: (i, 0)

    hi, mid, lo = _split3(w_router.astype(F32))
    wr = jnp.concatenate([hi, mid, lo, jnp.zeros((d, LANE - 3 * N_EXPERTS), BF16)], axis=1)
    tr = min(512, n)
    info, cnt = pl.pallas_call(
        _router_kernel,
        grid=(n // tr,),
        in_specs=[pl.BlockSpec((tr, d), row), _resident((d, LANE))],
        out_specs=[pl.BlockSpec((tr, LANE), row), pl.BlockSpec((1, LANE), lambda i: (0, 0))],
        out_shape=[jax.ShapeDtypeStruct((n, LANE), F32), jax.ShapeDtypeStruct((1, LANE), F32)],
        scratch_shapes=[pltpu.VMEM((1, LANE), F32)],
        compiler_params=_cp(1),
        name="moe_router",
    )(h, wr)

    idx = info[:, INFO_IDX:INFO_IDX + 2].astype(I32)
    rank = info[:, INFO_RANK:INFO_RANK + 2].astype(I32)
    counts = cnt[0, :N_EXPERTS].astype(I32)
    padded = ((counts + tm - 1) // tm) * tm
    ends = jnp.cumsum(padded)
    starts = ends - padded
    pos = starts[idx] + rank
    n_tiles = (2 * n) // tm + N_EXPERTS
    n_used = (ends[-1] // tm).astype(I32)
    tile_start = jnp.arange(n_tiles, dtype=I32) * tm
    tile_e = jnp.sum(tile_start[:, None] >= ends[None, :], axis=1).astype(I32)
    tile_e = jnp.minimum(tile_e, tile_e[jnp.maximum(n_used - 1, 0)])
    tile_rows = jnp.clip((starts + counts)[tile_e] - tile_start, 0, tm)
    tile_rows = jnp.where(tile_start < ends[-1], tile_rows, 0).astype(I32)
    rows = n_tiles * tm
    pos_blocks = pos.reshape(n // td, 1, 2 * td)

    xs = pl.pallas_call(
        _dispatch_kernel,
        grid=(n // td,),
        in_specs=[pl.BlockSpec((1, 1, 2 * td), lambda i: (i, 0, 0), memory_space=pltpu.SMEM),
                  pl.BlockSpec((td, d), row),
                  pl.BlockSpec(memory_space=pl.ANY)],
        out_specs=pl.BlockSpec(memory_space=pl.ANY),
        out_shape=jax.ShapeDtypeStruct((rows, d), F32),
        scratch_shapes=[pltpu.SemaphoreType.DMA(())],
        input_output_aliases={2: 0},
        compiler_params=_cp(1),
        name="moe_dispatch",
    )(pos_blocks, h, jnp.zeros((rows, d), F32))

    nj = f // tf

    def x_map(i, j, te, nv, nu):
        return (jnp.minimum(i, nu[0] - 1), 0)

    def w13_map(i, j, te, nv, nu):
        return (layer, te[i], 0, jnp.where(i < nu[0], j, nj - 1))

    def w2_map(i, j, te, nv, nu):
        return (layer, te[i], jnp.where(i < nu[0], j, nj - 1), 0)

    ys = pl.pallas_call(
        functools.partial(_experts_kernel, fc=fc),
        grid_spec=pltpu.PrefetchScalarGridSpec(
            num_scalar_prefetch=3,
            grid=(n_tiles, nj),
            in_specs=[pl.BlockSpec((tm, d), x_map),
                      pl.BlockSpec((None, None, d, tf), w13_map),
                      pl.BlockSpec((None, None, d, tf), w13_map),
                      pl.BlockSpec((None, None, tf, d), w2_map)],
            out_specs=pl.BlockSpec((tm, d), lambda i, j, te, nv, nu: (i, 0)),
            scratch_shapes=[pltpu.VMEM((tm, d), BF16), pltpu.VMEM((d, tf), BF16), pltpu.VMEM((d, tf), BF16),
                            pltpu.VMEM((tf, d), BF16)],
        ),
        out_shape=jax.ShapeDtypeStruct((rows, d), F32),
        compiler_params=_cp(2),
        name="moe_experts",
    )(tile_e, tile_rows, n_used.reshape(1), xs, w1, w3, w2)

    tc = td
    last = n // tc - 1
    return pl.pallas_call(
        functools.partial(_combine_kernel, alpha=alpha),
        grid=(n // tc,),
        in_specs=[pl.BlockSpec((1, 1, 2 * tc), lambda i: (i, 0, 0), memory_space=pltpu.SMEM),
                  pl.BlockSpec((1, 1, 2 * tc), lambda i: (jnp.minimum(i + 1, last), 0, 0), memory_space=pltpu.SMEM),
                  pl.BlockSpec((tc, LANE), row),
                  pl.BlockSpec((tc, d), row),
                  _resident((1, d)), _resident((1, d)),
                  pl.BlockSpec(memory_space=pl.ANY)],
        out_specs=[pl.BlockSpec((tc, d), row), pl.BlockSpec((tc, d), row)],
        out_shape=[jax.ShapeDtypeStruct((n, d), F32), jax.ShapeDtypeStruct((n, d), BF16)],
        scratch_shapes=[pltpu.VMEM((2, 2, tc, d), F32), pltpu.SemaphoreType.DMA((2,))],
        compiler_params=_cp(1),
        name="moe_combine",
    )(pos_blocks, pos_blocks, info, h, g, b, ys)


def _diff_mixer(hb, wq, wk, wv, lq1, lk1, lq2, lk2, sub_g, *, batch, seq, lambda_init):
    d = wq.shape[0]
    heads = d // LANE
    qkv = _linear(hb, (wq.astype(BF16), wk.astype(BF16), wv.astype(BF16)), name="diff_qkv")
    extra = tuple(a.reshape(1, -1).astype(F32) for a in (lq1, lk1, lq2, lk2, sub_g))
    return _flash("diff", qkv, qkv, qkv, batch=batch, seq=seq, groups=heads, q_blk0=0, k_blk0=heads,
                  v_blk0=2 * heads, extra=extra, lambda_init=lambda_init)


def _band_mixer(h, hb, w_qkv, rel_bias, wo, g, b, *, batch, seq, alpha):
    d = w_qkv.shape[0]
    qkv = _linear(hb, (w_qkv.astype(BF16),), name="band_qkv")
    o = _band_attention(qkv, _band_bias_table(rel_bias), batch=batch, seq=seq, pairs=d // LANE)
    return _proj_res_ln(o, wo.astype(BF16), h, g, b, alpha=alpha, name="band_out")


def _mla_mixer(hb, w_dq, q_g, w_uq, w_dkv, kv_g, w_ukv, *, batch, seq, tm=512):
    n, d = hb.shape
    q_rank = w_dq.shape[1]
    kv_rank = kv_g.shape[-1]
    heads = w_ukv.shape[1] // (2 * CHUNK)
    nope, rope = 64, 2 * ROPE_HALF
    tm = min(tm, seq)
    row = lambda i: (i, 0)
    tab = lambda i: (i % (seq // tm), 0)

    inv_freq = ROPE_THETA ** (-jnp.arange(ROPE_HALF, dtype=F32) / ROPE_HALF)
    ang = jnp.arange(seq).astype(F32)[:, None] * inv_freq[None, :]
    cos, sin = jnp.cos(ang), jnp.sin(ang)
    z = lambda w: jnp.zeros((seq, w), F32)
    ct_q = jnp.concatenate([jnp.ones((seq, nope), F32), cos, cos, z(LANE - nope - rope)], axis=1)
    st_q = jnp.concatenate([z(nope), sin, sin, z(LANE - nope - rope)], axis=1)
    ct_k = jnp.concatenate([cos, cos, z(LANE - rope)], axis=1)
    st_k = jnp.concatenate([sin, sin, z(LANE - rope)], axis=1)

    wq3 = w_uq.reshape(q_rank, heads, nope + rope)
    qn, q1, q2 = wq3[..., :nope], wq3[..., nope:nope + ROPE_HALF], wq3[..., nope + ROPE_HALF:]
    zq = lambda w: jnp.zeros((q_rank, heads, w), w_uq.dtype)
    wa = jnp.concatenate([qn, q1, q2, zq(LANE - nope - rope)], axis=2).reshape(q_rank, heads * LANE)
    wb = jnp.concatenate([zq(nope), -q2, q1, zq(LANE - nope - rope)], axis=2).reshape(q_rank, heads * LANE)

    k1, k2 = w_dkv[:, kv_rank:kv_rank + ROPE_HALF], w_dkv[:, kv_rank + ROPE_HALF:]
    zd = lambda w: jnp.zeros((d, w), w_dkv.dtype)
    wd = jnp.concatenate([w_dkv[:, :kv_rank], k1, k2, zd(LANE - rope), -k2, k1, zd(LANE - rope)], axis=1)

    wkv3 = w_ukv.reshape(kv_rank, heads, nope + CHUNK)
    zk = lambda r, w: jnp.zeros((r, heads, w), w_ukv.dtype)
    k_top = jnp.concatenate([wkv3[..., :nope], zk(kv_rank, LANE - nope)], axis=2).reshape(kv_rank, heads * LANE)
    eye = jnp.broadcast_to(jnp.eye(rope, dtype=w_ukv.dtype)[:, None, :], (rope, heads, rope))
    k_rope = jnp.concatenate([zk(rope, nope), eye, zk(rope, LANE - nope - rope)], axis=2).reshape(rope, heads * LANE)
    v_top = wkv3[..., nope:].reshape(kv_rank, heads * CHUNK)
    wkv = jnp.concatenate([
        jnp.concatenate([k_top, v_top], axis=1),
        jnp.concatenate([k_rope, jnp.zeros((rope, heads * CHUNK), w_ukv.dtype)], axis=1),
        jnp.zeros((LANE - rope, heads * (LANE + CHUNK)), w_ukv.dtype)], axis=0)

    cq = pl.pallas_call(
        _mla_cq_kernel,
        grid=(n // tm,),
        in_specs=[pl.BlockSpec((tm, d), row), _resident((d, q_rank)), _resident((1, q_rank))],
        out_specs=pl.BlockSpec((tm, q_rank), row),
        out_shape=jax.ShapeDtypeStruct((n, q_rank), BF16),
        compiler_params=_cp(1),
        name="mla_cq",
    )(hb, w_dq.astype(BF16), q_g.reshape(1, -1))

    q = pl.pallas_call(
        functools.partial(_mla_q_kernel, scale=float((nope + rope) ** -0.5), heads=heads),
        grid=(n // tm,),
        in_specs=[pl.BlockSpec((tm, q_rank), row), _resident((q_rank, heads * LANE)),
                  _resident((q_rank, heads * LANE)), pl.BlockSpec((tm, LANE), tab), pl.BlockSpec((tm, LANE), tab)],
        out_specs=pl.BlockSpec((tm, heads * LANE), row),
        out_shape=jax.ShapeDtypeStruct((n, heads * LANE), BF16),
        compiler_params=_cp(1),
        name="mla_q",
    )(cq, wa.astype(BF16), wb.astype(BF16), ct_q, st_q)

    ckr = pl.pallas_call(
        functools.partial(_mla_ckv_kernel, rank=kv_rank),
        grid=(n // tm,),
        in_specs=[pl.BlockSpec((tm, d), row), _resident((d, kv_rank + 2 * LANE)), _resident((1, kv_rank)),
                  pl.BlockSpec((tm, LANE), tab), pl.BlockSpec((tm, LANE), tab)],
        out_specs=pl.BlockSpec((tm, kv_rank + LANE), row),
        out_shape=jax.ShapeDtypeStruct((n, kv_rank + LANE), BF16),
        compiler_params=_cp(1),
        name="mla_ckv",
    )(hb, wd.astype(BF16), kv_g.reshape(1, -1), ct_k, st_k)

    kv = _linear(ckr, (wkv.astype(BF16),), name="mla_kv")
    return _flash("mla", q, kv, kv, batch=batch, seq=seq, groups=heads // 2, q_blk0=0, k_blk0=0, v_blk0=heads)


def _sg_mixer(h, hb, w_in, vg, vb, w_s, b_s, w_out, g, b, *, alpha, tm=512):
    n, d = hb.shape
    width = w_out.shape[0]
    groups = w_s.shape[0]
    tm = min(tm, n)
    row = lambda i: (i, 0)
    u, v = pl.pallas_call(
        _sg_in_kernel,
        grid=(n // tm,),
        in_specs=[pl.BlockSpec((tm, d), row), _resident((d, 2 * width)), _resident((1, width)),
                  _resident((1, width))],
        out_specs=[pl.BlockSpec((tm, width), row), pl.BlockSpec((tm, width), row)],
        out_shape=[jax.ShapeDtypeStruct((n, width), BF16), jax.ShapeDtypeStruct((n, width), BF16)],
        compiler_params=_cp(1),
        name="sg_in",
    )(hb, w_in.astype(BF16), vg.reshape(1, -1), vb.reshape(1, -1))
    bs_full = jnp.repeat(b_s.T.astype(F32), width // groups, axis=1)
    return pl.pallas_call(
        functools.partial(_sg_mix_kernel, alpha=alpha, groups=groups),
        grid=(n // tm,),
        in_specs=[pl.BlockSpec((tm, width), row), pl.BlockSpec((tm, width), row),
                  _resident((groups, GMLP_CHUNK, GMLP_CHUNK)), _resident((GMLP_CHUNK, width)),
                  _resident((width, d)), pl.BlockSpec((tm, d), row), _resident((1, d)), _resident((1, d))],
        out_specs=[pl.BlockSpec((tm, d), row), pl.BlockSpec((tm, d), row)],
        out_shape=[jax.ShapeDtypeStruct((n, d), F32), jax.ShapeDtypeStruct((n, d), BF16)],
        scratch_shapes=[pltpu.VMEM((tm, width), BF16)],
        compiler_params=_cp(1),
        name="sg_mix",
    )(u, v, w_s.astype(F32), bs_full, w_out.astype(BF16), h, g, b)


def kernel(x, ln_mix_g, ln_mix_b, ln_ffn_g, ln_ffn_b, diff_wq, diff_wk, diff_wv, diff_lq1, diff_lk1, diff_lq2, diff_lk2, diff_sub_g, diff_wo, ca_w_qkv, ca_rel_bias, ca_wo, mla_w_dq, mla_q_norm_g, mla_w_uq, mla_w_dkv, mla_kv_norm_g, mla_w_ukv, mla_wo, sg_w_in, sg_v_norm_g, sg_v_norm_b, sg_w_s, sg_b_s, sg_w_out, ffn_w1, ffn_w3, ffn_w2, moe_w_router, moe_w1, moe_w3, moe_w2):
    batch, seq, d = x.shape
    depth = ln_mix_g.shape[0]
    alpha = float((2 * depth) ** 0.25)
    h = x.reshape(batch * seq, d).astype(F32)
    hb = h
    vec = lambda a: a.reshape(1, -1).astype(F32)
    for i in range(depth):
        kind, j = i % 4, i // 4
        mg, mb = vec(ln_mix_g[i]), vec(ln_mix_b[i])
        fg, fb = vec(ln_ffn_g[i]), vec(ln_ffn_b[i])
        if kind in (0, 2):
            if kind == 0:
                a = _diff_mixer(hb, diff_wq[j], diff_wk[j], diff_wv[j], diff_lq1[j], diff_lk1[j], diff_lq2[j],
                                diff_lk2[j], diff_sub_g[j], batch=batch, seq=seq,
                                lambda_init=0.8 - 0.6 * math.exp(-0.3 * i))
                wo = diff_wo[j]
            else:
                a = _mla_mixer(hb, mla_w_dq[j], mla_q_norm_g[j], mla_w_uq[j], mla_w_dkv[j], mla_kv_norm_g[j],
                               mla_w_ukv[j], batch=batch, seq=seq)
                wo = mla_wo[j]
            h, hb = _mix_ffn(a, wo.astype(BF16), mg, mb, ffn_w1[i // 2].astype(BF16), ffn_w3[i // 2].astype(BF16),
                             ffn_w2[i // 2].astype(BF16), h, fg, fb, alpha=alpha)
        else:
            if kind == 1:
                h, hb = _band_mixer(h, hb, ca_w_qkv[j], ca_rel_bias[j], ca_wo[j], mg, mb, batch=batch, seq=seq,
                                    alpha=alpha)
            else:
                h, hb = _sg_mixer(h, hb, sg_w_in[j], sg_v_norm_g[j], sg_v_norm_b[j], sg_w_s[j], sg_b_s[j],
                                  sg_w_out[j], mg, mb, alpha=alpha)
            h, hb = _moe(h, moe_w_router[i // 2], moe_w1, moe_w3, moe_w2, fg, fb, layer=i // 2, alpha=alpha)
    return h.reshape(batch, seq, d).astype(x.dtype)
```

```python
import functools
import math

import jax
import jax.numpy as jnp
import numpy as np
from jax import lax
from jax.experimental import pallas as pl
from jax.experimental.pallas import tpu as pltpu

F32 = jnp.float32
BF16 = jnp.bfloat16
I32 = jnp.int32

LANE = 128
VMEM_LIMIT = 56 * 1024 * 1024

CHUNK = 64
N_EXPERTS = 8
LN_EPS = 1e-5
RMS_EPS = 1e-6
NEG_INF = -1e30
ROPE_THETA = 10000.0
CA_LEFT = 8
CA_REL_CLIP = 128
ROPE_HALF = 16


def _cp(n_axes, vmem=VMEM_LIMIT):
    return pltpu.CompilerParams(dimension_semantics=("arbitrary",) * n_axes, vmem_limit_bytes=vmem)


def _resident(shape):
    nd = len(shape)
    return pl.BlockSpec(shape, lambda *_: (0,) * nd, pipeline_mode=pl.Buffered(1))


def _res_ln(h, m, g, b, alpha):
    z = alpha * h + m
    mu = jnp.mean(z, axis=-1, keepdims=True)
    zc = z - mu
    var = jnp.mean(zc * zc, axis=-1, keepdims=True)
    return zc * lax.rsqrt(var + LN_EPS) * g + b


def _rms(x, g, eps):
    ms = jnp.mean(x * x, axis=-1, keepdims=True)
    return x * lax.rsqrt(ms + eps) * g


def _linear_kernel(x_ref, *refs):
    *w_refs, o_ref = refs
    x = x_ref[...].astype(BF16)
    col = 0
    for w_ref in w_refs:
        n = w_ref.shape[1]
        o_ref[:, col:col + n] = jnp.dot(x, w_ref[...], preferred_element_type=F32).astype(o_ref.dtype)
        col += n


def _linear(x, ws, *, tm=512, out_dtype=BF16, name="linear"):
    m, k = x.shape
    n = sum(w.shape[1] for w in ws)
    tm = min(tm, m)
    return pl.pallas_call(
        _linear_kernel,
        grid=(m // tm,),
        in_specs=[pl.BlockSpec((tm, k), lambda i: (i, 0))] + [_resident(w.shape) for w in ws],
        out_specs=pl.BlockSpec((tm, n), lambda i: (i, 0)),
        out_shape=jax.ShapeDtypeStruct((m, n), out_dtype),
        compiler_params=_cp(1),
        name=name,
    )(x, *ws)


ROUTE_TILE = 512


def _route_outputs(m, tm):
    specs = [pl.BlockSpec((tm, LANE), lambda i: (i, 0)), pl.BlockSpec((1, LANE), lambda i: (0, 0))]
    shapes = [jax.ShapeDtypeStruct((m, LANE), F32), jax.ShapeDtypeStruct((1, LANE), F32)]
    scratch = [pltpu.VMEM((1, LANE), F32), pltpu.VMEM((tm, tm), BF16)]
    return specs, shapes, scratch


def _proj_route_kernel(a_ref, w_ref, h_ref, g_ref, b_ref, wr_ref, oh_ref, info_ref, cnt_ref, run_ref, before_ref, *,
                       alpha):
    pl.when(pl.program_id(0) == 0)(lambda: _route_init(run_ref, before_ref))
    m = jnp.dot(a_ref[...], w_ref[...], preferred_element_type=F32)
    y = _res_ln(h_ref[...], m, g_ref[...], b_ref[...], alpha)
    oh_ref[...] = y
    _route(y, wr_ref, run_ref, before_ref, info_ref, cnt_ref)


def _proj_res_ln_route(a, w, h, g, b, wr, *, alpha, name):
    m, k = a.shape
    d = w.shape[1]
    tm = min(ROUTE_TILE, m)
    row = lambda i: (i, 0)
    r_specs, r_shapes, r_scratch = _route_outputs(m, tm)
    return pl.pallas_call(
        functools.partial(_proj_route_kernel, alpha=alpha),
        grid=(m // tm,),
        in_specs=[pl.BlockSpec((tm, k), row), _resident((k, d)), pl.BlockSpec((tm, d), row),
                  _resident((1, d)), _resident((1, d)), _resident((d, LANE))],
        out_specs=[pl.BlockSpec((tm, d), row)] + r_specs,
        out_shape=[jax.ShapeDtypeStruct((m, d), F32)] + r_shapes,
        scratch_shapes=r_scratch,
        compiler_params=_cp(1),
        name=name,
    )(a, w, h, g, b, wr)


def _mix_ffn_kernel(a_ref, wo_ref, mg_ref, mb_ref, w1_ref, w3_ref, w2_ref, h_ref, g_ref, b_ref, oh_ref, ohb_ref,
                    hm_ref, xb_ref, acc_ref, *, alpha, fc):
    mix = jnp.dot(a_ref[...], wo_ref[...], preferred_element_type=F32)
    hm = _res_ln(h_ref[...], mix, mg_ref[...], mb_ref[...], alpha)
    hm_ref[...] = hm
    xb_ref[...] = hm.astype(BF16)
    x = xb_ref[...]
    f = w1_ref.shape[1]
    for c in range(f // fc):
        sl = slice(c * fc, (c + 1) * fc)
        a = jnp.dot(x, w1_ref[:, sl], preferred_element_type=F32)
        b3 = jnp.dot(x, w3_ref[:, sl], preferred_element_type=F32)
        gated = (jax.nn.silu(a) * b3).astype(BF16)
        part = jnp.dot(gated, w2_ref[sl, :], preferred_element_type=F32)
        if c == 0:
            acc_ref[...] = part
        else:
            acc_ref[...] += part
    y = _res_ln(hm_ref[...], acc_ref[...], g_ref[...], b_ref[...], alpha)
    oh_ref[...] = y
    ohb_ref[...] = y.astype(BF16)


def _mix_ffn(a, wo, mg, mb, w1, w3, w2, h, g, b, *, alpha, tm=512, fc=256):
    m, d = h.shape
    k = a.shape[1]
    f = w1.shape[1]
    tm = min(tm, m)
    row = lambda i: (i, 0)
    vecs = [_resident((1, d)), _resident((1, d))]
    return pl.pallas_call(
        functools.partial(_mix_ffn_kernel, alpha=alpha, fc=fc),
        grid=(m // tm,),
        in_specs=[pl.BlockSpec((tm, k), row), _resident((k, d))] + vecs
                 + [_resident((d, f)), _resident((d, f)), _resident((f, d)), pl.BlockSpec((tm, d), row)] + vecs,
        out_specs=[pl.BlockSpec((tm, d), row), pl.BlockSpec((tm, d), row)],
        out_shape=[jax.ShapeDtypeStruct((m, d), F32), jax.ShapeDtypeStruct((m, d), BF16)],
        scratch_shapes=[pltpu.VMEM((tm, d), F32), pltpu.VMEM((tm, d), BF16), pltpu.VMEM((tm, d), F32)],
        compiler_params=_cp(1),
        name="mix_ffn",
    )(a, wo, mg, mb, w1, w3, w2, h, g, b)


ONES_ROWS = 16
ACC_ROWS = LANE + ONES_ROWS


def _vt_ext(v_blk):
    vt = v_blk.astype(F32).T.astype(BF16)
    return jnp.concatenate([vt, jnp.ones((ONES_ROWS, v_blk.shape[0]), BF16)], axis=0)


def _split_heads(x):
    low = lax.broadcasted_iota(I32, x.shape, 1) < CHUNK
    zero = jnp.zeros_like(x)
    return jnp.where(low, x, zero), jnp.where(low, zero, x)


_NT = (((1,), (1,)), ((), ()))


def _flash_kernel(*refs, mode, tq, gs, lambda_init):
    if mode == "diff":
        lq1_ref, lk1_ref, lq2_ref, lk2_ref, subg_ref, q_ref, k_ref, v_ref, o_ref, *scratch = refs
    else:
        q_ref, k_ref, v_ref, o_ref, *scratch = refs
    vt_ref, m_ref, a_ref, bm_ref, acc_ref, s_ref, p_ref = scratch
    qi = pl.program_id(2)
    tk = tq // 2
    nmap = 2 * gs
    lanes = lambda i: slice(i * LANE, (i + 1) * LANE)

    @pl.when(qi == 0)
    def _():
        for j in range(vt_ref.shape[0]):
            for g in range(gs):
                vt_ref[j, g] = _vt_ext(v_ref[j * tk:(j + 1) * tk, lanes(g)])

    qs = []
    for g in range(gs):
        if mode == "diff":
            qs += list(_split_heads(q_ref[:, lanes(g)] * jnp.asarray(CHUNK ** -0.5, BF16)))
        else:
            qs += [q_ref[:, lanes(2 * g)], q_ref[:, lanes(2 * g + 1)]]
    m_ref[...] = jnp.full(m_ref.shape, NEG_INF, F32)
    acc_ref[...] = jnp.zeros(acc_ref.shape, F32)

    def scores(blk):
        start = pl.multiple_of(blk * tk, tk)
        kb = k_ref[pl.ds(start, tk), :]
        kks = [kb[:, lanes(mp // 2 if mode == "diff" else mp)] for mp in range(nmap)]
        return [lax.dot_general(kks[mp], qs[mp], _NT, preferred_element_type=F32) for mp in range(nmap)]

    def values(blk, slot):
        return [jnp.dot(vt_ref[blk, mp // 2], p_ref[slot, mp], preferred_element_type=F32) for mp in range(nmap)]

    def stash(slot, ss):
        for mp in range(nmap):
            s_ref[slot, mp] = ss[mp]
            bm_ref[slot, mp] = jnp.max(ss[mp], axis=0, keepdims=True)

    def softmax(slot, masked):
        for mp in range(nmap):
            s = s_ref[slot, mp]
            if masked:
                kc = (lax.broadcasted_iota(I32, s.shape, 0) + slot * tk) >> 6
                qc = lax.broadcasted_iota(I32, s.shape, 1) >> 6
                s = jnp.where(kc <= qc, s, NEG_INF)
                blk_max = jnp.max(s, axis=0, keepdims=True)
            else:
                blk_max = bm_ref[slot, mp]
            m_old = m_ref[mp]
            m_new = jnp.maximum(m_old, blk_max)
            p_ref[slot, mp] = jnp.exp(s - m_new).astype(BF16)
            a_ref[slot, mp] = jnp.exp(m_old - m_new)
            m_ref[mp] = m_new

    def fold(pv, scale):
        for mp in range(nmap):
            acc_ref[mp] = scale[mp] * acc_ref[mp] + pv[mp]

    def trip(blk, masked, produce):
        scale = [[a_ref[u, mp] for mp in range(nmap)] for u in range(2)]
        pvs = [values(jnp.maximum(blk - 2 + u, 0), u) for u in range(2)]
        nxt = [scores(blk + 2 + u) for u in range(2)] if produce else []
        for u in range(2):
            softmax(u, masked)
        for u in range(2):
            fold(pvs[u], scale[u])
        for u, s in enumerate(nxt):
            stash(u, s)

    for u in range(2):
        stash(u, scores(u))
    p_ref[...] = jnp.zeros(p_ref.shape, BF16)
    a_ref[...] = jnp.ones(a_ref.shape, F32)

    def step(j, carry):
        trip(2 * j, False, True)
        return carry

    lax.fori_loop(0, qi, step, 0)
    trip(2 * qi, True, False)
    for u in range(2):
        fold(values(2 * qi + u, u), [a_ref[u, mp] for mp in range(nmap)])

    for g in range(gs):
        acc_a, acc_b = acc_ref[2 * g], acc_ref[2 * g + 1]
        oa = acc_a[:LANE] / acc_a[LANE:LANE + 1]
        ob = acc_b[:LANE] / acc_b[LANE:LANE + 1]
        if mode == "diff":
            lam = (jnp.exp(jnp.sum(lq1_ref[...] * lk1_ref[...], axis=-1, keepdims=True))
                   - jnp.exp(jnp.sum(lq2_ref[...] * lk2_ref[...], axis=-1, keepdims=True)) + lambda_init)
            o = (oa - lam * ob).T
            o = _rms(o, subg_ref[...], 1e-5) * (1.0 - lambda_init)
        else:
            row = lax.broadcasted_iota(I32, oa.shape, 0)
            o = jnp.where(row < CHUNK, oa, ob).T
        o_ref[:, lanes(g)] = o.astype(o_ref.dtype)


def _flash(mode, q_arr, k_arr, v_arr, *, batch, seq, groups, q_blk0, k_blk0, v_blk0, extra=(), lambda_init=0.0,
           tq=512, gs=2):
    qw = (LANE if mode == "diff" else 2 * LANE) * gs
    tq = min(tq, seq)
    tk = tq // 2
    nq = seq // tq
    n = batch * seq
    nmap = 2 * gs
    assert groups % gs == 0 and q_blk0 % gs == 0 and k_blk0 % gs == 0 and v_blk0 % gs == 0
    qb, kb, vb = q_blk0 // gs, k_blk0 // gs, v_blk0 // gs
    in_specs = [pl.BlockSpec(e.shape, lambda b, g, i: (0, 0)) for e in extra]
    in_specs += [
        pl.BlockSpec((tq, qw), lambda b, g, i: (b * nq + i, qb + g)),
        pl.BlockSpec((seq, qw), lambda b, g, i: (b, kb + g)),
        pl.BlockSpec((seq, gs * LANE), lambda b, g, i: (b, vb + g)),
    ]
    return pl.pallas_call(
        functools.partial(_flash_kernel, mode=mode, tq=tq, gs=gs, lambda_init=lambda_init),
        grid=(batch, groups // gs, nq),
        in_specs=in_specs,
        out_specs=pl.BlockSpec((tq, gs * LANE), lambda b, g, i: (b * nq + i, g)),
        out_shape=jax.ShapeDtypeStruct((n, groups * LANE), BF16),
        scratch_shapes=[pltpu.VMEM((seq // tk, gs, ACC_ROWS, tk), BF16), pltpu.VMEM((nmap, 1, tq), F32),
                        pltpu.VMEM((2, nmap, 1, tq), F32), pltpu.VMEM((2, nmap, 1, tq), F32),
                        pltpu.VMEM((nmap, ACC_ROWS, tq), F32),
                        pltpu.VMEM((2, nmap, tk, tq), F32), pltpu.VMEM((2, nmap, tk, tq), BF16)],
        compiler_params=_cp(3),
        name="flash_" + mode,
    )(*extra, q_arr, k_arr, v_arr)


BAND_TQ = 4 * CHUNK
BAND_W = (CA_LEFT + 4) * CHUNK
BAND_PAD = CA_LEFT * CHUNK
BAND_NW = BAND_W // BAND_TQ
BAND_EXT = 1024


def _band_kernel(q_ref, qn_ref, k_ref, v_ref, ext_ref, o_ref, kpad_ref, vt_ref, s_ref, bm_ref, bias_ref, *, seq,
                 gp):
    t = pl.program_id(2)
    nt = pl.num_programs(2)
    npad = BAND_PAD // BAND_TQ

    nh = 2 * gp
    lanes = lambda i: slice(i * LANE, (i + 1) * LANE)

    def produce(qr, tile, slot, front):
        qhs = []
        for g in range(gp):
            qhs += list(_split_heads(qr[:, lanes(g)] * jnp.asarray(CHUNK ** -0.5, BF16)))
        tops = [None] * nh
        for u in range(BAND_NW):
            start = pl.multiple_of((tile + u) * BAND_TQ, BAND_TQ)
            kb = kpad_ref[pl.ds(start, BAND_TQ), :]
            for hh in range(nh):
                s = lax.dot_general(kb[:, lanes(hh // 2)], qhs[hh], _NT, preferred_element_type=F32)
                s = s + bias_ref[hh, u * BAND_TQ:(u + 1) * BAND_TQ, :]
                if front:
                    invalid = lax.broadcasted_iota(I32, s.shape, 0) + (tile + u) * BAND_TQ < BAND_PAD
                    s = jnp.where(invalid, NEG_INF, s)
                s_ref[slot, hh, u] = s
                top = jnp.max(s, axis=0, keepdims=True)
                tops[hh] = top if tops[hh] is None else jnp.maximum(tops[hh], top)
        for hh in range(nh):
            bm_ref[slot, hh] = tops[hh]

    @pl.when((t == 0) & (pl.program_id(1) == 0))
    def _():
        ci = lax.broadcasted_iota(I32, (BAND_W, BAND_TQ), 1) >> 6
        cj = lax.broadcasted_iota(I32, (BAND_W, BAND_TQ), 0) >> 6
        vis = (cj >= ci) & (cj <= ci + CA_LEFT)
        for hh in range(nh):
            rows = jnp.broadcast_to(ext_ref[hh // 2, hh % 2], (BAND_W, BAND_EXT))
            skew = pltpu.roll(rows, BAND_EXT - (BAND_W - 1), 1, stride=1, stride_axis=0)
            bias_ref[hh] = jnp.where(vis, skew[:, :BAND_TQ], NEG_INF)

    @pl.when(t == 0)
    def _():
        kpad_ref[0:BAND_PAD, :] = jnp.zeros((BAND_PAD, gp * LANE), BF16)
        kpad_ref[BAND_PAD:BAND_PAD + seq, :] = k_ref[...]
        for j in range(npad):
            vt_ref[j] = jnp.zeros((gp, ACC_ROWS, BAND_TQ), BF16)
        for j in range(seq // BAND_TQ):
            for g in range(gp):
                vt_ref[npad + j, g] = _vt_ext(v_ref[j * BAND_TQ:(j + 1) * BAND_TQ, lanes(g)])
        produce(q_ref, 0, 0, True)

    cur = t & 1
    nxt_tile = jnp.minimum(t + 1, nt - 1)

    def consume():
        outs = []
        for hh in range(nh):
            m = bm_ref[cur, hh]
            acc = None
            for u in range(BAND_NW):
                p = jnp.exp(s_ref[cur, hh, u] - m).astype(BF16)
                pv = jnp.dot(vt_ref[t + u, hh // 2], p, preferred_element_type=F32)
                acc = pv if acc is None else acc + pv
            outs.append(acc[:LANE] / acc[LANE:LANE + 1])
        row = lax.broadcasted_iota(I32, outs[0].shape, 0)
        for g in range(gp):
            o_ref[:, lanes(g)] = jnp.where(row < CHUNK, outs[2 * g], outs[2 * g + 1]).T.astype(o_ref.dtype)

    def run(front):
        consume()
        produce(qn_ref, nxt_tile, 1 - cur, front)

    pl.when(t + 1 < npad)(lambda: run(True))
    pl.when(t + 1 >= npad)(lambda: run(False))


def _band_attention(qkv, ext, *, batch, seq, pairs, gp=2):
    n = batch * seq
    nt = seq // BAND_TQ
    assert pairs % gp == 0
    steps, w, nh = pairs // gp, gp * LANE, 2 * gp
    return pl.pallas_call(
        functools.partial(_band_kernel, seq=seq, gp=gp),
        grid=(steps, batch, nt),
        in_specs=[
            pl.BlockSpec((BAND_TQ, w), lambda g, b, t: (b * nt + t, g)),
            pl.BlockSpec((BAND_TQ, w), lambda g, b, t: (b * nt + jnp.minimum(t + 1, nt - 1), g)),
            pl.BlockSpec((seq, w), lambda g, b, t: (b, steps + g)),
            pl.BlockSpec((seq, w), lambda g, b, t: (b, 2 * steps + g)),
            pl.BlockSpec((gp, 2, 1, BAND_EXT), lambda g, b, t: (g, 0, 0, 0)),
        ],
        out_specs=pl.BlockSpec((BAND_TQ, w), lambda g, b, t: (b * nt + t, g)),
        out_shape=jax.ShapeDtypeStruct((n, pairs * LANE), BF16),
        scratch_shapes=[pltpu.VMEM((seq + BAND_PAD, w), BF16),
                        pltpu.VMEM(((seq + BAND_PAD) // BAND_TQ, gp, ACC_ROWS, BAND_TQ), BF16),
                        pltpu.VMEM((2, nh, BAND_NW, BAND_TQ, BAND_TQ), F32),
                        pltpu.VMEM((2, nh, 1, BAND_TQ), F32),
                        pltpu.VMEM((nh, BAND_W, BAND_TQ), F32)],
        compiler_params=_cp(3),
        name="band_attention",
    )(qkv, qkv, qkv, qkv, ext)


def _band_bias_table(rel_bias):
    heads = rel_bias.shape[0]
    rel = np.minimum(np.arange(BAND_EXT), BAND_TQ + BAND_W - 2) - (BAND_W - 1) + BAND_PAD
    ext = rel_bias.astype(F32)[:, np.clip(rel, -CA_REL_CLIP, CA_REL_CLIP) + CA_REL_CLIP]
    return ext.reshape(heads // 2, 2, 1, BAND_EXT)


def _mla_cq_kernel(x_ref, w_ref, g_ref, o_ref):
    c = jnp.dot(x_ref[...], w_ref[...], preferred_element_type=F32)
    o_ref[...] = _rms(c, g_ref[...], RMS_EPS).astype(o_ref.dtype)


def _mla_q_kernel(c_ref, wa_ref, wb_ref, ct_ref, st_ref, o_ref, *, scale, heads):
    c = c_ref[...]
    a = jnp.dot(c, wa_ref[...], preferred_element_type=F32)
    b = jnp.dot(c, wb_ref[...], preferred_element_type=F32)
    ct = ct_ref[...] * scale
    st = st_ref[...] * scale
    for h in range(heads):
        sl = slice(h * LANE, (h + 1) * LANE)
        o_ref[:, sl] = (a[:, sl] * ct + b[:, sl] * st).astype(o_ref.dtype)


def _mla_ckv_kernel(x_ref, w_ref, g_ref, ct_ref, st_ref, o_ref, *, rank):
    y = jnp.dot(x_ref[...], w_ref[...], preferred_element_type=F32)
    o_ref[:, :rank] = _rms(y[:, :rank], g_ref[...], RMS_EPS).astype(o_ref.dtype)
    kr = y[:, rank:rank + LANE] * ct_ref[...] + y[:, rank + LANE:rank + 2 * LANE] * st_ref[...]
    o_ref[:, rank:] = kr.astype(o_ref.dtype)


GMLP_CHUNK = 128


def _sg_in_kernel(x_ref, w_ref, g_ref, b_ref, u_ref, v_ref, vbuf_ref, *, nc):
    width = u_ref.shape[1]
    x = x_ref[...]
    for c in range(0, 2 * width, nc):
        hc = jax.nn.gelu(jnp.dot(x, w_ref[:, c:c + nc], preferred_element_type=F32))
        if c < width:
            u_ref[:, c:c + nc] = hc.astype(u_ref.dtype)
        else:
            vbuf_ref[:, c - width:c - width + nc] = hc
    v = vbuf_ref[...]
    mu = jnp.mean(v, axis=-1, keepdims=True)
    vc = v - mu
    var = jnp.mean(vc * vc, axis=-1, keepdims=True)
    v_ref[...] = (vc * lax.rsqrt(var + LN_EPS) * g_ref[...] + b_ref[...]).astype(v_ref.dtype)


def _sg_mix_kernel(u_ref, v_ref, ws_ref, bs_ref, wo_ref, h_ref, g_ref, b_ref, wr_ref, oh_ref, info_ref, cnt_ref,
                   gated_ref, run_ref, before_ref, *, alpha, groups):
    tm = u_ref.shape[0]
    pl.when(pl.program_id(0) == 0)(lambda: _route_init(run_ref, before_ref))
    r = lax.broadcasted_iota(I32, (GMLP_CHUNK, GMLP_CHUNK), 0) >> 6
    c = lax.broadcasted_iota(I32, (GMLP_CHUNK, GMLP_CHUNK), 1) >> 6
    vis = c <= r
    for gi in range(groups):
        w = jnp.where(vis, ws_ref[gi], 0.0).astype(BF16)
        cs = slice(gi * LANE, (gi + 1) * LANE)
        for ch in range(tm // GMLP_CHUNK):
            rs = slice(ch * GMLP_CHUNK, (ch + 1) * GMLP_CHUNK)
            mixed = jnp.dot(w, v_ref[rs, cs], preferred_element_type=F32) + bs_ref[:, cs]
            gated_ref[rs, cs] = (u_ref[rs, cs].astype(F32) * mixed).astype(BF16)
    m = jnp.dot(gated_ref[...], wo_ref[...], preferred_element_type=F32)
    y = _res_ln(h_ref[...], m, g_ref[...], b_ref[...], alpha)
    oh_ref[...] = y
    _route(y, wr_ref, run_ref, before_ref, info_ref, cnt_ref)


INFO_IDX, INFO_RANK, INFO_GATE = 0, 2, 4
ROW_DMA_UNROLL = 8


def _route_init(run_ref, before_ref):
    tr = before_ref.shape[0]
    run_ref[...] = jnp.zeros_like(run_ref)
    rr = lax.broadcasted_iota(I32, (tr, tr), 0)
    cc = lax.broadcasted_iota(I32, (tr, tr), 1)
    before_ref[...] = jnp.where(cc < rr, 1.0, 0.0).astype(BF16)


def _route(x, w_ref, run_ref, before_ref, info_ref, cnt_ref):
    tr = x.shape[0]
    x1 = x.astype(BF16)
    r1 = x - x1.astype(F32)
    x2 = r1.astype(BF16)
    x3 = (r1 - x2.astype(F32)).astype(BF16)
    w = w_ref[...]
    lane = lax.broadcasted_iota(I32, (tr, LANE), 1)
    t = (jnp.dot(x1, w, preferred_element_type=F32)
         + jnp.where(lane < 2 * N_EXPERTS, jnp.dot(x2, w, preferred_element_type=F32), 0.0)
         + jnp.where(lane < N_EXPERTS, jnp.dot(x3, w, preferred_element_type=F32), 0.0))
    logits = t + pltpu.roll(t, LANE - N_EXPERTS, 1) + pltpu.roll(t, LANE - 2 * N_EXPERTS, 1)
    big = jnp.float32(3e38)
    logits = jnp.where(lane < N_EXPERTS, logits, -big)

    lane_f = lane.astype(F32)
    v1 = jnp.max(logits, axis=-1, keepdims=True)
    i1 = jnp.min(jnp.where(logits == v1, lane_f, float(LANE)), axis=-1, keepdims=True)
    rest = jnp.where(lane_f == i1, -big, logits)
    v2 = jnp.max(rest, axis=-1, keepdims=True)
    i2 = jnp.min(jnp.where(rest == v2, lane_f, float(LANE)), axis=-1, keepdims=True)
    e = jnp.exp(v2 - v1)
    g1 = 1.0 / (1.0 + e)
    g2 = e / (1.0 + e)

    oh1 = jnp.where(lane_f == i1, 1.0, 0.0)
    oh2 = jnp.where(lane_f == i2, 1.0, 0.0)
    oh = oh1 + oh2
    prior = jnp.dot(before_ref[...], oh.astype(BF16), preferred_element_type=F32) + run_ref[...]
    rank1 = jnp.sum(prior * oh1, axis=-1, keepdims=True)
    rank2 = jnp.sum(prior * oh2, axis=-1, keepdims=True)
    run_ref[...] += jnp.sum(oh, axis=0, keepdims=True)
    cnt_ref[...] = run_ref[...]

    info = jnp.where(lane == INFO_IDX, i1, 0.0)
    info = jnp.where(lane == INFO_IDX + 1, i2, info)
    info = jnp.where(lane == INFO_RANK, rank1, info)
    info = jnp.where(lane == INFO_RANK + 1, rank2, info)
    info = jnp.where(lane == INFO_GATE, g1, info)
    info = jnp.where(lane == INFO_GATE + 1, g2, info)
    info_ref[...] = info


def _dispatch_kernel(pos_ref, h_ref, init_ref, xs_ref, sem):
    del init_ref
    td = h_ref.shape[0]

    def start(r, carry):
        for k in range(2):
            p = pos_ref[0, 0, 2 * r + k]
            pltpu.make_async_copy(h_ref.at[pl.ds(r, 1), :], xs_ref.at[pl.ds(p, 1), :], sem).start(priority=k)
        return carry

    lax.fori_loop(0, td, start, 0, unroll=ROW_DMA_UNROLL)
    for _ in range(2):
        pltpu.make_async_copy(h_ref, xs_ref.at[pl.ds(0, td), :], sem).wait()


def _experts_kernel(te_ref, nu_ref, x_ref, w1_ref, w3_ref, w2_ref, o_ref, xb_ref, *, fc):
    i = pl.program_id(0)
    j = pl.program_id(1)
    del te_ref

    @pl.when(j == 0)
    def _():
        o_ref[...] = jnp.zeros_like(o_ref)
        xb_ref[...] = x_ref[...].astype(BF16)

    @pl.when(i < nu_ref[0])
    def _():
        x = xb_ref[...]
        tf = w1_ref.shape[1]
        gated = []
        for c in range(tf // fc):
            sl = slice(c * fc, (c + 1) * fc)
            a = jnp.dot(x, w1_ref[:, sl].astype(BF16), preferred_element_type=F32)
            b3 = jnp.dot(x, w3_ref[:, sl].astype(BF16), preferred_element_type=F32)
            gated.append((jax.nn.silu(a) * b3).astype(BF16))
        o_ref[...] += jnp.dot(jnp.concatenate(gated, axis=1), w2_ref[...].astype(BF16),
                              preferred_element_type=F32)


def _combine_kernel(pos_ref, posn_ref, info_ref, h_ref, g_ref, b_ref, ys_ref, oh_ref, ohb_ref, buf_ref, sem, *,
                    alpha):
    i = pl.program_id(0)
    tc = h_ref.shape[0]

    def gather(p_ref, slot):
        def start(r, carry):
            for k in range(2):
                p = p_ref[0, 0, 2 * r + k]
                pltpu.make_async_copy(ys_ref.at[pl.ds(p, 1), :], buf_ref.at[slot, k, pl.ds(r, 1), :],
                                      sem.at[slot]).start(priority=k)
            return carry

        lax.fori_loop(0, tc, start, 0, unroll=ROW_DMA_UNROLL)

    @pl.when(i == 0)
    def _():
        gather(pos_ref, 0)

    @pl.when(i + 1 < pl.num_programs(0))
    def _():
        gather(posn_ref, (i + 1) & 1)

    cur = i & 1
    for k in range(2):
        pltpu.make_async_copy(ys_ref.at[pl.ds(0, tc), :], buf_ref.at[cur, k], sem.at[cur]).wait()
    info = info_ref[...]
    g1 = info[:, INFO_GATE:INFO_GATE + 1]
    g2 = info[:, INFO_GATE + 1:INFO_GATE + 2]
    m = g1 * buf_ref[cur, 0] + g2 * buf_ref[cur, 1]
    y = _res_ln(h_ref[...], m, g_ref[...], b_ref[...], alpha)
    oh_ref[...] = y
    ohb_ref[...] = y.astype(BF16)


def _split3(w):
    hi = w.astype(BF16)
    r = w - hi.astype(F32)
    mid = r.astype(BF16)
    lo = (r - mid.astype(F32)).astype(BF16)
    return hi, mid, lo


def _router_weights(w_router):
    hi, mid, lo = _split3(w_router.astype(F32))
    pad = jnp.zeros((w_router.shape[0], LANE - 3 * N_EXPERTS), BF16)
    return jnp.concatenate([hi, mid, lo, pad], axis=1)


def _moe(h, info, cnt, w1, w3, w2, g, b, *, layer, alpha, tm=1024, tf=512, fc=256, td=256):
    n, d = h.shape
    f = w1.shape[3]
    tm = min(tm, n)
    td = min(td, n)
    tf = min(tf, f)
    row = lambda i: (i, 0)

    idx = info[:, INFO_IDX:INFO_IDX + 2].astype(I32)
    rank = info[:, INFO_RANK:INFO_RANK + 2].astype(I32)
    counts = cnt[0, :N_EXPERTS].astype(I32)
    padded = ((counts + tm - 1) // tm) * tm
    ends = jnp.cumsum(padded)
    starts = ends - padded
    pos = starts[idx] + rank
    n_tiles = (2 * n) // tm + N_EXPERTS
    n_used = (ends[-1] // tm).astype(I32)
    tile_start = jnp.arange(n_tiles, dtype=I32) * tm
    tile_e = jnp.sum(tile_start[:, None] >= ends[None, :], axis=1).astype(I32)
    tile_e = jnp.minimum(tile_e, tile_e[jnp.maximum(n_used - 1, 0)])
    rows = n_tiles * tm
    pos_blocks = pos.reshape(n // td, 1, 2 * td)

    xs = pl.pallas_call(
        _dispatch_kernel,
        grid=(n // td,),
        in_specs=[pl.BlockSpec((1, 1, 2 * td), lambda i: (i, 0, 0), memory_space=pltpu.SMEM),
                  pl.BlockSpec((td, d), row),
                  pl.BlockSpec(memory_space=pl.ANY)],
        out_specs=pl.BlockSpec(memory_space=pl.ANY),
        out_shape=jax.ShapeDtypeStruct((rows, d), F32),
        scratch_shapes=[pltpu.SemaphoreType.DMA(())],
        input_output_aliases={2: 0},
        compiler_params=_cp(1),
        name="moe_dispatch",
    )(pos_blocks, h, jnp.zeros((rows, d), F32))

    nj = f // tf

    def x_map(i, j, te, nu):
        return (jnp.minimum(i, nu[0] - 1), 0)

    def w13_map(i, j, te, nu):
        return (layer, te[i], 0, jnp.where(i < nu[0], j, nj - 1))

    def w2_map(i, j, te, nu):
        return (layer, te[i], jnp.where(i < nu[0], j, nj - 1), 0)

    ys = pl.pallas_call(
        functools.partial(_experts_kernel, fc=fc),
        grid_spec=pltpu.PrefetchScalarGridSpec(
            num_scalar_prefetch=2,
            grid=(n_tiles, nj),
            in_specs=[pl.BlockSpec((tm, d), x_map),
                      pl.BlockSpec((None, None, d, tf), w13_map),
                      pl.BlockSpec((None, None, d, tf), w13_map),
                      pl.BlockSpec((None, None, tf, d), w2_map)],
            out_specs=pl.BlockSpec((tm, d), lambda i, j, te, nu: (i, 0)),
            scratch_shapes=[pltpu.VMEM((tm, d), BF16)],
        ),
        out_shape=jax.ShapeDtypeStruct((rows, d), F32),
        compiler_params=_cp(2),
        name="moe_experts",
    )(tile_e, n_used.reshape(1), xs, w1, w3, w2)

    tc = td
    last = n // tc - 1
    return pl.pallas_call(
        functools.partial(_combine_kernel, alpha=alpha),
        grid=(n // tc,),
        in_specs=[pl.BlockSpec((1, 1, 2 * tc), lambda i: (i, 0, 0), memory_space=pltpu.SMEM),
                  pl.BlockSpec((1, 1, 2 * tc), lambda i: (jnp.minimum(i + 1, last), 0, 0), memory_space=pltpu.SMEM),
                  pl.BlockSpec((tc, LANE), row),
                  pl.BlockSpec((tc, d), row),
                  _resident((1, d)), _resident((1, d)),
                  pl.BlockSpec(memory_space=pl.ANY)],
        out_specs=[pl.BlockSpec((tc, d), row), pl.BlockSpec((tc, d), row)],
        out_shape=[jax.ShapeDtypeStruct((n, d), F32), jax.ShapeDtypeStruct((n, d), BF16)],
        scratch_shapes=[pltpu.VMEM((2, 2, tc, d), F32), pltpu.SemaphoreType.DMA((2,))],
        compiler_params=_cp(1),
        name="moe_combine",
    )(pos_blocks, pos_blocks, info, h, g, b, ys)


def _diff_mixer(hb, wq, wk, wv, lq1, lk1, lq2, lk2, sub_g, *, batch, seq, lambda_init):
    d = wq.shape[0]
    heads = d // LANE
    qkv = _linear(hb, (wq.astype(BF16), wk.astype(BF16), wv.astype(BF16)), name="diff_qkv")
    extra = tuple(a.reshape(1, -1).astype(F32) for a in (lq1, lk1, lq2, lk2, sub_g))
    return _flash("diff", qkv, qkv, qkv, batch=batch, seq=seq, groups=heads, q_blk0=0, k_blk0=heads,
                  v_blk0=2 * heads, extra=extra, lambda_init=lambda_init)


def _band_mixer(h, hb, w_qkv, rel_bias, wo, g, b, wr, *, batch, seq, alpha):
    d = w_qkv.shape[0]
    qkv = _linear(hb, (w_qkv.astype(BF16),), name="band_qkv")
    o = _band_attention(qkv, _band_bias_table(rel_bias), batch=batch, seq=seq, pairs=d // LANE)
    return _proj_res_ln_route(o, wo.astype(BF16), h, g, b, wr, alpha=alpha, name="band_out")


def _mla_mixer(hb, w_dq, q_g, w_uq, w_dkv, kv_g, w_ukv, *, batch, seq, tm=512):
    n, d = hb.shape
    q_rank = w_dq.shape[1]
    kv_rank = kv_g.shape[-1]
    heads = w_ukv.shape[1] // (2 * CHUNK)
    nope, rope = 64, 2 * ROPE_HALF
    tm = min(tm, seq)
    row = lambda i: (i, 0)
    tab = lambda i: (i % (seq // tm), 0)

    inv_freq = ROPE_THETA ** (-jnp.arange(ROPE_HALF, dtype=F32) / ROPE_HALF)
    ang = jnp.arange(seq).astype(F32)[:, None] * inv_freq[None, :]
    cos, sin = jnp.cos(ang), jnp.sin(ang)
    z = lambda w: jnp.zeros((seq, w), F32)
    ct_q = jnp.concatenate([jnp.ones((seq, nope), F32), cos, cos, z(LANE - nope - rope)], axis=1)
    st_q = jnp.concatenate([z(nope), sin, sin, z(LANE - nope - rope)], axis=1)
    ct_k = jnp.concatenate([cos, cos, z(LANE - rope)], axis=1)
    st_k = jnp.concatenate([sin, sin, z(LANE - rope)], axis=1)

    wq3 = w_uq.reshape(q_rank, heads, nope + rope)
    qn, q1, q2 = wq3[..., :nope], wq3[..., nope:nope + ROPE_HALF], wq3[..., nope + ROPE_HALF:]
    zq = lambda w: jnp.zeros((q_rank, heads, w), w_uq.dtype)
    wa = jnp.concatenate([qn, q1, q2, zq(LANE - nope - rope)], axis=2).reshape(q_rank, heads * LANE)
    wb = jnp.concatenate([zq(nope), -q2, q1, zq(LANE - nope - rope)], axis=2).reshape(q_rank, heads * LANE)

    k1, k2 = w_dkv[:, kv_rank:kv_rank + ROPE_HALF], w_dkv[:, kv_rank + ROPE_HALF:]
    zd = lambda w: jnp.zeros((d, w), w_dkv.dtype)
    wd = jnp.concatenate([w_dkv[:, :kv_rank], k1, k2, zd(LANE - rope), -k2, k1, zd(LANE - rope)], axis=1)

    wkv3 = w_ukv.reshape(kv_rank, heads, nope + CHUNK)
    zk = lambda r, w: jnp.zeros((r, heads, w), w_ukv.dtype)
    k_top = jnp.concatenate([wkv3[..., :nope], zk(kv_rank, LANE - nope)], axis=2).reshape(kv_rank, heads * LANE)
    eye = jnp.broadcast_to(jnp.eye(rope, dtype=w_ukv.dtype)[:, None, :], (rope, heads, rope))
    k_rope = jnp.concatenate([zk(rope, nope), eye, zk(rope, LANE - nope - rope)], axis=2).reshape(rope, heads * LANE)
    v_top = wkv3[..., nope:].reshape(kv_rank, heads * CHUNK)
    wkv = jnp.concatenate([
        jnp.concatenate([k_top, v_top], axis=1),
        jnp.concatenate([k_rope, jnp.zeros((rope, heads * CHUNK), w_ukv.dtype)], axis=1),
        jnp.zeros((LANE - rope, heads * (LANE + CHUNK)), w_ukv.dtype)], axis=0)

    cq = pl.pallas_call(
        _mla_cq_kernel,
        grid=(n // tm,),
        in_specs=[pl.BlockSpec((tm, d), row), _resident((d, q_rank)), _resident((1, q_rank))],
        out_specs=pl.BlockSpec((tm, q_rank), row),
        out_shape=jax.ShapeDtypeStruct((n, q_rank), BF16),
        compiler_params=_cp(1),
        name="mla_cq",
    )(hb, w_dq.astype(BF16), q_g.reshape(1, -1))

    q = pl.pallas_call(
        functools.partial(_mla_q_kernel, scale=float((nope + rope) ** -0.5), heads=heads),
        grid=(n // tm,),
        in_specs=[pl.BlockSpec((tm, q_rank), row), _resident((q_rank, heads * LANE)),
                  _resident((q_rank, heads * LANE)), pl.BlockSpec((tm, LANE), tab), pl.BlockSpec((tm, LANE), tab)],
        out_specs=pl.BlockSpec((tm, heads * LANE), row),
        out_shape=jax.ShapeDtypeStruct((n, heads * LANE), BF16),
        compiler_params=_cp(1),
        name="mla_q",
    )(cq, wa.astype(BF16), wb.astype(BF16), ct_q, st_q)

    ckr = pl.pallas_call(
        functools.partial(_mla_ckv_kernel, rank=kv_rank),
        grid=(n // tm,),
        in_specs=[pl.BlockSpec((tm, d), row), _resident((d, kv_rank + 2 * LANE)), _resident((1, kv_rank)),
                  pl.BlockSpec((tm, LANE), tab), pl.BlockSpec((tm, LANE), tab)],
        out_specs=pl.BlockSpec((tm, kv_rank + LANE), row),
        out_shape=jax.ShapeDtypeStruct((n, kv_rank + LANE), BF16),
        compiler_params=_cp(1),
        name="mla_ckv",
    )(hb, wd.astype(BF16), kv_g.reshape(1, -1), ct_k, st_k)

    kv = _linear(ckr, (wkv.astype(BF16),), name="mla_kv")
    return _flash("mla", q, kv, kv, batch=batch, seq=seq, groups=heads // 2, q_blk0=0, k_blk0=0, v_blk0=heads)


def _sg_mixer(h, hb, w_in, vg, vb, w_s, b_s, w_out, g, b, wr, *, alpha):
    n, d = hb.shape
    width = w_out.shape[0]
    groups = w_s.shape[0]
    tm = min(ROUTE_TILE, n)
    row = lambda i: (i, 0)
    r_specs, r_shapes, r_scratch = _route_outputs(n, tm)
    u, v = pl.pallas_call(
        functools.partial(_sg_in_kernel, nc=min(512, width)),
        grid=(n // tm,),
        in_specs=[pl.BlockSpec((tm, d), row), _resident((d, 2 * width)), _resident((1, width)),
                  _resident((1, width))],
        out_specs=[pl.BlockSpec((tm, width), row), pl.BlockSpec((tm, width), row)],
        out_shape=[jax.ShapeDtypeStruct((n, width), BF16), jax.ShapeDtypeStruct((n, width), BF16)],
        scratch_shapes=[pltpu.VMEM((tm, width), F32)],
        compiler_params=_cp(1),
        name="sg_in",
    )(hb, w_in.astype(BF16), vg.reshape(1, -1), vb.reshape(1, -1))
    bs_full = jnp.repeat(b_s.T.astype(F32), width // groups, axis=1)
    return pl.pallas_call(
        functools.partial(_sg_mix_kernel, alpha=alpha, groups=groups),
        grid=(n // tm,),
        in_specs=[pl.BlockSpec((tm, width), row), pl.BlockSpec((tm, width), row),
                  _resident((groups, GMLP_CHUNK, GMLP_CHUNK)), _resident((GMLP_CHUNK, width)),
                  _resident((width, d)), pl.BlockSpec((tm, d), row), _resident((1, d)), _resident((1, d)),
                  _resident((d, LANE))],
        out_specs=[pl.BlockSpec((tm, d), row)] + r_specs,
        out_shape=[jax.ShapeDtypeStruct((n, d), F32)] + r_shapes,
        scratch_shapes=[pltpu.VMEM((tm, width), BF16)] + r_scratch,
        compiler_params=_cp(1),
        name="sg_mix",
    )(u, v, w_s.astype(F32), bs_full, w_out.astype(BF16), h, g, b, wr)


def kernel(x, ln_mix_g, ln_mix_b, ln_ffn_g, ln_ffn_b, diff_wq, diff_wk, diff_wv, diff_lq1, diff_lk1, diff_lq2, diff_lk2, diff_sub_g, diff_wo, ca_w_qkv, ca_rel_bias, ca_wo, mla_w_dq, mla_q_norm_g, mla_w_uq, mla_w_dkv, mla_kv_norm_g, mla_w_ukv, mla_wo, sg_w_in, sg_v_norm_g, sg_v_norm_b, sg_w_s, sg_b_s, sg_w_out, ffn_w1, ffn_w3, ffn_w2, moe_w_router, moe_w1, moe_w3, moe_w2):
    batch, seq, d = x.shape
    depth = ln_mix_g.shape[0]
    alpha = float((2 * depth) ** 0.25)
    h = x.reshape(batch * seq, d).astype(F32)
    hb = h
    vec = lambda a: a.reshape(1, -1).astype(F32)
    for i in range(depth):
        kind, j = i % 4, i // 4
        mg, mb = vec(ln_mix_g[i]), vec(ln_mix_b[i])
        fg, fb = vec(ln_ffn_g[i]), vec(ln_ffn_b[i])
        if kind in (0, 2):
            if kind == 0:
                a = _diff_mixer(hb, diff_wq[j], diff_wk[j], diff_wv[j], diff_lq1[j], diff_lk1[j], diff_lq2[j],
                                diff_lk2[j], diff_sub_g[j], batch=batch, seq=seq,
                                lambda_init=0.8 - 0.6 * math.exp(-0.3 * i))
                wo = diff_wo[j]
            else:
                a = _mla_mixer(hb, mla_w_dq[j], mla_q_norm_g[j], mla_w_uq[j], mla_w_dkv[j], mla_kv_norm_g[j],
                               mla_w_ukv[j], batch=batch, seq=seq)
                wo = mla_wo[j]
            h, hb = _mix_ffn(a, wo.astype(BF16), mg, mb, ffn_w1[i // 2].astype(BF16), ffn_w3[i // 2].astype(BF16),
                             ffn_w2[i // 2].astype(BF16), h, fg, fb, alpha=alpha)
        else:
            wr = _router_weights(moe_w_router[i // 2])
            if kind == 1:
                h, info, cnt = _band_mixer(h, hb, ca_w_qkv[j], ca_rel_bias[j], ca_wo[j], mg, mb, wr, batch=batch,
                                           seq=seq, alpha=alpha)
            else:
                h, info, cnt = _sg_mixer(h, hb, sg_w_in[j], sg_v_norm_g[j], sg_v_norm_b[j], sg_w_s[j], sg_b_s[j],
                                         sg_w_out[j], mg, mb, wr, alpha=alpha)
            h, hb = _moe(h, info, cnt, moe_w1, moe_w3, moe_w2, fg, fb, layer=i // 2, alpha=alpha)
    return h.reshape(batch, seq, d).astype(x.dtype)
```

```python
import functools
import math

import jax
import jax.numpy as jnp
import numpy as np
from jax import lax
from jax.experimental import pallas as pl
from jax.experimental.pallas import tpu as pltpu

F32 = jnp.float32
BF16 = jnp.bfloat16
I32 = jnp.int32

LANE = 128
VMEM_LIMIT = 56 * 1024 * 1024

CHUNK = 64
N_EXPERTS = 8
LN_EPS = 1e-5
RMS_EPS = 1e-6
NEG_INF = -1e30
ROPE_THETA = 10000.0
CA_LEFT = 8
CA_REL_CLIP = 128
ROPE_HALF = 16


def _cp(n_axes, vmem=VMEM_LIMIT):
    return pltpu.CompilerParams(dimension_semantics=("arbitrary",) * n_axes, vmem_limit_bytes=vmem)


def _resident(shape):
    nd = len(shape)
    return pl.BlockSpec(shape, lambda *_: (0,) * nd, pipeline_mode=pl.Buffered(1))


def _res_ln(h, m, g, b, alpha):
    z = alpha * h + m
    mu = jnp.mean(z, axis=-1, keepdims=True)
    zc = z - mu
    var = jnp.mean(zc * zc, axis=-1, keepdims=True)
    return zc * lax.rsqrt(var + LN_EPS) * g + b


def _rms(x, g, eps):
    ms = jnp.mean(x * x, axis=-1, keepdims=True)
    return x * lax.rsqrt(ms + eps) * g


def _linear_kernel(x_ref, *refs):
    *w_refs, o_ref = refs
    x = x_ref[...].astype(BF16)
    col = 0
    for w_ref in w_refs:
        n = w_ref.shape[1]
        o_ref[:, col:col + n] = jnp.dot(x, w_ref[...], preferred_element_type=F32).astype(o_ref.dtype)
        col += n


def _linear(x, ws, *, tm=512, out_dtype=BF16, name="linear"):
    m, k = x.shape
    n = sum(w.shape[1] for w in ws)
    tm = min(tm, m)
    return pl.pallas_call(
        _linear_kernel,
        grid=(m // tm,),
        in_specs=[pl.BlockSpec((tm, k), lambda i: (i, 0))] + [_resident(w.shape) for w in ws],
        out_specs=pl.BlockSpec((tm, n), lambda i: (i, 0)),
        out_shape=jax.ShapeDtypeStruct((m, n), out_dtype),
        compiler_params=_cp(1),
        name=name,
    )(x, *ws)


ROUTE_TILE = 512


def _route_outputs(m, tm):
    specs = [pl.BlockSpec((tm, LANE), lambda i: (i, 0)), pl.BlockSpec((1, LANE), lambda i: (0, 0))]
    shapes = [jax.ShapeDtypeStruct((m, LANE), F32), jax.ShapeDtypeStruct((1, LANE), F32)]
    scratch = [pltpu.VMEM((1, LANE), F32), pltpu.VMEM((tm, tm), BF16)]
    return specs, shapes, scratch


def _proj_route_kernel(a_ref, w_ref, h_ref, g_ref, b_ref, wr_ref, oh_ref, info_ref, cnt_ref, run_ref, before_ref, *,
                       alpha):
    pl.when(pl.program_id(0) == 0)(lambda: _route_init(run_ref, before_ref))
    m = jnp.dot(a_ref[...], w_ref[...], preferred_element_type=F32)
    y = _res_ln(h_ref[...], m, g_ref[...], b_ref[...], alpha)
    oh_ref[...] = y
    _route(y, wr_ref, run_ref, before_ref, info_ref, cnt_ref)


def _proj_res_ln_route(a, w, h, g, b, wr, *, alpha, name):
    m, k = a.shape
    d = w.shape[1]
    tm = min(ROUTE_TILE, m)
    row = lambda i: (i, 0)
    r_specs, r_shapes, r_scratch = _route_outputs(m, tm)
    return pl.pallas_call(
        functools.partial(_proj_route_kernel, alpha=alpha),
        grid=(m // tm,),
        in_specs=[pl.BlockSpec((tm, k), row), _resident((k, d)), pl.BlockSpec((tm, d), row),
                  _resident((1, d)), _resident((1, d)), _resident((d, LANE))],
        out_specs=[pl.BlockSpec((tm, d), row)] + r_specs,
        out_shape=[jax.ShapeDtypeStruct((m, d), F32)] + r_shapes,
        scratch_shapes=r_scratch,
        compiler_params=_cp(1),
        name=name,
    )(a, w, h, g, b, wr)


def _mix_ffn_kernel(a_ref, wo_ref, mg_ref, mb_ref, w1_ref, w3_ref, w2_ref, h_ref, g_ref, b_ref, oh_ref, ohb_ref,
                    hm_ref, xb_ref, acc_ref, *, alpha, fc):
    mix = jnp.dot(a_ref[...], wo_ref[...], preferred_element_type=F32)
    hm = _res_ln(h_ref[...], mix, mg_ref[...], mb_ref[...], alpha)
    hm_ref[...] = hm
    xb_ref[...] = hm.astype(BF16)
    x = xb_ref[...]
    f = w1_ref.shape[1]
    for c in range(f // fc):
        sl = slice(c * fc, (c + 1) * fc)
        a = jnp.dot(x, w1_ref[:, sl], preferred_element_type=F32)
        b3 = jnp.dot(x, w3_ref[:, sl], preferred_element_type=F32)
        gated = (jax.nn.silu(a) * b3).astype(BF16)
        part = jnp.dot(gated, w2_ref[sl, :], preferred_element_type=F32)
        if c == 0:
            acc_ref[...] = part
        else:
            acc_ref[...] += part
    y = _res_ln(hm_ref[...], acc_ref[...], g_ref[...], b_ref[...], alpha)
    oh_ref[...] = y
    ohb_ref[...] = y.astype(BF16)


def _mix_ffn(a, wo, mg, mb, w1, w3, w2, h, g, b, *, alpha, tm=512, fc=256):
    m, d = h.shape
    k = a.shape[1]
    f = w1.shape[1]
    tm = min(tm, m)
    row = lambda i: (i, 0)
    vecs = [_resident((1, d)), _resident((1, d))]
    return pl.pallas_call(
        functools.partial(_mix_ffn_kernel, alpha=alpha, fc=fc),
        grid=(m // tm,),
        in_specs=[pl.BlockSpec((tm, k), row), _resident((k, d))] + vecs
                 + [_resident((d, f)), _resident((d, f)), _resident((f, d)), pl.BlockSpec((tm, d), row)] + vecs,
        out_specs=[pl.BlockSpec((tm, d), row), pl.BlockSpec((tm, d), row)],
        out_shape=[jax.ShapeDtypeStruct((m, d), F32), jax.ShapeDtypeStruct((m, d), BF16)],
        scratch_shapes=[pltpu.VMEM((tm, d), F32), pltpu.VMEM((tm, d), BF16), pltpu.VMEM((tm, d), F32)],
        compiler_params=_cp(1),
        name="mix_ffn",
    )(a, wo, mg, mb, w1, w3, w2, h, g, b)


ONES_ROWS = 16
ACC_ROWS = LANE + ONES_ROWS


def _vt_ext(v_blk):
    vt = v_blk.astype(F32).T.astype(BF16)
    return jnp.concatenate([vt, jnp.ones((ONES_ROWS, v_blk.shape[0]), BF16)], axis=0)


def _split_heads(x):
    low = lax.broadcasted_iota(I32, x.shape, 1) < CHUNK
    zero = jnp.zeros_like(x)
    return jnp.where(low, x, zero), jnp.where(low, zero, x)


_NT = (((1,), (1,)), ((), ()))


def _flash_kernel(*refs, mode, tq, gs, lambda_init):
    if mode == "diff":
        lq1_ref, lk1_ref, lq2_ref, lk2_ref, subg_ref, q_ref, k_ref, v_ref, o_ref, *scratch = refs
    else:
        q_ref, k_ref, v_ref, o_ref, *scratch = refs
    vt_ref, m_ref, a_ref, bm_ref, acc_ref, s_ref, p_ref = scratch
    qi = pl.program_id(2)
    tk = tq // 2
    nmap = 2 * gs
    lanes = lambda i: slice(i * LANE, (i + 1) * LANE)

    @pl.when(qi == 0)
    def _():
        for j in range(vt_ref.shape[0]):
            for g in range(gs):
                vt_ref[j, g] = _vt_ext(v_ref[j * tk:(j + 1) * tk, lanes(g)])

    qs = []
    for g in range(gs):
        if mode == "diff":
            qs += list(_split_heads(q_ref[:, lanes(g)] * jnp.asarray(CHUNK ** -0.5, BF16)))
        else:
            qs += [q_ref[:, lanes(2 * g)], q_ref[:, lanes(2 * g + 1)]]
    m_ref[...] = jnp.full(m_ref.shape, NEG_INF, F32)
    acc_ref[...] = jnp.zeros(acc_ref.shape, F32)

    def scores(blk):
        start = pl.multiple_of(blk * tk, tk)
        kb = k_ref[pl.ds(start, tk), :]
        kks = [kb[:, lanes(mp // 2 if mode == "diff" else mp)] for mp in range(nmap)]
        return [lax.dot_general(kks[mp], qs[mp], _NT, preferred_element_type=F32) for mp in range(nmap)]

    def values(blk, slot):
        return [jnp.dot(vt_ref[blk, mp // 2], p_ref[slot, mp], preferred_element_type=F32) for mp in range(nmap)]

    def stash(slot, ss):
        for mp in range(nmap):
            s_ref[slot, mp] = ss[mp]
            bm_ref[slot, mp] = jnp.max(ss[mp], axis=0, keepdims=True)

    def softmax(slot, masked):
        for mp in range(nmap):
            s = s_ref[slot, mp]
            if masked:
                kc = (lax.broadcasted_iota(I32, s.shape, 0) + slot * tk) >> 6
                qc = lax.broadcasted_iota(I32, s.shape, 1) >> 6
                s = jnp.where(kc <= qc, s, NEG_INF)
                blk_max = jnp.max(s, axis=0, keepdims=True)
            else:
                blk_max = bm_ref[slot, mp]
            m_old = m_ref[mp]
            m_new = jnp.maximum(m_old, blk_max)
            p_ref[slot, mp] = jnp.exp(s - m_new).astype(BF16)
            a_ref[slot, mp] = jnp.exp(m_old - m_new)
            m_ref[mp] = m_new

    def fold(pv, scale):
        for mp in range(nmap):
            acc_ref[mp] = scale[mp] * acc_ref[mp] + pv[mp]

    def trip(blk, masked, produce):
        scale = [[a_ref[u, mp] for mp in range(nmap)] for u in range(2)]
        pvs = [values(jnp.maximum(blk - 2 + u, 0), u) for u in range(2)]
        nxt = [scores(blk + 2 + u) for u in range(2)] if produce else []
        for u in range(2):
            softmax(u, masked)
        for u in range(2):
            fold(pvs[u], scale[u])
        for u, s in enumerate(nxt):
            stash(u, s)

    for u in range(2):
        stash(u, scores(u))
    p_ref[...] = jnp.zeros(p_ref.shape, BF16)
    a_ref[...] = jnp.ones(a_ref.shape, F32)

    def step(j, carry):
        trip(2 * j, False, True)
        return carry

    lax.fori_loop(0, qi, step, 0)
    trip(2 * qi, True, False)
    for u in range(2):
        fold(values(2 * qi + u, u), [a_ref[u, mp] for mp in range(nmap)])

    for g in range(gs):
        acc_a, acc_b = acc_ref[2 * g], acc_ref[2 * g + 1]
        oa = acc_a[:LANE] / acc_a[LANE:LANE + 1]
        ob = acc_b[:LANE] / acc_b[LANE:LANE + 1]
        if mode == "diff":
            lam = (jnp.exp(jnp.sum(lq1_ref[...] * lk1_ref[...], axis=-1, keepdims=True))
                   - jnp.exp(jnp.sum(lq2_ref[...] * lk2_ref[...], axis=-1, keepdims=True)) + lambda_init)
            o = (oa - lam * ob).T
            o = _rms(o, subg_ref[...], 1e-5) * (1.0 - lambda_init)
        else:
            row = lax.broadcasted_iota(I32, oa.shape, 0)
            o = jnp.where(row < CHUNK, oa, ob).T
        o_ref[:, lanes(g)] = o.astype(o_ref.dtype)


def _flash(mode, q_arr, k_arr, v_arr, *, batch, seq, groups, q_blk0, k_blk0, v_blk0, extra=(), lambda_init=0.0,
           tq=512, gs=2):
    qw = (LANE if mode == "diff" else 2 * LANE) * gs
    tq = min(tq, seq)
    tk = tq // 2
    nq = seq // tq
    n = batch * seq
    nmap = 2 * gs
    assert groups % gs == 0 and q_blk0 % gs == 0 and k_blk0 % gs == 0 and v_blk0 % gs == 0
    qb, kb, vb = q_blk0 // gs, k_blk0 // gs, v_blk0 // gs
    in_specs = [pl.BlockSpec(e.shape, lambda b, g, i: (0, 0)) for e in extra]
    in_specs += [
        pl.BlockSpec((tq, qw), lambda b, g, i: (b * nq + i, qb + g)),
        pl.BlockSpec((seq, qw), lambda b, g, i: (b, kb + g)),
        pl.BlockSpec((seq, gs * LANE), lambda b, g, i: (b, vb + g)),
    ]
    return pl.pallas_call(
        functools.partial(_flash_kernel, mode=mode, tq=tq, gs=gs, lambda_init=lambda_init),
        grid=(batch, groups // gs, nq),
        in_specs=in_specs,
        out_specs=pl.BlockSpec((tq, gs * LANE), lambda b, g, i: (b * nq + i, g)),
        out_shape=jax.ShapeDtypeStruct((n, groups * LANE), BF16),
        scratch_shapes=[pltpu.VMEM((seq // tk, gs, ACC_ROWS, tk), BF16), pltpu.VMEM((nmap, 1, tq), F32),
                        pltpu.VMEM((2, nmap, 1, tq), F32), pltpu.VMEM((2, nmap, 1, tq), F32),
                        pltpu.VMEM((nmap, ACC_ROWS, tq), F32),
                        pltpu.VMEM((2, nmap, tk, tq), F32), pltpu.VMEM((2, nmap, tk, tq), BF16)],
        compiler_params=_cp(3),
        name="flash_" + mode,
    )(*extra, q_arr, k_arr, v_arr)


BAND_TQ = 4 * CHUNK
BAND_W = (CA_LEFT + 4) * CHUNK
BAND_PAD = CA_LEFT * CHUNK
BAND_NW = BAND_W // BAND_TQ
BAND_EXT = 1024


def _band_kernel(q_ref, qn_ref, k_ref, v_ref, ext_ref, o_ref, kpad_ref, vt_ref, s_ref, bm_ref, bias_ref, *, seq,
                 gp):
    t = pl.program_id(2)
    nt = pl.num_programs(2)
    npad = BAND_PAD // BAND_TQ

    nh = 2 * gp
    lanes = lambda i: slice(i * LANE, (i + 1) * LANE)

    def produce(qr, tile, slot, front):
        qhs = []
        for g in range(gp):
            qhs += list(_split_heads(qr[:, lanes(g)] * jnp.asarray(CHUNK ** -0.5, BF16)))
        tops = [None] * nh
        for u in range(BAND_NW):
            start = pl.multiple_of((tile + u) * BAND_TQ, BAND_TQ)
            kb = kpad_ref[pl.ds(start, BAND_TQ), :]
            for hh in range(nh):
                s = lax.dot_general(kb[:, lanes(hh // 2)], qhs[hh], _NT, preferred_element_type=F32)
                s = s + bias_ref[hh, u * BAND_TQ:(u + 1) * BAND_TQ, :]
                if front:
                    invalid = lax.broadcasted_iota(I32, s.shape, 0) + (tile + u) * BAND_TQ < BAND_PAD
                    s = jnp.where(invalid, NEG_INF, s)
                s_ref[slot, hh, u] = s
                top = jnp.max(s, axis=0, keepdims=True)
                tops[hh] = top if tops[hh] is None else jnp.maximum(tops[hh], top)
        for hh in range(nh):
            bm_ref[slot, hh] = tops[hh]

    @pl.when((t == 0) & (pl.program_id(1) == 0))
    def _():
        ci = lax.broadcasted_iota(I32, (BAND_W, BAND_TQ), 1) >> 6
        cj = lax.broadcasted_iota(I32, (BAND_W, BAND_TQ), 0) >> 6
        vis = (cj >= ci) & (cj <= ci + CA_LEFT)
        for hh in range(nh):
            rows = jnp.broadcast_to(ext_ref[hh // 2, hh % 2], (BAND_W, BAND_EXT))
            skew = pltpu.roll(rows, BAND_EXT - (BAND_W - 1), 1, stride=1, stride_axis=0)
            bias_ref[hh] = jnp.where(vis, skew[:, :BAND_TQ], NEG_INF)

    @pl.when(t == 0)
    def _():
        kpad_ref[0:BAND_PAD, :] = jnp.zeros((BAND_PAD, gp * LANE), BF16)
        kpad_ref[BAND_PAD:BAND_PAD + seq, :] = k_ref[...]
        for j in range(npad):
            vt_ref[j] = jnp.zeros((gp, ACC_ROWS, BAND_TQ), BF16)
        for j in range(seq // BAND_TQ):
            for g in range(gp):
                vt_ref[npad + j, g] = _vt_ext(v_ref[j * BAND_TQ:(j + 1) * BAND_TQ, lanes(g)])
        produce(q_ref, 0, 0, True)

    cur = t & 1
    nxt_tile = jnp.minimum(t + 1, nt - 1)

    def consume():
        outs = []
        for hh in range(nh):
            m = bm_ref[cur, hh]
            acc = None
            for u in range(BAND_NW):
                p = jnp.exp(s_ref[cur, hh, u] - m).astype(BF16)
                pv = jnp.dot(vt_ref[t + u, hh // 2], p, preferred_element_type=F32)
                acc = pv if acc is None else acc + pv
            outs.append(acc[:LANE] / acc[LANE:LANE + 1])
        row = lax.broadcasted_iota(I32, outs[0].shape, 0)
        for g in range(gp):
            o_ref[:, lanes(g)] = jnp.where(row < CHUNK, outs[2 * g], outs[2 * g + 1]).T.astype(o_ref.dtype)

    def run(front):
        consume()
        produce(qn_ref, nxt_tile, 1 - cur, front)

    pl.when(t + 1 < npad)(lambda: run(True))
    pl.when(t + 1 >= npad)(lambda: run(False))


def _band_attention(qkv, ext, *, batch, seq, pairs, gp=2):
    n = batch * seq
    nt = seq // BAND_TQ
    assert pairs % gp == 0
    steps, w, nh = pairs // gp, gp * LANE, 2 * gp
    return pl.pallas_call(
        functools.partial(_band_kernel, seq=seq, gp=gp),
        grid=(steps, batch, nt),
        in_specs=[
            pl.BlockSpec((BAND_TQ, w), lambda g, b, t: (b * nt + t, g)),
            pl.BlockSpec((BAND_TQ, w), lambda g, b, t: (b * nt + jnp.minimum(t + 1, nt - 1), g)),
            pl.BlockSpec((seq, w), lambda g, b, t: (b, steps + g)),
            pl.BlockSpec((seq, w), lambda g, b, t: (b, 2 * steps + g)),
            pl.BlockSpec((gp, 2, 1, BAND_EXT), lambda g, b, t: (g, 0, 0, 0)),
        ],
        out_specs=pl.BlockSpec((BAND_TQ, w), lambda g, b, t: (b * nt + t, g)),
        out_shape=jax.ShapeDtypeStruct((n, pairs * LANE), BF16),
        scratch_shapes=[pltpu.VMEM((seq + BAND_PAD, w), BF16),
                        pltpu.VMEM(((seq + BAND_PAD) // BAND_TQ, gp, ACC_ROWS, BAND_TQ), BF16),
                        pltpu.VMEM((2, nh, BAND_NW, BAND_TQ, BAND_TQ), F32),
                        pltpu.VMEM((2, nh, 1, BAND_TQ), F32),
                        pltpu.VMEM((nh, BAND_W, BAND_TQ), F32)],
        compiler_params=_cp(3),
        name="band_attention",
    )(qkv, qkv, qkv, qkv, ext)


def _band_bias_table(rel_bias):
    heads = rel_bias.shape[0]
    rel = np.minimum(np.arange(BAND_EXT), BAND_TQ + BAND_W - 2) - (BAND_W - 1) + BAND_PAD
    ext = rel_bias.astype(F32)[:, np.clip(rel, -CA_REL_CLIP, CA_REL_CLIP) + CA_REL_CLIP]
    return ext.reshape(heads // 2, 2, 1, BAND_EXT)


def _mla_cq_kernel(x_ref, w_ref, g_ref, o_ref):
    c = jnp.dot(x_ref[...], w_ref[...], preferred_element_type=F32)
    o_ref[...] = _rms(c, g_ref[...], RMS_EPS).astype(o_ref.dtype)


def _mla_q_kernel(c_ref, wa_ref, wb_ref, ct_ref, st_ref, o_ref, *, scale, heads):
    c = c_ref[...]
    a = jnp.dot(c, wa_ref[...], preferred_element_type=F32)
    b = jnp.dot(c, wb_ref[...], preferred_element_type=F32)
    ct = ct_ref[...] * scale
    st = st_ref[...] * scale
    for h in range(heads):
        sl = slice(h * LANE, (h + 1) * LANE)
        o_ref[:, sl] = (a[:, sl] * ct + b[:, sl] * st).astype(o_ref.dtype)


def _mla_ckv_kernel(x_ref, w_ref, g_ref, ct_ref, st_ref, o_ref, *, rank):
    y = jnp.dot(x_ref[...], w_ref[...], preferred_element_type=F32)
    o_ref[:, :rank] = _rms(y[:, :rank], g_ref[...], RMS_EPS).astype(o_ref.dtype)
    kr = y[:, rank:rank + LANE] * ct_ref[...] + y[:, rank + LANE:rank + 2 * LANE] * st_ref[...]
    o_ref[:, rank:] = kr.astype(o_ref.dtype)


GMLP_CHUNK = 128


def _sg_in_kernel(x_ref, w_ref, g_ref, b_ref, u_ref, v_ref, vbuf_ref, *, nc):
    width = u_ref.shape[1]
    x = x_ref[...]
    for c in range(0, 2 * width, nc):
        hc = jax.nn.gelu(jnp.dot(x, w_ref[:, c:c + nc], preferred_element_type=F32))
        if c < width:
            u_ref[:, c:c + nc] = hc.astype(u_ref.dtype)
        else:
            vbuf_ref[:, c - width:c - width + nc] = hc
    v = vbuf_ref[...]
    mu = jnp.mean(v, axis=-1, keepdims=True)
    vc = v - mu
    var = jnp.mean(vc * vc, axis=-1, keepdims=True)
    v_ref[...] = (vc * lax.rsqrt(var + LN_EPS) * g_ref[...] + b_ref[...]).astype(v_ref.dtype)


def _sg_mix_kernel(u_ref, v_ref, ws_ref, bs_ref, wo_ref, h_ref, g_ref, b_ref, wr_ref, oh_ref, info_ref, cnt_ref,
                   gated_ref, run_ref, before_ref, *, alpha, groups):
    tm = u_ref.shape[0]
    pl.when(pl.program_id(0) == 0)(lambda: _route_init(run_ref, before_ref))
    r = lax.broadcasted_iota(I32, (GMLP_CHUNK, GMLP_CHUNK), 0) >> 6
    c = lax.broadcasted_iota(I32, (GMLP_CHUNK, GMLP_CHUNK), 1) >> 6
    vis = c <= r
    for gi in range(groups):
        w = jnp.where(vis, ws_ref[gi], 0.0).astype(BF16)
        cs = slice(gi * LANE, (gi + 1) * LANE)
        for ch in range(tm // GMLP_CHUNK):
            rs = slice(ch * GMLP_CHUNK, (ch + 1) * GMLP_CHUNK)
            mixed = jnp.dot(w, v_ref[rs, cs], preferred_element_type=F32) + bs_ref[:, cs]
            gated_ref[rs, cs] = (u_ref[rs, cs].astype(F32) * mixed).astype(BF16)
    m = jnp.dot(gated_ref[...], wo_ref[...], preferred_element_type=F32)
    y = _res_ln(h_ref[...], m, g_ref[...], b_ref[...], alpha)
    oh_ref[...] = y
    _route(y, wr_ref, run_ref, before_ref, info_ref, cnt_ref)


INFO_IDX, INFO_RANK, INFO_GATE = 0, 2, 4
ROW_DMA_UNROLL = 8


def _route_init(run_ref, before_ref):
    tr = before_ref.shape[0]
    run_ref[...] = jnp.zeros_like(run_ref)
    rr = lax.broadcasted_iota(I32, (tr, tr), 0)
    cc = lax.broadcasted_iota(I32, (tr, tr), 1)
    before_ref[...] = jnp.where(cc < rr, 1.0, 0.0).astype(BF16)


def _route(x, w_ref, run_ref, before_ref, info_ref, cnt_ref):
    tr = x.shape[0]
    x1 = x.astype(BF16)
    r1 = x - x1.astype(F32)
    x2 = r1.astype(BF16)
    x3 = (r1 - x2.astype(F32)).astype(BF16)
    w = w_ref[...]
    lane = lax.broadcasted_iota(I32, (tr, LANE), 1)
    t = (jnp.dot(x1, w, preferred_element_type=F32)
         + jnp.where(lane < 2 * N_EXPERTS, jnp.dot(x2, w, preferred_element_type=F32), 0.0)
         + jnp.where(lane < N_EXPERTS, jnp.dot(x3, w, preferred_element_type=F32), 0.0))
    logits = t + pltpu.roll(t, LANE - N_EXPERTS, 1) + pltpu.roll(t, LANE - 2 * N_EXPERTS, 1)
    big = jnp.float32(3e38)
    logits = jnp.where(lane < N_EXPERTS, logits, -big)

    lane_f = lane.astype(F32)
    v1 = jnp.max(logits, axis=-1, keepdims=True)
    i1 = jnp.min(jnp.where(logits == v1, lane_f, float(LANE)), axis=-1, keepdims=True)
    rest = jnp.where(lane_f == i1, -big, logits)
    v2 = jnp.max(rest, axis=-1, keepdims=True)
    i2 = jnp.min(jnp.where(rest == v2, lane_f, float(LANE)), axis=-1, keepdims=True)
    e = jnp.exp(v2 - v1)
    g1 = 1.0 / (1.0 + e)
    g2 = e / (1.0 + e)

    oh1 = jnp.where(lane_f == i1, 1.0, 0.0)
    oh2 = jnp.where(lane_f == i2, 1.0, 0.0)
    oh = oh1 + oh2
    prior = jnp.dot(before_ref[...], oh.astype(BF16), preferred_element_type=F32) + run_ref[...]
    rank1 = jnp.sum(prior * oh1, axis=-1, keepdims=True)
    rank2 = jnp.sum(prior * oh2, axis=-1, keepdims=True)
    run_ref[...] += jnp.sum(oh, axis=0, keepdims=True)
    cnt_ref[...] = run_ref[...]

    info = jnp.where(lane == INFO_IDX, i1, 0.0)
    info = jnp.where(lane == INFO_IDX + 1, i2, info)
    info = jnp.where(lane == INFO_RANK, rank1, info)
    info = jnp.where(lane == INFO_RANK + 1, rank2, info)
    info = jnp.where(lane == INFO_GATE, g1, info)
    info = jnp.where(lane == INFO_GATE + 1, g2, info)
    info_ref[...] = info


def _dispatch_kernel(zoff_ref, pos_ref, h_ref, xs_ref, zero_ref, sem, zsem):
    td = h_ref.shape[0]

    @pl.when(pl.program_id(0) == 0)
    def _():
        tile = zero_ref.shape[0]
        zero_ref[...] = jnp.zeros_like(zero_ref)

        def fill(k):
            start = pl.multiple_of(jnp.maximum(zoff_ref[k], 0), tile)
            return pltpu.make_async_copy(zero_ref, xs_ref.at[pl.ds(start, tile), :], zsem)

        for k in range(zoff_ref.shape[0]):
            pl.when(zoff_ref[k] >= 0)(lambda k=k: fill(k).start())
        for k in range(zoff_ref.shape[0]):
            pl.when(zoff_ref[k] >= 0)(lambda k=k: fill(k).wait())

    def start(r, carry):
        for k in range(2):
            p = pos_ref[0, 0, 2 * r + k]
            pltpu.make_async_copy(h_ref.at[pl.ds(r, 1), :], xs_ref.at[pl.ds(p, 1), :], sem).start(priority=k)
        return carry

    lax.fori_loop(0, td, start, 0, unroll=ROW_DMA_UNROLL)
    for _ in range(2):
        pltpu.make_async_copy(h_ref, xs_ref.at[pl.ds(0, td), :], sem).wait()


def _experts_kernel(te_ref, nu_ref, x_ref, w1_ref, w3_ref, w2_ref, o_ref, xb_ref, *, fc):
    i = pl.program_id(0)
    j = pl.program_id(1)
    del te_ref

    @pl.when(j == 0)
    def _():
        o_ref[...] = jnp.zeros_like(o_ref)
        xb_ref[...] = x_ref[...].astype(BF16)

    @pl.when(i < nu_ref[0])
    def _():
        x = xb_ref[...]
        tf = w1_ref.shape[1]
        gated = []
        for c in range(tf // fc):
            sl = slice(c * fc, (c + 1) * fc)
            a = jnp.dot(x, w1_ref[:, sl].astype(BF16), preferred_element_type=F32)
            b3 = jnp.dot(x, w3_ref[:, sl].astype(BF16), preferred_element_type=F32)
            gated.append((jax.nn.silu(a) * b3).astype(BF16))
        o_ref[...] += jnp.dot(jnp.concatenate(gated, axis=1), w2_ref[...].astype(BF16),
                              preferred_element_type=F32)


def _combine_kernel(pos_ref, posn_ref, info_ref, h_ref, g_ref, b_ref, ys_ref, oh_ref, ohb_ref, buf_ref, sem, *,
                    alpha):
    i = pl.program_id(0)
    tc = h_ref.shape[0]

    def gather(p_ref, slot):
        def start(r, carry):
            for k in range(2):
                p = p_ref[0, 0, 2 * r + k]
                pltpu.make_async_copy(ys_ref.at[pl.ds(p, 1), :], buf_ref.at[slot, k, pl.ds(r, 1), :],
                                      sem.at[slot]).start(priority=k)
            return carry

        lax.fori_loop(0, tc, start, 0, unroll=ROW_DMA_UNROLL)

    @pl.when(i == 0)
    def _():
        gather(pos_ref, 0)

    @pl.when(i + 1 < pl.num_programs(0))
    def _():
        gather(posn_ref, (i + 1) & 1)

    cur = i & 1
    for k in range(2):
        pltpu.make_async_copy(ys_ref.at[pl.ds(0, tc), :], buf_ref.at[cur, k], sem.at[cur]).wait()
    info = info_ref[...]
    g1 = info[:, INFO_GATE:INFO_GATE + 1]
    g2 = info[:, INFO_GATE + 1:INFO_GATE + 2]
    m = g1 * buf_ref[cur, 0] + g2 * buf_ref[cur, 1]
    y = _res_ln(h_ref[...], m, g_ref[...], b_ref[...], alpha)
    oh_ref[...] = y
    ohb_ref[...] = y.astype(BF16)


def _split3(w):
    hi = w.astype(BF16)
    r = w - hi.astype(F32)
    mid = r.astype(BF16)
    lo = (r - mid.astype(F32)).astype(BF16)
    return hi, mid, lo


def _router_weights(w_router):
    hi, mid, lo = _split3(w_router.astype(F32))
    pad = jnp.zeros((w_router.shape[0], LANE - 3 * N_EXPERTS), BF16)
    return jnp.concatenate([hi, mid, lo, pad], axis=1)


def _moe(h, info, cnt, w1, w3, w2, g, b, *, layer, alpha, tm=1024, tf=512, fc=256, td=256):
    n, d = h.shape
    f = w1.shape[3]
    tm = min(tm, n)
    td = min(td, n)
    tf = min(tf, f)
    row = lambda i: (i, 0)

    idx = info[:, INFO_IDX:INFO_IDX + 2].astype(I32)
    rank = info[:, INFO_RANK:INFO_RANK + 2].astype(I32)
    counts = cnt[0, :N_EXPERTS].astype(I32)
    padded = ((counts + tm - 1) // tm) * tm
    ends = jnp.cumsum(padded)
    starts = ends - padded
    pos = starts[idx] + rank
    n_tiles = (2 * n) // tm + N_EXPERTS
    n_used = (ends[-1] // tm).astype(I32)
    tile_start = jnp.arange(n_tiles, dtype=I32) * tm
    tile_e = jnp.sum(tile_start[:, None] >= ends[None, :], axis=1).astype(I32)
    tile_e = jnp.minimum(tile_e, tile_e[jnp.maximum(n_used - 1, 0)])
    rows = n_tiles * tm
    pos_blocks = pos.reshape(n // td, 1, 2 * td)

    group_fill = jnp.where(padded > 0, ends - tm, -1)
    tail = n_used + jnp.arange(N_EXPERTS, dtype=I32)
    tail_fill = jnp.where(tail < n_tiles, tail * tm, -1)
    zero_tiles = jnp.concatenate([group_fill, tail_fill]).astype(I32)
    xs = pl.pallas_call(
        _dispatch_kernel,
        grid_spec=pltpu.PrefetchScalarGridSpec(
            num_scalar_prefetch=1,
            grid=(n // td,),
            in_specs=[pl.BlockSpec((1, 1, 2 * td), lambda i, z: (i, 0, 0), memory_space=pltpu.SMEM),
                      pl.BlockSpec((td, d), lambda i, z: (i, 0))],
            out_specs=pl.BlockSpec(memory_space=pl.ANY),
            scratch_shapes=[pltpu.VMEM((tm, d), F32), pltpu.SemaphoreType.DMA(()), pltpu.SemaphoreType.DMA(())],
        ),
        out_shape=jax.ShapeDtypeStruct((rows, d), F32),
        compiler_params=_cp(1),
        name="moe_dispatch",
    )(zero_tiles, pos_blocks, h)

    nj = f // tf

    def x_map(i, j, te, nu):
        return (jnp.minimum(i, nu[0] - 1), 0)

    def w13_map(i, j, te, nu):
        return (layer, te[i], 0, jnp.where(i < nu[0], j, nj - 1))

    def w2_map(i, j, te, nu):
        return (layer, te[i], jnp.where(i < nu[0], j, nj - 1), 0)

    ys = pl.pallas_call(
        functools.partial(_experts_kernel, fc=fc),
        grid_spec=pltpu.PrefetchScalarGridSpec(
            num_scalar_prefetch=2,
            grid=(n_tiles, nj),
            in_specs=[pl.BlockSpec((tm, d), x_map),
                      pl.BlockSpec((None, None, d, tf), w13_map),
                      pl.BlockSpec((None, None, d, tf), w13_map),
                      pl.BlockSpec((None, None, tf, d), w2_map)],
            out_specs=pl.BlockSpec((tm, d), lambda i, j, te, nu: (i, 0)),
            scratch_shapes=[pltpu.VMEM((tm, d), BF16)],
        ),
        out_shape=jax.ShapeDtypeStruct((rows, d), F32),
        compiler_params=_cp(2),
        name="moe_experts",
    )(tile_e, n_used.reshape(1), xs, w1, w3, w2)

    tc = td
    last = n // tc - 1
    return pl.pallas_call(
        functools.partial(_combine_kernel, alpha=alpha),
        grid=(n // tc,),
        in_specs=[pl.BlockSpec((1, 1, 2 * tc), lambda i: (i, 0, 0), memory_space=pltpu.SMEM),
                  pl.BlockSpec((1, 1, 2 * tc), lambda i: (jnp.minimum(i + 1, last), 0, 0), memory_space=pltpu.SMEM),
                  pl.BlockSpec((tc, LANE), row),
                  pl.BlockSpec((tc, d), row),
                  _resident((1, d)), _resident((1, d)),
                  pl.BlockSpec(memory_space=pl.ANY)],
        out_specs=[pl.BlockSpec((tc, d), row), pl.BlockSpec((tc, d), row)],
        out_shape=[jax.ShapeDtypeStruct((n, d), F32), jax.ShapeDtypeStruct((n, d), BF16)],
        scratch_shapes=[pltpu.VMEM((2, 2, tc, d), F32), pltpu.SemaphoreType.DMA((2,))],
        compiler_params=_cp(1),
        name="moe_combine",
    )(pos_blocks, pos_blocks, info, h, g, b, ys)


def _diff_mixer(hb, wq, wk, wv, lq1, lk1, lq2, lk2, sub_g, *, batch, seq, lambda_init):
    d = wq.shape[0]
    heads = d // LANE
    qkv = _linear(hb, (wq.astype(BF16), wk.astype(BF16), wv.astype(BF16)), name="diff_qkv")
    extra = tuple(a.reshape(1, -1).astype(F32) for a in (lq1, lk1, lq2, lk2, sub_g))
    return _flash("diff", qkv, qkv, qkv, batch=batch, seq=seq, groups=heads, q_blk0=0, k_blk0=heads,
                  v_blk0=2 * heads, extra=extra, lambda_init=lambda_init)


def _band_mixer(h, hb, w_qkv, rel_bias, wo, g, b, wr, *, batch, seq, alpha):
    d = w_qkv.shape[0]
    qkv = _linear(hb, (w_qkv.astype(BF16),), name="band_qkv")
    o = _band_attention(qkv, _band_bias_table(rel_bias), batch=batch, seq=seq, pairs=d // LANE)
    return _proj_res_ln_route(o, wo.astype(BF16), h, g, b, wr, alpha=alpha, name="band_out")


def _mla_mixer(hb, w_dq, q_g, w_uq, w_dkv, kv_g, w_ukv, *, batch, seq, tm=512):
    n, d = hb.shape
    q_rank = w_dq.shape[1]
    kv_rank = kv_g.shape[-1]
    heads = w_ukv.shape[1] // (2 * CHUNK)
    nope, rope = 64, 2 * ROPE_HALF
    tm = min(tm, seq)
    row = lambda i: (i, 0)
    tab = lambda i: (i % (seq // tm), 0)

    inv_freq = ROPE_THETA ** (-jnp.arange(ROPE_HALF, dtype=F32) / ROPE_HALF)
    ang = jnp.arange(seq).astype(F32)[:, None] * inv_freq[None, :]
    cos, sin = jnp.cos(ang), jnp.sin(ang)
    z = lambda w: jnp.zeros((seq, w), F32)
    ct_q = jnp.concatenate([jnp.ones((seq, nope), F32), cos, cos, z(LANE - nope - rope)], axis=1)
    st_q = jnp.concatenate([z(nope), sin, sin, z(LANE - nope - rope)], axis=1)
    ct_k = jnp.concatenate([cos, cos, z(LANE - rope)], axis=1)
    st_k = jnp.concatenate([sin, sin, z(LANE - rope)], axis=1)

    wq3 = w_uq.reshape(q_rank, heads, nope + rope)
    qn, q1, q2 = wq3[..., :nope], wq3[..., nope:nope + ROPE_HALF], wq3[..., nope + ROPE_HALF:]
    zq = lambda w: jnp.zeros((q_rank, heads, w), w_uq.dtype)
    wa = jnp.concatenate([qn, q1, q2, zq(LANE - nope - rope)], axis=2).reshape(q_rank, heads * LANE)
    wb = jnp.concatenate([zq(nope), -q2, q1, zq(LANE - nope - rope)], axis=2).reshape(q_rank, heads * LANE)

    k1, k2 = w_dkv[:, kv_rank:kv_rank + ROPE_HALF], w_dkv[:, kv_rank + ROPE_HALF:]
    zd = lambda w: jnp.zeros((d, w), w_dkv.dtype)
    wd = jnp.concatenate([w_dkv[:, :kv_rank], k1, k2, zd(LANE - rope), -k2, k1, zd(LANE - rope)], axis=1)

    wkv3 = w_ukv.reshape(kv_rank, heads, nope + CHUNK)
    zk = lambda r, w: jnp.zeros((r, heads, w), w_ukv.dtype)
    k_top = jnp.concatenate([wkv3[..., :nope], zk(kv_rank, LANE - nope)], axis=2).reshape(kv_rank, heads * LANE)
    eye = jnp.broadcast_to(jnp.eye(rope, dtype=w_ukv.dtype)[:, None, :], (rope, heads, rope))
    k_rope = jnp.concatenate([zk(rope, nope), eye, zk(rope, LANE - nope - rope)], axis=2).reshape(rope, heads * LANE)
    v_top = wkv3[..., nope:].reshape(kv_rank, heads * CHUNK)
    wkv = jnp.concatenate([
        jnp.concatenate([k_top, v_top], axis=1),
        jnp.concatenate([k_rope, jnp.zeros((rope, heads * CHUNK), w_ukv.dtype)], axis=1),
        jnp.zeros((LANE - rope, heads * (LANE + CHUNK)), w_ukv.dtype)], axis=0)

    cq = pl.pallas_call(
        _mla_cq_kernel,
        grid=(n // tm,),
        in_specs=[pl.BlockSpec((tm, d), row), _resident((d, q_rank)), _resident((1, q_rank))],
        out_specs=pl.BlockSpec((tm, q_rank), row),
        out_shape=jax.ShapeDtypeStruct((n, q_rank), BF16),
        compiler_params=_cp(1),
        name="mla_cq",
    )(hb, w_dq.astype(BF16), q_g.reshape(1, -1))

    q = pl.pallas_call(
        functools.partial(_mla_q_kernel, scale=float((nope + rope) ** -0.5), heads=heads),
        grid=(n // tm,),
        in_specs=[pl.BlockSpec((tm, q_rank), row), _resident((q_rank, heads * LANE)),
                  _resident((q_rank, heads * LANE)), pl.BlockSpec((tm, LANE), tab), pl.BlockSpec((tm, LANE), tab)],
        out_specs=pl.BlockSpec((tm, heads * LANE), row),
        out_shape=jax.ShapeDtypeStruct((n, heads * LANE), BF16),
        compiler_params=_cp(1),
        name="mla_q",
    )(cq, wa.astype(BF16), wb.astype(BF16), ct_q, st_q)

    ckr = pl.pallas_call(
        functools.partial(_mla_ckv_kernel, rank=kv_rank),
        grid=(n // tm,),
        in_specs=[pl.BlockSpec((tm, d), row), _resident((d, kv_rank + 2 * LANE)), _resident((1, kv_rank)),
                  pl.BlockSpec((tm, LANE), tab), pl.BlockSpec((tm, LANE), tab)],
        out_specs=pl.BlockSpec((tm, kv_rank + LANE), row),
        out_shape=jax.ShapeDtypeStruct((n, kv_rank + LANE), BF16),
        compiler_params=_cp(1),
        name="mla_ckv",
    )(hb, wd.astype(BF16), kv_g.reshape(1, -1), ct_k, st_k)

    kv = _linear(ckr, (wkv.astype(BF16),), name="mla_kv")
    return _flash("mla", q, kv, kv, batch=batch, seq=seq, groups=heads // 2, q_blk0=0, k_blk0=0, v_blk0=heads)


def _sg_mixer(h, hb, w_in, vg, vb, w_s, b_s, w_out, g, b, wr, *, alpha):
    n, d = hb.shape
    width = w_out.shape[0]
    groups = w_s.shape[0]
    tm = min(ROUTE_TILE, n)
    row = lambda i: (i, 0)
    r_specs, r_shapes, r_scratch = _route_outputs(n, tm)
    u, v = pl.pallas_call(
        functools.partial(_sg_in_kernel, nc=min(512, width)),
        grid=(n // tm,),
        in_specs=[pl.BlockSpec((tm, d), row), _resident((d, 2 * width)), _resident((1, width)),
                  _resident((1, width))],
        out_specs=[pl.BlockSpec((tm, width), row), pl.BlockSpec((tm, width), row)],
        out_shape=[jax.ShapeDtypeStruct((n, width), BF16), jax.ShapeDtypeStruct((n, width), BF16)],
        scratch_shapes=[pltpu.VMEM((tm, width), F32)],
        compiler_params=_cp(1),
        name="sg_in",
    )(hb, w_in.astype(BF16), vg.reshape(1, -1), vb.reshape(1, -1))
    bs_full = jnp.repeat(b_s.T.astype(F32), width // groups, axis=1)
    return pl.pallas_call(
        functools.partial(_sg_mix_kernel, alpha=alpha, groups=groups),
        grid=(n // tm,),
        in_specs=[pl.BlockSpec((tm, width), row), pl.BlockSpec((tm, width), row),
                  _resident((groups, GMLP_CHUNK, GMLP_CHUNK)), _resident((GMLP_CHUNK, width)),
                  _resident((width, d)), pl.BlockSpec((tm, d), row), _resident((1, d)), _resident((1, d)),
                  _resident((d, LANE))],
        out_specs=[pl.BlockSpec((tm, d), row)] + r_specs,
        out_shape=[jax.ShapeDtypeStruct((n, d), F32)] + r_shapes,
        scratch_shapes=[pltpu.VMEM((tm, width), BF16)] + r_scratch,
        compiler_params=_cp(1),
        name="sg_mix",
    )(u, v, w_s.astype(F32), bs_full, w_out.astype(BF16), h, g, b, wr)


def kernel(x, ln_mix_g, ln_mix_b, ln_ffn_g, ln_ffn_b, diff_wq, diff_wk, diff_wv, diff_lq1, diff_lk1, diff_lq2, diff_lk2, diff_sub_g, diff_wo, ca_w_qkv, ca_rel_bias, ca_wo, mla_w_dq, mla_q_norm_g, mla_w_uq, mla_w_dkv, mla_kv_norm_g, mla_w_ukv, mla_wo, sg_w_in, sg_v_norm_g, sg_v_norm_b, sg_w_s, sg_b_s, sg_w_out, ffn_w1, ffn_w3, ffn_w2, moe_w_router, moe_w1, moe_w3, moe_w2):
    batch, seq, d = x.shape
    depth = ln_mix_g.shape[0]
    alpha = float((2 * depth) ** 0.25)
    h = x.reshape(batch * seq, d).astype(F32)
    hb = h
    vec = lambda a: a.reshape(1, -1).astype(F32)
    for i in range(depth):
        kind, j = i % 4, i // 4
        mg, mb = vec(ln_mix_g[i]), vec(ln_mix_b[i])
        fg, fb = vec(ln_ffn_g[i]), vec(ln_ffn_b[i])
        if kind in (0, 2):
            if kind == 0:
                a = _diff_mixer(hb, diff_wq[j], diff_wk[j], diff_wv[j], diff_lq1[j], diff_lk1[j], diff_lq2[j],
                                diff_lk2[j], diff_sub_g[j], batch=batch, seq=seq,
                                lambda_init=0.8 - 0.6 * math.exp(-0.3 * i))
                wo = diff_wo[j]
            else:
                a = _mla_mixer(hb, mla_w_dq[j], mla_q_norm_g[j], mla_w_uq[j], mla_w_dkv[j], mla_kv_norm_g[j],
                               mla_w_ukv[j], batch=batch, seq=seq)
                wo = mla_wo[j]
            h, hb = _mix_ffn(a, wo.astype(BF16), mg, mb, ffn_w1[i // 2].astype(BF16), ffn_w3[i // 2].astype(BF16),
                             ffn_w2[i // 2].astype(BF16), h, fg, fb, alpha=alpha)
        else:
            wr = _router_weights(moe_w_router[i // 2])
            if kind == 1:
                h, info, cnt = _band_mixer(h, hb, ca_w_qkv[j], ca_rel_bias[j], ca_wo[j], mg, mb, wr, batch=batch,
                                           seq=seq, alpha=alpha)
            else:
                h, info, cnt = _sg_mixer(h, hb, sg_w_in[j], sg_v_norm_g[j], sg_v_norm_b[j], sg_w_s[j], sg_b_s[j],
                                         sg_w_out[j], mg, mb, wr, alpha=alpha)
            h, hb = _moe(h, info, cnt, moe_w1, moe_w3, moe_w2, fg, fb, layer=i // 2, alpha=alpha)
    return h.reshape(batch, seq, d).astype(x.dtype)
```

```python
import functools
import math

import jax
import jax.numpy as jnp
import numpy as np
from jax import lax
from jax.experimental import pallas as pl
from jax.experimental.pallas import tpu as pltpu

F32 = jnp.float32
BF16 = jnp.bfloat16
I32 = jnp.int32

LANE = 128
VMEM_LIMIT = 56 * 1024 * 1024

CHUNK = 64
N_EXPERTS = 8
LN_EPS = 1e-5
RMS_EPS = 1e-6
NEG_INF = -1e30
ROPE_THETA = 10000.0
CA_LEFT = 8
CA_REL_CLIP = 128
ROPE_HALF = 16


def _cp(n_axes, vmem=VMEM_LIMIT):
    return pltpu.CompilerParams(dimension_semantics=("arbitrary",) * n_axes, vmem_limit_bytes=vmem)


def _resident(shape):
    nd = len(shape)
    return pl.BlockSpec(shape, lambda *_: (0,) * nd, pipeline_mode=pl.Buffered(1))


def _res_ln(h, m, g, b, alpha):
    z = alpha * h + m
    mu = jnp.mean(z, axis=-1, keepdims=True)
    zc = z - mu
    var = jnp.mean(zc * zc, axis=-1, keepdims=True)
    return zc * lax.rsqrt(var + LN_EPS) * g + b


def _rms(x, g, eps):
    ms = jnp.mean(x * x, axis=-1, keepdims=True)
    return x * lax.rsqrt(ms + eps) * g


def _linear_kernel(x_ref, *refs):
    *w_refs, o_ref = refs
    x = x_ref[...].astype(BF16)
    col = 0
    for w_ref in w_refs:
        n = w_ref.shape[1]
        o_ref[:, col:col + n] = jnp.dot(x, w_ref[...], preferred_element_type=F32).astype(o_ref.dtype)
        col += n


def _linear(x, ws, *, tm=512, out_dtype=BF16, name="linear"):
    m, k = x.shape
    n = sum(w.shape[1] for w in ws)
    tm = min(tm, m)
    return pl.pallas_call(
        _linear_kernel,
        grid=(m // tm,),
        in_specs=[pl.BlockSpec((tm, k), lambda i: (i, 0))] + [_resident(w.shape) for w in ws],
        out_specs=pl.BlockSpec((tm, n), lambda i: (i, 0)),
        out_shape=jax.ShapeDtypeStruct((m, n), out_dtype),
        compiler_params=_cp(1),
        name=name,
    )(x, *ws)


ROUTE_TILE = 512


def _route_outputs(m, tm):
    specs = [pl.BlockSpec((tm, LANE), lambda i: (i, 0)), pl.BlockSpec((1, LANE), lambda i: (0, 0))]
    shapes = [jax.ShapeDtypeStruct((m, LANE), F32), jax.ShapeDtypeStruct((1, LANE), F32)]
    scratch = [pltpu.VMEM((1, LANE), F32), pltpu.VMEM((tm, tm), BF16)]
    return specs, shapes, scratch


def _proj_route_kernel(a_ref, w_ref, h_ref, g_ref, b_ref, wr_ref, oh_ref, info_ref, cnt_ref, run_ref, before_ref, *,
                       alpha):
    pl.when(pl.program_id(0) == 0)(lambda: _route_init(run_ref, before_ref))
    m = jnp.dot(a_ref[...], w_ref[...], preferred_element_type=F32)
    y = _res_ln(h_ref[...], m, g_ref[...], b_ref[...], alpha)
    oh_ref[...] = y
    _route(y, wr_ref, run_ref, before_ref, info_ref, cnt_ref)


def _proj_res_ln_route(a, w, h, g, b, wr, *, alpha, name):
    m, k = a.shape
    d = w.shape[1]
    tm = min(ROUTE_TILE, m)
    row = lambda i: (i, 0)
    r_specs, r_shapes, r_scratch = _route_outputs(m, tm)
    return pl.pallas_call(
        functools.partial(_proj_route_kernel, alpha=alpha),
        grid=(m // tm,),
        in_specs=[pl.BlockSpec((tm, k), row), _resident((k, d)), pl.BlockSpec((tm, d), row),
                  _resident((1, d)), _resident((1, d)), _resident((d, LANE))],
        out_specs=[pl.BlockSpec((tm, d), row)] + r_specs,
        out_shape=[jax.ShapeDtypeStruct((m, d), F32)] + r_shapes,
        scratch_shapes=r_scratch,
        compiler_params=_cp(1),
        name=name,
    )(a, w, h, g, b, wr)


def _mix_ffn_kernel(a_ref, wo_ref, mg_ref, mb_ref, w1_ref, w3_ref, w2_ref, h_ref, g_ref, b_ref, oh_ref, ohb_ref,
                    hm_ref, xb_ref, acc_ref, *, alpha, fc):
    mix = jnp.dot(a_ref[...], wo_ref[...], preferred_element_type=F32)
    hm = _res_ln(h_ref[...], mix, mg_ref[...], mb_ref[...], alpha)
    hm_ref[...] = hm
    xb_ref[...] = hm.astype(BF16)
    x = xb_ref[...]
    f = w1_ref.shape[1]
    for c in range(f // fc):
        sl = slice(c * fc, (c + 1) * fc)
        a = jnp.dot(x, w1_ref[:, sl], preferred_element_type=F32)
        b3 = jnp.dot(x, w3_ref[:, sl], preferred_element_type=F32)
        gated = (jax.nn.silu(a) * b3).astype(BF16)
        part = jnp.dot(gated, w2_ref[sl, :], preferred_element_type=F32)
        if c == 0:
            acc_ref[...] = part
        else:
            acc_ref[...] += part
    y = _res_ln(hm_ref[...], acc_ref[...], g_ref[...], b_ref[...], alpha)
    oh_ref[...] = y
    ohb_ref[...] = y.astype(BF16)


def _mix_ffn(a, wo, mg, mb, w1, w3, w2, h, g, b, *, alpha, tm=512, fc=256):
    m, d = h.shape
    k = a.shape[1]
    f = w1.shape[1]
    tm = min(tm, m)
    row = lambda i: (i, 0)
    vecs = [_resident((1, d)), _resident((1, d))]
    return pl.pallas_call(
        functools.partial(_mix_ffn_kernel, alpha=alpha, fc=fc),
        grid=(m // tm,),
        in_specs=[pl.BlockSpec((tm, k), row), _resident((k, d))] + vecs
                 + [_resident((d, f)), _resident((d, f)), _resident((f, d)), pl.BlockSpec((tm, d), row)] + vecs,
        out_specs=[pl.BlockSpec((tm, d), row), pl.BlockSpec((tm, d), row)],
        out_shape=[jax.ShapeDtypeStruct((m, d), F32), jax.ShapeDtypeStruct((m, d), BF16)],
        scratch_shapes=[pltpu.VMEM((tm, d), F32), pltpu.VMEM((tm, d), BF16), pltpu.VMEM((tm, d), F32)],
        compiler_params=_cp(1),
        name="mix_ffn",
    )(a, wo, mg, mb, w1, w3, w2, h, g, b)


ONES_ROWS = 16
ACC_ROWS = LANE + ONES_ROWS


def _vt_ext(v_blk):
    vt = v_blk.astype(F32).T.astype(BF16)
    return jnp.concatenate([vt, jnp.ones((ONES_ROWS, v_blk.shape[0]), BF16)], axis=0)


def _split_heads(x):
    low = lax.broadcasted_iota(I32, x.shape, 1) < CHUNK
    zero = jnp.zeros_like(x)
    return jnp.where(low, x, zero), jnp.where(low, zero, x)


_NT = (((1,), (1,)), ((), ()))


def _flash_kernel(*refs, mode, tq, gs, lambda_init):
    if mode == "diff":
        lq1_ref, lk1_ref, lq2_ref, lk2_ref, subg_ref, q_ref, k_ref, v_ref, o_ref, *scratch = refs
    else:
        q_ref, k_ref, v_ref, o_ref, *scratch = refs
    vt_ref, m_ref, a_ref, bm_ref, acc_ref, s_ref, p_ref = scratch
    qi = pl.program_id(2)
    tk = tq // 2
    nmap = 2 * gs
    lanes = lambda i: slice(i * LANE, (i + 1) * LANE)

    @pl.when(qi == 0)
    def _():
        for j in range(vt_ref.shape[0]):
            for g in range(gs):
                vt_ref[j, g] = _vt_ext(v_ref[j * tk:(j + 1) * tk, lanes(g)])

    qs = []
    for g in range(gs):
        if mode == "diff":
            qs += list(_split_heads(q_ref[:, lanes(g)] * jnp.asarray(CHUNK ** -0.5, BF16)))
        else:
            qs += [q_ref[:, lanes(2 * g)], q_ref[:, lanes(2 * g + 1)]]
    m_ref[...] = jnp.full(m_ref.shape, NEG_INF, F32)
    acc_ref[...] = jnp.zeros(acc_ref.shape, F32)

    def scores(blk):
        start = pl.multiple_of(blk * tk, tk)
        kb = k_ref[pl.ds(start, tk), :]
        kks = [kb[:, lanes(mp // 2 if mode == "diff" else mp)] for mp in range(nmap)]
        return [lax.dot_general(kks[mp], qs[mp], _NT, preferred_element_type=F32) for mp in range(nmap)]

    def values(blk, slot):
        return [jnp.dot(vt_ref[blk, mp // 2], p_ref[slot, mp], preferred_element_type=F32) for mp in range(nmap)]

    def stash(slot, ss):
        for mp in range(nmap):
            s_ref[slot, mp] = ss[mp]
            bm_ref[slot, mp] = jnp.max(ss[mp], axis=0, keepdims=True)

    def softmax(slot, masked):
        for mp in range(nmap):
            s = s_ref[slot, mp]
            if masked:
                kc = (lax.broadcasted_iota(I32, s.shape, 0) + slot * tk) >> 6
                qc = lax.broadcasted_iota(I32, s.shape, 1) >> 6
                s = jnp.where(kc <= qc, s, NEG_INF)
                blk_max = jnp.max(s, axis=0, keepdims=True)
            else:
                blk_max = bm_ref[slot, mp]
            m_old = m_ref[mp]
            m_new = jnp.maximum(m_old, blk_max)
            p_ref[slot, mp] = jnp.exp(s - m_new).astype(BF16)
            a_ref[slot, mp] = jnp.exp(m_old - m_new)
            m_ref[mp] = m_new

    def fold(pv, scale):
        for mp in range(nmap):
            acc_ref[mp] = scale[mp] * acc_ref[mp] + pv[mp]

    def trip(blk, masked, produce):
        scale = [[a_ref[u, mp] for mp in range(nmap)] for u in range(2)]
        pvs = [values(jnp.maximum(blk - 2 + u, 0), u) for u in range(2)]
        nxt = [scores(blk + 2 + u) for u in range(2)] if produce else []
        for u in range(2):
            softmax(u, masked)
        for u in range(2):
            fold(pvs[u], scale[u])
        for u, s in enumerate(nxt):
            stash(u, s)

    for u in range(2):
        stash(u, scores(u))
    p_ref[...] = jnp.zeros(p_ref.shape, BF16)
    a_ref[...] = jnp.ones(a_ref.shape, F32)

    def step(j, carry):
        trip(2 * j, False, True)
        return carry

    lax.fori_loop(0, qi, step, 0)
    trip(2 * qi, True, False)
    for u in range(2):
        fold(values(2 * qi + u, u), [a_ref[u, mp] for mp in range(nmap)])

    for g in range(gs):
        acc_a, acc_b = acc_ref[2 * g], acc_ref[2 * g + 1]
        oa = acc_a[:LANE] / acc_a[LANE:LANE + 1]
        ob = acc_b[:LANE] / acc_b[LANE:LANE + 1]
        if mode == "diff":
            lam = (jnp.exp(jnp.sum(lq1_ref[...] * lk1_ref[...], axis=-1, keepdims=True))
                   - jnp.exp(jnp.sum(lq2_ref[...] * lk2_ref[...], axis=-1, keepdims=True)) + lambda_init)
            o = (oa - lam * ob).T
            o = _rms(o, subg_ref[...], 1e-5) * (1.0 - lambda_init)
        else:
            row = lax.broadcasted_iota(I32, oa.shape, 0)
            o = jnp.where(row < CHUNK, oa, ob).T
        o_ref[:, lanes(g)] = o.astype(o_ref.dtype)


def _flash(mode, q_arr, k_arr, v_arr, *, batch, seq, groups, q_blk0, k_blk0, v_blk0, extra=(), lambda_init=0.0,
           tq=512, gs=2):
    qw = (LANE if mode == "diff" else 2 * LANE) * gs
    tq = min(tq, seq)
    tk = tq // 2
    nq = seq // tq
    n = batch * seq
    nmap = 2 * gs
    assert groups % gs == 0 and q_blk0 % gs == 0 and k_blk0 % gs == 0 and v_blk0 % gs == 0
    qb, kb, vb = q_blk0 // gs, k_blk0 // gs, v_blk0 // gs
    in_specs = [pl.BlockSpec(e.shape, lambda b, g, i: (0, 0)) for e in extra]
    in_specs += [
        pl.BlockSpec((tq, qw), lambda b, g, i: (b * nq + i, qb + g)),
        pl.BlockSpec((seq, qw), lambda b, g, i: (b, kb + g)),
        pl.BlockSpec((seq, gs * LANE), lambda b, g, i: (b, vb + g)),
    ]
    return pl.pallas_call(
        functools.partial(_flash_kernel, mode=mode, tq=tq, gs=gs, lambda_init=lambda_init),
        grid=(batch, groups // gs, nq),
        in_specs=in_specs,
        out_specs=pl.BlockSpec((tq, gs * LANE), lambda b, g, i: (b * nq + i, g)),
        out_shape=jax.ShapeDtypeStruct((n, groups * LANE), BF16),
        scratch_shapes=[pltpu.VMEM((seq // tk, gs, ACC_ROWS, tk), BF16), pltpu.VMEM((nmap, 1, tq), F32),
                        pltpu.VMEM((2, nmap, 1, tq), F32), pltpu.VMEM((2, nmap, 1, tq), F32),
                        pltpu.VMEM((nmap, ACC_ROWS, tq), F32),
                        pltpu.VMEM((2, nmap, tk, tq), F32), pltpu.VMEM((2, nmap, tk, tq), BF16)],
        compiler_params=_cp(3),
        name="flash_" + mode,
    )(*extra, q_arr, k_arr, v_arr)


BAND_TQ = 4 * CHUNK
BAND_W = (CA_LEFT + 4) * CHUNK
BAND_PAD = CA_LEFT * CHUNK
BAND_NW = BAND_W // BAND_TQ
BAND_EXT = 1024


def _band_kernel(q_ref, k_ref, v_ref, ext_ref, o_ref, kpad_ref, vt_ref, s_ref, bm_ref, bias_ref, *, seq, gp):
    t = pl.program_id(2)
    nt = pl.num_programs(2)
    npad = BAND_PAD // BAND_TQ

    nh = 2 * gp
    lanes = lambda i: slice(i * LANE, (i + 1) * LANE)

    def produce(tile, slot, front):
        qr = q_ref[pl.ds(pl.multiple_of(tile * BAND_TQ, BAND_TQ), BAND_TQ), :]
        qhs = []
        for g in range(gp):
            qhs += list(_split_heads(qr[:, lanes(g)] * jnp.asarray(CHUNK ** -0.5, BF16)))
        tops = [None] * nh
        for u in range(BAND_NW):
            start = pl.multiple_of((tile + u) * BAND_TQ, BAND_TQ)
            kb = kpad_ref[pl.ds(start, BAND_TQ), :]
            for hh in range(nh):
                s = lax.dot_general(kb[:, lanes(hh // 2)], qhs[hh], _NT, preferred_element_type=F32)
                s = s + bias_ref[hh, u * BAND_TQ:(u + 1) * BAND_TQ, :]
                if front:
                    invalid = lax.broadcasted_iota(I32, s.shape, 0) + (tile + u) * BAND_TQ < BAND_PAD
                    s = jnp.where(invalid, NEG_INF, s)
                s_ref[slot, hh, u] = s
                top = jnp.max(s, axis=0, keepdims=True)
                tops[hh] = top if tops[hh] is None else jnp.maximum(tops[hh], top)
        for hh in range(nh):
            bm_ref[slot, hh] = tops[hh]

    @pl.when((t == 0) & (pl.program_id(1) == 0))
    def _():
        ci = lax.broadcasted_iota(I32, (BAND_W, BAND_TQ), 1) >> 6
        cj = lax.broadcasted_iota(I32, (BAND_W, BAND_TQ), 0) >> 6
        vis = (cj >= ci) & (cj <= ci + CA_LEFT)
        for hh in range(nh):
            rows = jnp.broadcast_to(ext_ref[hh // 2, hh % 2], (BAND_W, BAND_EXT))
            skew = pltpu.roll(rows, BAND_EXT - (BAND_W - 1), 1, stride=1, stride_axis=0)
            bias_ref[hh] = jnp.where(vis, skew[:, :BAND_TQ], NEG_INF)

    @pl.when(t == 0)
    def _():
        kpad_ref[0:BAND_PAD, :] = jnp.zeros((BAND_PAD, gp * LANE), BF16)
        kpad_ref[BAND_PAD:BAND_PAD + seq, :] = k_ref[...]
        for j in range(npad):
            vt_ref[j] = jnp.zeros((gp, ACC_ROWS, BAND_TQ), BF16)
        for j in range(seq // BAND_TQ):
            for g in range(gp):
                vt_ref[npad + j, g] = _vt_ext(v_ref[j * BAND_TQ:(j + 1) * BAND_TQ, lanes(g)])
        produce(0, 0, True)

    cur = t & 1
    nxt_tile = jnp.minimum(t + 1, nt - 1)

    def consume():
        outs = []
        for hh in range(nh):
            m = bm_ref[cur, hh]
            acc = None
            for u in range(BAND_NW):
                p = jnp.exp(s_ref[cur, hh, u] - m).astype(BF16)
                pv = jnp.dot(vt_ref[t + u, hh // 2], p, preferred_element_type=F32)
                acc = pv if acc is None else acc + pv
            outs.append(acc[:LANE] / acc[LANE:LANE + 1])
        row = lax.broadcasted_iota(I32, outs[0].shape, 0)
        rows = pl.ds(pl.multiple_of(t * BAND_TQ, BAND_TQ), BAND_TQ)
        for g in range(gp):
            o_ref[rows, lanes(g)] = jnp.where(row < CHUNK, outs[2 * g], outs[2 * g + 1]).T.astype(o_ref.dtype)

    def run(front):
        consume()
        produce(nxt_tile, 1 - cur, front)

    pl.when(t + 1 < npad)(lambda: run(True))
    pl.when(t + 1 >= npad)(lambda: run(False))


def _band_attention(qkv, ext, *, batch, seq, pairs, gp=2):
    n = batch * seq
    nt = seq // BAND_TQ
    assert pairs % gp == 0
    steps, w, nh = pairs // gp, gp * LANE, 2 * gp
    return pl.pallas_call(
        functools.partial(_band_kernel, seq=seq, gp=gp),
        grid=(steps, batch, nt),
        in_specs=[
            pl.BlockSpec((seq, w), lambda g, b, t: (b, g)),
            pl.BlockSpec((seq, w), lambda g, b, t: (b, steps + g)),
            pl.BlockSpec((seq, w), lambda g, b, t: (b, 2 * steps + g)),
            pl.BlockSpec((gp, 2, 1, BAND_EXT), lambda g, b, t: (g, 0, 0, 0)),
        ],
        out_specs=pl.BlockSpec((seq, w), lambda g, b, t: (b, g)),
        out_shape=jax.ShapeDtypeStruct((n, pairs * LANE), BF16),
        scratch_shapes=[pltpu.VMEM((seq + BAND_PAD, w), BF16),
                        pltpu.VMEM(((seq + BAND_PAD) // BAND_TQ, gp, ACC_ROWS, BAND_TQ), BF16),
                        pltpu.VMEM((2, nh, BAND_NW, BAND_TQ, BAND_TQ), F32),
                        pltpu.VMEM((2, nh, 1, BAND_TQ), F32),
                        pltpu.VMEM((nh, BAND_W, BAND_TQ), F32)],
        compiler_params=_cp(3),
        name="band_attention",
    )(qkv, qkv, qkv, ext)


def _band_bias_table(rel_bias):
    heads = rel_bias.shape[0]
    rel = np.minimum(np.arange(BAND_EXT), BAND_TQ + BAND_W - 2) - (BAND_W - 1) + BAND_PAD
    ext = rel_bias.astype(F32)[:, np.clip(rel, -CA_REL_CLIP, CA_REL_CLIP) + CA_REL_CLIP]
    return ext.reshape(heads // 2, 2, 1, BAND_EXT)


def _mla_cq_kernel(x_ref, w_ref, g_ref, o_ref):
    c = jnp.dot(x_ref[...], w_ref[...], preferred_element_type=F32)
    o_ref[...] = _rms(c, g_ref[...], RMS_EPS).astype(o_ref.dtype)


def _mla_q_kernel(c_ref, wa_ref, wb_ref, ct_ref, st_ref, o_ref, *, scale, heads):
    c = c_ref[...]
    a = jnp.dot(c, wa_ref[...], preferred_element_type=F32)
    b = jnp.dot(c, wb_ref[...], preferred_element_type=F32)
    ct = ct_ref[...] * scale
    st = st_ref[...] * scale
    for h in range(heads):
        sl = slice(h * LANE, (h + 1) * LANE)
        o_ref[:, sl] = (a[:, sl] * ct + b[:, sl] * st).astype(o_ref.dtype)


def _mla_ckv_kernel(x_ref, w_ref, g_ref, ct_ref, st_ref, o_ref, *, rank):
    y = jnp.dot(x_ref[...], w_ref[...], preferred_element_type=F32)
    o_ref[:, :rank] = _rms(y[:, :rank], g_ref[...], RMS_EPS).astype(o_ref.dtype)
    kr = y[:, rank:rank + LANE] * ct_ref[...] + y[:, rank + LANE:rank + 2 * LANE] * st_ref[...]
    o_ref[:, rank:] = kr.astype(o_ref.dtype)


GMLP_CHUNK = 128


def _sg_in_kernel(x_ref, w_ref, g_ref, b_ref, u_ref, v_ref, vbuf_ref, *, nc):
    width = u_ref.shape[1]
    x = x_ref[...]
    for c in range(0, 2 * width, nc):
        hc = jax.nn.gelu(jnp.dot(x, w_ref[:, c:c + nc], preferred_element_type=F32))
        if c < width:
            u_ref[:, c:c + nc] = hc.astype(u_ref.dtype)
        else:
            vbuf_ref[:, c - width:c - width + nc] = hc
    v = vbuf_ref[...]
    mu = jnp.mean(v, axis=-1, keepdims=True)
    vc = v - mu
    var = jnp.mean(vc * vc, axis=-1, keepdims=True)
    v_ref[...] = (vc * lax.rsqrt(var + LN_EPS) * g_ref[...] + b_ref[...]).astype(v_ref.dtype)


def _sg_mix_kernel(u_ref, v_ref, ws_ref, bs_ref, wo_ref, h_ref, g_ref, b_ref, wr_ref, oh_ref, info_ref, cnt_ref,
                   gated_ref, run_ref, before_ref, *, alpha, groups):
    tm = u_ref.shape[0]
    pl.when(pl.program_id(0) == 0)(lambda: _route_init(run_ref, before_ref))
    r = lax.broadcasted_iota(I32, (GMLP_CHUNK, GMLP_CHUNK), 0) >> 6
    c = lax.broadcasted_iota(I32, (GMLP_CHUNK, GMLP_CHUNK), 1) >> 6
    vis = c <= r
    for gi in range(groups):
        w = jnp.where(vis, ws_ref[gi], 0.0).astype(BF16)
        cs = slice(gi * LANE, (gi + 1) * LANE)
        for ch in range(tm // GMLP_CHUNK):
            rs = slice(ch * GMLP_CHUNK, (ch + 1) * GMLP_CHUNK)
            mixed = jnp.dot(w, v_ref[rs, cs], preferred_element_type=F32) + bs_ref[:, cs]
            gated_ref[rs, cs] = (u_ref[rs, cs].astype(F32) * mixed).astype(BF16)
    m = jnp.dot(gated_ref[...], wo_ref[...], preferred_element_type=F32)
    y = _res_ln(h_ref[...], m, g_ref[...], b_ref[...], alpha)
    oh_ref[...] = y
    _route(y, wr_ref, run_ref, before_ref, info_ref, cnt_ref)


INFO_IDX, INFO_RANK, INFO_GATE = 0, 2, 4
SUBLANES = 8


def _route_init(run_ref, before_ref):
    tr = before_ref.shape[0]
    run_ref[...] = jnp.zeros_like(run_ref)
    rr = lax.broadcasted_iota(I32, (tr, tr), 0)
    cc = lax.broadcasted_iota(I32, (tr, tr), 1)
    before_ref[...] = jnp.where(cc < rr, 1.0, 0.0).astype(BF16)


def _route(x, w_ref, run_ref, before_ref, info_ref, cnt_ref):
    tr = x.shape[0]
    x1 = x.astype(BF16)
    r1 = x - x1.astype(F32)
    x2 = r1.astype(BF16)
    x3 = (r1 - x2.astype(F32)).astype(BF16)
    w = w_ref[...]
    lane = lax.broadcasted_iota(I32, (tr, LANE), 1)
    t = (jnp.dot(x1, w, preferred_element_type=F32)
         + jnp.where(lane < 2 * N_EXPERTS, jnp.dot(x2, w, preferred_element_type=F32), 0.0)
         + jnp.where(lane < N_EXPERTS, jnp.dot(x3, w, preferred_element_type=F32), 0.0))
    logits = t + pltpu.roll(t, LANE - N_EXPERTS, 1) + pltpu.roll(t, LANE - 2 * N_EXPERTS, 1)
    big = jnp.float32(3e38)
    logits = jnp.where(lane < N_EXPERTS, logits, -big)

    lane_f = lane.astype(F32)
    v1 = jnp.max(logits, axis=-1, keepdims=True)
    i1 = jnp.min(jnp.where(logits == v1, lane_f, float(LANE)), axis=-1, keepdims=True)
    rest = jnp.where(lane_f == i1, -big, logits)
    v2 = jnp.max(rest, axis=-1, keepdims=True)
    i2 = jnp.min(jnp.where(rest == v2, lane_f, float(LANE)), axis=-1, keepdims=True)
    e = jnp.exp(v2 - v1)
    g1 = 1.0 / (1.0 + e)
    g2 = e / (1.0 + e)

    oh1 = jnp.where(lane_f == i1, 1.0, 0.0)
    oh2 = jnp.where(lane_f == i2, 1.0, 0.0)
    oh = oh1 + oh2
    prior = jnp.dot(before_ref[...], oh.astype(BF16), preferred_element_type=F32) + run_ref[...]
    rank1 = jnp.sum(prior * oh1, axis=-1, keepdims=True)
    rank2 = jnp.sum(prior * oh2, axis=-1, keepdims=True)
    run_ref[...] += jnp.sum(oh, axis=0, keepdims=True)
    cnt_ref[...] = run_ref[...]

    info = jnp.where(lane == INFO_IDX, i1, 0.0)
    info = jnp.where(lane == INFO_IDX + 1, i2, info)
    info = jnp.where(lane == INFO_RANK, rank1, info)
    info = jnp.where(lane == INFO_RANK + 1, rank2, info)
    info = jnp.where(lane == INFO_GATE, g1, info)
    info = jnp.where(lane == INFO_GATE + 1, g2, info)
    info_ref[...] = info


def _dispatch_kernel(zoff_ref, pos_ref, h_ref, xs_ref, zero_ref, sem, zsem):
    td = h_ref.shape[0]

    @pl.when(pl.program_id(0) == 0)
    def _():
        tile = zero_ref.shape[0]
        zero_ref[...] = jnp.zeros_like(zero_ref)

        def fill(k):
            start = pl.multiple_of(jnp.maximum(zoff_ref[k], 0), tile)
            return pltpu.make_async_copy(zero_ref, xs_ref.at[pl.ds(start, tile), :], zsem)

        for k in range(zoff_ref.shape[0]):
            pl.when(zoff_ref[k] >= 0)(lambda k=k: fill(k).start())
        for k in range(zoff_ref.shape[0]):
            pl.when(zoff_ref[k] >= 0)(lambda k=k: fill(k).wait())

    def start(g, carry):
        base = pl.multiple_of(g * SUBLANES, SUBLANES)
        for s in range(SUBLANES):
            for k in range(2):
                p = pos_ref[0, 0, 2 * (base + s) + k]
                pltpu.make_async_copy(h_ref.at[pl.ds(base + s, 1), :], xs_ref.at[pl.ds(p, 1), :],
                                      sem).start(priority=k)
        return carry

    lax.fori_loop(0, td // SUBLANES, start, 0)
    for _ in range(2):
        pltpu.make_async_copy(h_ref, xs_ref.at[pl.ds(0, td), :], sem).wait()


def _experts_kernel(te_ref, nu_ref, x_ref, w1_ref, w3_ref, w2_ref, o_ref, xb_ref, *, fc):
    i = pl.program_id(0)
    j = pl.program_id(1)
    del te_ref

    @pl.when(j == 0)
    def _():
        o_ref[...] = jnp.zeros_like(o_ref)
        xb_ref[...] = x_ref[...].astype(BF16)

    @pl.when(i < nu_ref[0])
    def _():
        x = xb_ref[...]
        tf = w1_ref.shape[1]
        gated = []
        for c in range(tf // fc):
            sl = slice(c * fc, (c + 1) * fc)
            a = jnp.dot(x, w1_ref[:, sl].astype(BF16), preferred_element_type=F32)
            b3 = jnp.dot(x, w3_ref[:, sl].astype(BF16), preferred_element_type=F32)
            gated.append((jax.nn.silu(a) * b3).astype(BF16))
        o_ref[...] += jnp.dot(jnp.concatenate(gated, axis=1), w2_ref[...].astype(BF16),
                              preferred_element_type=F32)


def _combine_kernel(pos_ref, posn_ref, info_ref, h_ref, g_ref, b_ref, ys_ref, oh_ref, ohb_ref, buf_ref, sem, *,
                    alpha):
    i = pl.program_id(0)
    tc = h_ref.shape[0]

    def gather(p_ref, slot):
        def start(g, carry):
            base = pl.multiple_of(g * SUBLANES, SUBLANES)
            for s in range(SUBLANES):
                for k in range(2):
                    p = p_ref[0, 0, 2 * (base + s) + k]
                    pltpu.make_async_copy(ys_ref.at[pl.ds(p, 1), :], buf_ref.at[slot, k, pl.ds(base + s, 1), :],
                                          sem.at[slot]).start(priority=k)
            return carry

        lax.fori_loop(0, tc // SUBLANES, start, 0)

    @pl.when(i == 0)
    def _():
        gather(pos_ref, 0)

    @pl.when(i + 1 < pl.num_programs(0))
    def _():
        gather(posn_ref, (i + 1) & 1)

    cur = i & 1
    for k in range(2):
        pltpu.make_async_copy(ys_ref.at[pl.ds(0, tc), :], buf_ref.at[cur, k], sem.at[cur]).wait()
    info = info_ref[...]
    g1 = info[:, INFO_GATE:INFO_GATE + 1]
    g2 = info[:, INFO_GATE + 1:INFO_GATE + 2]
    m = g1 * buf_ref[cur, 0] + g2 * buf_ref[cur, 1]
    y = _res_ln(h_ref[...], m, g_ref[...], b_ref[...], alpha)
    oh_ref[...] = y
    ohb_ref[...] = y.astype(BF16)


def _split3(w):
    hi = w.astype(BF16)
    r = w - hi.astype(F32)
    mid = r.astype(BF16)
    lo = (r - mid.astype(F32)).astype(BF16)
    return hi, mid, lo


def _router_weights(w_router):
    hi, mid, lo = _split3(w_router.astype(F32))
    pad = jnp.zeros((w_router.shape[0], LANE - 3 * N_EXPERTS), BF16)
    return jnp.concatenate([hi, mid, lo, pad], axis=1)


def _moe(h, info, cnt, w1, w3, w2, g, b, *, layer, alpha, tm=1024, tf=512, fc=256, td=512):
    n, d = h.shape
    f = w1.shape[3]
    tm = min(tm, n)
    td = min(td, n)
    tf = min(tf, f)
    row = lambda i: (i, 0)

    idx = info[:, INFO_IDX:INFO_IDX + 2].astype(I32)
    rank = info[:, INFO_RANK:INFO_RANK + 2].astype(I32)
    counts = cnt[0, :N_EXPERTS].astype(I32)
    padded = ((counts + tm - 1) // tm) * tm
    ends = jnp.cumsum(padded)
    starts = ends - padded
    pos = starts[idx] + rank
    n_tiles = (2 * n) // tm + N_EXPERTS
    n_used = (ends[-1] // tm).astype(I32)
    tile_start = jnp.arange(n_tiles, dtype=I32) * tm
    tile_e = jnp.sum(tile_start[:, None] >= ends[None, :], axis=1).astype(I32)
    tile_e = jnp.minimum(tile_e, tile_e[jnp.maximum(n_used - 1, 0)])
    rows = n_tiles * tm
    pos_blocks = pos.reshape(n // td, 1, 2 * td)

    group_fill = jnp.where(padded > 0, ends - tm, -1)
    tail = n_used + jnp.arange(N_EXPERTS, dtype=I32)
    tail_fill = jnp.where(tail < n_tiles, tail * tm, -1)
    zero_tiles = jnp.concatenate([group_fill, tail_fill]).astype(I32)
    xs = pl.pallas_call(
        _dispatch_kernel,
        grid_spec=pltpu.PrefetchScalarGridSpec(
            num_scalar_prefetch=1,
            grid=(n // td,),
            in_specs=[pl.BlockSpec((1, 1, 2 * td), lambda i, z: (i, 0, 0), memory_space=pltpu.SMEM),
                      pl.BlockSpec((td, d), lambda i, z: (i, 0))],
            out_specs=pl.BlockSpec(memory_space=pl.ANY),
            scratch_shapes=[pltpu.VMEM((tm, d), F32), pltpu.SemaphoreType.DMA(()), pltpu.SemaphoreType.DMA(())],
        ),
        out_shape=jax.ShapeDtypeStruct((rows, d), F32),
        compiler_params=_cp(1),
        name="moe_dispatch",
    )(zero_tiles, pos_blocks, h)

    nj = f // tf

    def x_map(i, j, te, nu):
        return (jnp.minimum(i, nu[0] - 1), 0)

    def w13_map(i, j, te, nu):
        return (layer, te[i], 0, jnp.where(i < nu[0], j, nj - 1))

    def w2_map(i, j, te, nu):
        return (layer, te[i], jnp.where(i < nu[0], j, nj - 1), 0)

    ys = pl.pallas_call(
        functools.partial(_experts_kernel, fc=fc),
        grid_spec=pltpu.PrefetchScalarGridSpec(
            num_scalar_prefetch=2,
            grid=(n_tiles, nj),
            in_specs=[pl.BlockSpec((tm, d), x_map),
                      pl.BlockSpec((None, None, d, tf), w13_map),
                      pl.BlockSpec((None, None, d, tf), w13_map),
                      pl.BlockSpec((None, None, tf, d), w2_map)],
            out_specs=pl.BlockSpec((tm, d), lambda i, j, te, nu: (i, 0)),
            scratch_shapes=[pltpu.VMEM((tm, d), BF16)],
        ),
        out_shape=jax.ShapeDtypeStruct((rows, d), F32),
        compiler_params=_cp(2),
        name="moe_experts",
    )(tile_e, n_used.reshape(1), xs, w1, w3, w2)

    tc = td
    last = n // tc - 1
    return pl.pallas_call(
        functools.partial(_combine_kernel, alpha=alpha),
        grid=(n // tc,),
        in_specs=[pl.BlockSpec((1, 1, 2 * tc), lambda i: (i, 0, 0), memory_space=pltpu.SMEM),
                  pl.BlockSpec((1, 1, 2 * tc), lambda i: (jnp.minimum(i + 1, last), 0, 0), memory_space=pltpu.SMEM),
                  pl.BlockSpec((tc, LANE), row),
                  pl.BlockSpec((tc, d), row),
                  _resident((1, d)), _resident((1, d)),
                  pl.BlockSpec(memory_space=pl.ANY)],
        out_specs=[pl.BlockSpec((tc, d), row), pl.BlockSpec((tc, d), row)],
        out_shape=[jax.ShapeDtypeStruct((n, d), F32), jax.ShapeDtypeStruct((n, d), BF16)],
        scratch_shapes=[pltpu.VMEM((2, 2, tc, d), F32), pltpu.SemaphoreType.DMA((2,))],
        compiler_params=_cp(1),
        name="moe_combine",
    )(pos_blocks, pos_blocks, info, h, g, b, ys)


def _diff_mixer(hb, wq, wk, wv, lq1, lk1, lq2, lk2, sub_g, *, batch, seq, lambda_init):
    d = wq.shape[0]
    heads = d // LANE
    qkv = _linear(hb, (wq.astype(BF16), wk.astype(BF16), wv.astype(BF16)), name="diff_qkv")
    extra = tuple(a.reshape(1, -1).astype(F32) for a in (lq1, lk1, lq2, lk2, sub_g))
    return _flash("diff", qkv, qkv, qkv, batch=batch, seq=seq, groups=heads, q_blk0=0, k_blk0=heads,
                  v_blk0=2 * heads, extra=extra, lambda_init=lambda_init)


def _band_mixer(h, hb, w_qkv, rel_bias, wo, g, b, wr, *, batch, seq, alpha):
    d = w_qkv.shape[0]
    qkv = _linear(hb, (w_qkv.astype(BF16),), name="band_qkv")
    o = _band_attention(qkv, _band_bias_table(rel_bias), batch=batch, seq=seq, pairs=d // LANE)
    return _proj_res_ln_route(o, wo.astype(BF16), h, g, b, wr, alpha=alpha, name="band_out")


def _mla_mixer(hb, w_dq, q_g, w_uq, w_dkv, kv_g, w_ukv, *, batch, seq, tm=512):
    n, d = hb.shape
    q_rank = w_dq.shape[1]
    kv_rank = kv_g.shape[-1]
    heads = w_ukv.shape[1] // (2 * CHUNK)
    nope, rope = 64, 2 * ROPE_HALF
    tm = min(tm, seq)
    row = lambda i: (i, 0)
    tab = lambda i: (i % (seq // tm), 0)

    inv_freq = ROPE_THETA ** (-jnp.arange(ROPE_HALF, dtype=F32) / ROPE_HALF)
    ang = jnp.arange(seq).astype(F32)[:, None] * inv_freq[None, :]
    cos, sin = jnp.cos(ang), jnp.sin(ang)
    z = lambda w: jnp.zeros((seq, w), F32)
    ct_q = jnp.concatenate([jnp.ones((seq, nope), F32), cos, cos, z(LANE - nope - rope)], axis=1)
    st_q = jnp.concatenate([z(nope), sin, sin, z(LANE - nope - rope)], axis=1)
    ct_k = jnp.concatenate([cos, cos, z(LANE - rope)], axis=1)
    st_k = jnp.concatenate([sin, sin, z(LANE - rope)], axis=1)

    wq3 = w_uq.reshape(q_rank, heads, nope + rope)
    qn, q1, q2 = wq3[..., :nope], wq3[..., nope:nope + ROPE_HALF], wq3[..., nope + ROPE_HALF:]
    zq = lambda w: jnp.zeros((q_rank, heads, w), w_uq.dtype)
    wa = jnp.concatenate([qn, q1, q2, zq(LANE - nope - rope)], axis=2).reshape(q_rank, heads * LANE)
    wb = jnp.concatenate([zq(nope), -q2, q1, zq(LANE - nope - rope)], axis=2).reshape(q_rank, heads * LANE)

    k1, k2 = w_dkv[:, kv_rank:kv_rank + ROPE_HALF], w_dkv[:, kv_rank + ROPE_HALF:]
    zd = lambda w: jnp.zeros((d, w), w_dkv.dtype)
    wd = jnp.concatenate([w_dkv[:, :kv_rank], k1, k2, zd(LANE - rope), -k2, k1, zd(LANE - rope)], axis=1)

    wkv3 = w_ukv.reshape(kv_rank, heads, nope + CHUNK)
    zk = lambda r, w: jnp.zeros((r, heads, w), w_ukv.dtype)
    k_top = jnp.concatenate([wkv3[..., :nope], zk(kv_rank, LANE - nope)], axis=2).reshape(kv_rank, heads * LANE)
    eye = jnp.broadcast_to(jnp.eye(rope, dtype=w_ukv.dtype)[:, None, :], (rope, heads, rope))
    k_rope = jnp.concatenate([zk(rope, nope), eye, zk(rope, LANE - nope - rope)], axis=2).reshape(rope, heads * LANE)
    v_top = wkv3[..., nope:].reshape(kv_rank, heads * CHUNK)
    wkv = jnp.concatenate([
        jnp.concatenate([k_top, v_top], axis=1),
        jnp.concatenate([k_rope, jnp.zeros((rope, heads * CHUNK), w_ukv.dtype)], axis=1),
        jnp.zeros((LANE - rope, heads * (LANE + CHUNK)), w_ukv.dtype)], axis=0)

    cq = pl.pallas_call(
        _mla_cq_kernel,
        grid=(n // tm,),
        in_specs=[pl.BlockSpec((tm, d), row), _resident((d, q_rank)), _resident((1, q_rank))],
        out_specs=pl.BlockSpec((tm, q_rank), row),
        out_shape=jax.ShapeDtypeStruct((n, q_rank), BF16),
        compiler_params=_cp(1),
        name="mla_cq",
    )(hb, w_dq.astype(BF16), q_g.reshape(1, -1))

    q = pl.pallas_call(
        functools.partial(_mla_q_kernel, scale=float((nope + rope) ** -0.5), heads=heads),
        grid=(n // tm,),
        in_specs=[pl.BlockSpec((tm, q_rank), row), _resident((q_rank, heads * LANE)),
                  _resident((q_rank, heads * LANE)), pl.BlockSpec((tm, LANE), tab), pl.BlockSpec((tm, LANE), tab)],
        out_specs=pl.BlockSpec((tm, heads * LANE), row),
        out_shape=jax.ShapeDtypeStruct((n, heads * LANE), BF16),
        compiler_params=_cp(1),
        name="mla_q",
    )(cq, wa.astype(BF16), wb.astype(BF16), ct_q, st_q)

    ckr = pl.pallas_call(
        functools.partial(_mla_ckv_kernel, rank=kv_rank),
        grid=(n // tm,),
        in_specs=[pl.BlockSpec((tm, d), row), _resident((d, kv_rank + 2 * LANE)), _resident((1, kv_rank)),
                  pl.BlockSpec((tm, LANE), tab), pl.BlockSpec((tm, LANE), tab)],
        out_specs=pl.BlockSpec((tm, kv_rank + LANE), row),
        out_shape=jax.ShapeDtypeStruct((n, kv_rank + LANE), BF16),
        compiler_params=_cp(1),
        name="mla_ckv",
    )(hb, wd.astype(BF16), kv_g.reshape(1, -1), ct_k, st_k)

    kv = _linear(ckr, (wkv.astype(BF16),), name="mla_kv")
    return _flash("mla", q, kv, kv, batch=batch, seq=seq, groups=heads // 2, q_blk0=0, k_blk0=0, v_blk0=heads)


def _sg_mixer(h, hb, w_in, vg, vb, w_s, b_s, w_out, g, b, wr, *, alpha):
    n, d = hb.shape
    width = w_out.shape[0]
    groups = w_s.shape[0]
    tm = min(ROUTE_TILE, n)
    row = lambda i: (i, 0)
    r_specs, r_shapes, r_scratch = _route_outputs(n, tm)
    u, v = pl.pallas_call(
        functools.partial(_sg_in_kernel, nc=min(512, width)),
        grid=(n // tm,),
        in_specs=[pl.BlockSpec((tm, d), row), _resident((d, 2 * width)), _resident((1, width)),
                  _resident((1, width))],
        out_specs=[pl.BlockSpec((tm, width), row), pl.BlockSpec((tm, width), row)],
        out_shape=[jax.ShapeDtypeStruct((n, width), BF16), jax.ShapeDtypeStruct((n, width), BF16)],
        scratch_shapes=[pltpu.VMEM((tm, width), F32)],
        compiler_params=_cp(1),
        name="sg_in",
    )(hb, w_in.astype(BF16), vg.reshape(1, -1), vb.reshape(1, -1))
    bs_full = jnp.repeat(b_s.T.astype(F32), width // groups, axis=1)
    return pl.pallas_call(
        functools.partial(_sg_mix_kernel, alpha=alpha, groups=groups),
        grid=(n // tm,),
        in_specs=[pl.BlockSpec((tm, width), row), pl.BlockSpec((tm, width), row),
                  _resident((groups, GMLP_CHUNK, GMLP_CHUNK)), _resident((GMLP_CHUNK, width)),
                  _resident((width, d)), pl.BlockSpec((tm, d), row), _resident((1, d)), _resident((1, d)),
                  _resident((d, LANE))],
        out_specs=[pl.BlockSpec((tm, d), row)] + r_specs,
        out_shape=[jax.ShapeDtypeStruct((n, d), F32)] + r_shapes,
        scratch_shapes=[pltpu.VMEM((tm, width), BF16)] + r_scratch,
        compiler_params=_cp(1),
        name="sg_mix",
    )(u, v, w_s.astype(F32), bs_full, w_out.astype(BF16), h, g, b, wr)


def kernel(x, ln_mix_g, ln_mix_b, ln_ffn_g, ln_ffn_b, diff_wq, diff_wk, diff_wv, diff_lq1, diff_lk1, diff_lq2, diff_lk2, diff_sub_g, diff_wo, ca_w_qkv, ca_rel_bias, ca_wo, mla_w_dq, mla_q_norm_g, mla_w_uq, mla_w_dkv, mla_kv_norm_g, mla_w_ukv, mla_wo, sg_w_in, sg_v_norm_g, sg_v_norm_b, sg_w_s, sg_b_s, sg_w_out, ffn_w1, ffn_w3, ffn_w2, moe_w_router, moe_w1, moe_w3, moe_w2):
    batch, seq, d = x.shape
    depth = ln_mix_g.shape[0]
    alpha = float((2 * depth) ** 0.25)
    h = x.reshape(batch * seq, d).astype(F32)
    hb = h
    vec = lambda a: a.reshape(1, -1).astype(F32)
    for i in range(depth):
        kind, j = i % 4, i // 4
        mg, mb = vec(ln_mix_g[i]), vec(ln_mix_b[i])
        fg, fb = vec(ln_ffn_g[i]), vec(ln_ffn_b[i])
        if kind in (0, 2):
            if kind == 0:
                a = _diff_mixer(hb, diff_wq[j], diff_wk[j], diff_wv[j], diff_lq1[j], diff_lk1[j], diff_lq2[j],
                                diff_lk2[j], diff_sub_g[j], batch=batch, seq=seq,
                                lambda_init=0.8 - 0.6 * math.exp(-0.3 * i))
                wo = diff_wo[j]
            else:
                a = _mla_mixer(hb, mla_w_dq[j], mla_q_norm_g[j], mla_w_uq[j], mla_w_dkv[j], mla_kv_norm_g[j],
                               mla_w_ukv[j], batch=batch, seq=seq)
                wo = mla_wo[j]
            h, hb = _mix_ffn(a, wo.astype(BF16), mg, mb, ffn_w1[i // 2].astype(BF16), ffn_w3[i // 2].astype(BF16),
                             ffn_w2[i // 2].astype(BF16), h, fg, fb, alpha=alpha)
        else:
            wr = _router_weights(moe_w_router[i // 2])
            if kind == 1:
                h, info, cnt = _band_mixer(h, hb, ca_w_qkv[j], ca_rel_bias[j], ca_wo[j], mg, mb, wr, batch=batch,
                                           seq=seq, alpha=alpha)
            else:
                h, info, cnt = _sg_mixer(h, hb, sg_w_in[j], sg_v_norm_g[j], sg_v_norm_b[j], sg_w_s[j], sg_b_s[j],
                                         sg_w_out[j], mg, mb, wr, alpha=alpha)
            h, hb = _moe(h, info, cnt, moe_w1, moe_w3, moe_w2, fg, fb, layer=i // 2, alpha=alpha)
    return h.reshape(batch, seq, d).astype(x.dtype)
```

```python
import functools
import math

import jax
import jax.numpy as jnp
import numpy as np
from jax import lax
from jax.experimental import pallas as pl
from jax.experimental.pallas import tpu as pltpu

F32 = jnp.float32
BF16 = jnp.bfloat16
I32 = jnp.int32

LANE = 128
VMEM_LIMIT = 56 * 1024 * 1024

CHUNK = 64
N_EXPERTS = 8
LN_EPS = 1e-5
RMS_EPS = 1e-6
NEG_INF = -1e30
ROPE_THETA = 10000.0
CA_LEFT = 8
CA_REL_CLIP = 128
ROPE_HALF = 16


def _cp(n_axes, vmem=VMEM_LIMIT):
    return pltpu.CompilerParams(dimension_semantics=("arbitrary",) * n_axes, vmem_limit_bytes=vmem)


def _resident(shape):
    nd = len(shape)
    return pl.BlockSpec(shape, lambda *_: (0,) * nd, pipeline_mode=pl.Buffered(1))


def _res_ln(h, m, g, b, alpha):
    z = alpha * h + m
    mu = jnp.mean(z, axis=-1, keepdims=True)
    zc = z - mu
    var = jnp.mean(zc * zc, axis=-1, keepdims=True)
    return zc * lax.rsqrt(var + LN_EPS) * g + b


def _rms(x, g, eps):
    ms = jnp.mean(x * x, axis=-1, keepdims=True)
    return x * lax.rsqrt(ms + eps) * g


def _linear_kernel(x_ref, *refs):
    *w_refs, o_ref = refs
    x = x_ref[...].astype(BF16)
    col = 0
    for w_ref in w_refs:
        n = w_ref.shape[1]
        o_ref[:, col:col + n] = jnp.dot(x, w_ref[...], preferred_element_type=F32).astype(o_ref.dtype)
        col += n


def _linear(x, ws, *, tm=512, out_dtype=BF16, name="linear"):
    m, k = x.shape
    n = sum(w.shape[1] for w in ws)
    tm = min(tm, m)
    return pl.pallas_call(
        _linear_kernel,
        grid=(m // tm,),
        in_specs=[pl.BlockSpec((tm, k), lambda i: (i, 0))] + [_resident(w.shape) for w in ws],
        out_specs=pl.BlockSpec((tm, n), lambda i: (i, 0)),
        out_shape=jax.ShapeDtypeStruct((m, n), out_dtype),
        compiler_params=_cp(1),
        name=name,
    )(x, *ws)


ROUTE_TILE = 512


def _route_outputs(m, tm):
    specs = [pl.BlockSpec((tm, LANE), lambda i: (i, 0)), pl.BlockSpec((1, LANE), lambda i: (0, 0))]
    shapes = [jax.ShapeDtypeStruct((m, LANE), F32), jax.ShapeDtypeStruct((1, LANE), F32)]
    scratch = [pltpu.VMEM((1, LANE), F32), pltpu.VMEM((tm, tm), BF16)]
    return specs, shapes, scratch


def _proj_route_kernel(a_ref, w_ref, h_ref, g_ref, b_ref, wr_ref, oh_ref, info_ref, cnt_ref, run_ref, before_ref, *,
                       alpha):
    pl.when(pl.program_id(0) == 0)(lambda: _route_init(run_ref, before_ref))
    m = jnp.dot(a_ref[...], w_ref[...], preferred_element_type=F32)
    y = _res_ln(h_ref[...], m, g_ref[...], b_ref[...], alpha)
    oh_ref[...] = y
    _route(y, wr_ref, run_ref, before_ref, info_ref, cnt_ref)


def _proj_res_ln_route(a, w, h, g, b, wr, *, alpha, name):
    m, k = a.shape
    d = w.shape[1]
    tm = min(ROUTE_TILE, m)
    row = lambda i: (i, 0)
    r_specs, r_shapes, r_scratch = _route_outputs(m, tm)
    return pl.pallas_call(
        functools.partial(_proj_route_kernel, alpha=alpha),
        grid=(m // tm,),
        in_specs=[pl.BlockSpec((tm, k), row), _resident((k, d)), pl.BlockSpec((tm, d), row),
                  _resident((1, d)), _resident((1, d)), _resident((d, LANE))],
        out_specs=[pl.BlockSpec((tm, d), row)] + r_specs,
        out_shape=[jax.ShapeDtypeStruct((m, d), F32)] + r_shapes,
        scratch_shapes=r_scratch,
        compiler_params=_cp(1),
        name=name,
    )(a, w, h, g, b, wr)


def _mix_ffn_kernel(a_ref, wo_ref, mg_ref, mb_ref, w1_ref, w3_ref, w2_ref, h_ref, g_ref, b_ref, oh_ref, ohb_ref,
                    hm_ref, xb_ref, acc_ref, *, alpha, fc):
    mix = jnp.dot(a_ref[...], wo_ref[...], preferred_element_type=F32)
    hm = _res_ln(h_ref[...], mix, mg_ref[...], mb_ref[...], alpha)
    hm_ref[...] = hm
    xb_ref[...] = hm.astype(BF16)
    x = xb_ref[...]
    f = w1_ref.shape[1]
    for c in range(f // fc):
        sl = slice(c * fc, (c + 1) * fc)
        a = jnp.dot(x, w1_ref[:, sl], preferred_element_type=F32)
        b3 = jnp.dot(x, w3_ref[:, sl], preferred_element_type=F32)
        gated = (jax.nn.silu(a) * b3).astype(BF16)
        part = jnp.dot(gated, w2_ref[sl, :], preferred_element_type=F32)
        if c == 0:
            acc_ref[...] = part
        else:
            acc_ref[...] += part
    y = _res_ln(hm_ref[...], acc_ref[...], g_ref[...], b_ref[...], alpha)
    oh_ref[...] = y
    ohb_ref[...] = y.astype(BF16)


def _mix_ffn(a, wo, mg, mb, w1, w3, w2, h, g, b, *, alpha, tm=512, fc=256):
    m, d = h.shape
    k = a.shape[1]
    f = w1.shape[1]
    tm = min(tm, m)
    row = lambda i: (i, 0)
    vecs = [_resident((1, d)), _resident((1, d))]
    return pl.pallas_call(
        functools.partial(_mix_ffn_kernel, alpha=alpha, fc=fc),
        grid=(m // tm,),
        in_specs=[pl.BlockSpec((tm, k), row), _resident((k, d))] + vecs
                 + [_resident((d, f)), _resident((d, f)), _resident((f, d)), pl.BlockSpec((tm, d), row)] + vecs,
        out_specs=[pl.BlockSpec((tm, d), row), pl.BlockSpec((tm, d), row)],
        out_shape=[jax.ShapeDtypeStruct((m, d), F32), jax.ShapeDtypeStruct((m, d), BF16)],
        scratch_shapes=[pltpu.VMEM((tm, d), F32), pltpu.VMEM((tm, d), BF16), pltpu.VMEM((tm, d), F32)],
        compiler_params=_cp(1),
        name="mix_ffn",
    )(a, wo, mg, mb, w1, w3, w2, h, g, b)


ONES_ROWS = 16
ACC_ROWS = LANE + ONES_ROWS


def _vt_ext(v_blk):
    vt = v_blk.astype(F32).T.astype(BF16)
    return jnp.concatenate([vt, jnp.ones((ONES_ROWS, v_blk.shape[0]), BF16)], axis=0)


def _split_heads(x):
    low = lax.broadcasted_iota(I32, x.shape, 1) < CHUNK
    zero = jnp.zeros_like(x)
    return jnp.where(low, x, zero), jnp.where(low, zero, x)


_NT = (((1,), (1,)), ((), ()))


def _flash_kernel(*refs, mode, tq, gs, lambda_init):
    if mode == "diff":
        lq1_ref, lk1_ref, lq2_ref, lk2_ref, subg_ref, q_ref, k_ref, v_ref, o_ref, *scratch = refs
    else:
        q_ref, k_ref, v_ref, o_ref, *scratch = refs
    vt_ref, m_ref, a_ref, bm_ref, acc_ref, s_ref, p_ref = scratch
    qi = pl.program_id(2)
    tk = tq // 2
    nmap = 2 * gs
    lanes = lambda i: slice(i * LANE, (i + 1) * LANE)

    @pl.when(qi == 0)
    def _():
        for j in range(vt_ref.shape[0]):
            for g in range(gs):
                vt_ref[j, g] = _vt_ext(v_ref[j * tk:(j + 1) * tk, lanes(g)])

    qs = []
    for g in range(gs):
        if mode == "diff":
            qs += list(_split_heads(q_ref[:, lanes(g)] * jnp.asarray(CHUNK ** -0.5, BF16)))
        else:
            qs += [q_ref[:, lanes(2 * g)], q_ref[:, lanes(2 * g + 1)]]
    m_ref[...] = jnp.full(m_ref.shape, NEG_INF, F32)
    acc_ref[...] = jnp.zeros(acc_ref.shape, F32)

    def scores(blk):
        start = pl.multiple_of(blk * tk, tk)
        kb = k_ref[pl.ds(start, tk), :]
        kks = [kb[:, lanes(mp // 2 if mode == "diff" else mp)] for mp in range(nmap)]
        return [lax.dot_general(kks[mp], qs[mp], _NT, preferred_element_type=F32) for mp in range(nmap)]

    def values(blk, slot):
        return [jnp.dot(vt_ref[blk, mp // 2], p_ref[slot, mp], preferred_element_type=F32) for mp in range(nmap)]

    def stash(slot, ss):
        for mp in range(nmap):
            s_ref[slot, mp] = ss[mp]
            bm_ref[slot, mp] = jnp.max(ss[mp], axis=0, keepdims=True)

    def softmax(slot, masked):
        for mp in range(nmap):
            s = s_ref[slot, mp]
            if masked:
                kc = (lax.broadcasted_iota(I32, s.shape, 0) + slot * tk) >> 6
                qc = lax.broadcasted_iota(I32, s.shape, 1) >> 6
                s = jnp.where(kc <= qc, s, NEG_INF)
                blk_max = jnp.max(s, axis=0, keepdims=True)
            else:
                blk_max = bm_ref[slot, mp]
            m_old = m_ref[mp]
            m_new = jnp.maximum(m_old, blk_max)
            p_ref[slot, mp] = jnp.exp(s - m_new).astype(BF16)
            a_ref[slot, mp] = jnp.exp(m_old - m_new)
            m_ref[mp] = m_new

    def fold(pv, scale):
        for mp in range(nmap):
            acc_ref[mp] = scale[mp] * acc_ref[mp] + pv[mp]

    def trip(blk, masked, produce):
        scale = [[a_ref[u, mp] for mp in range(nmap)] for u in range(2)]
        pvs = [values(jnp.maximum(blk - 2 + u, 0), u) for u in range(2)]
        nxt = [scores(blk + 2 + u) for u in range(2)] if produce else []
        for u in range(2):
            softmax(u, masked)
        for u in range(2):
            fold(pvs[u], scale[u])
        for u, s in enumerate(nxt):
            stash(u, s)

    for u in range(2):
        stash(u, scores(u))
    p_ref[...] = jnp.zeros(p_ref.shape, BF16)
    a_ref[...] = jnp.ones(a_ref.shape, F32)

    def step(j, carry):
        trip(2 * j, False, True)
        return carry

    lax.fori_loop(0, qi, step, 0)
    trip(2 * qi, True, False)
    for u in range(2):
        fold(values(2 * qi + u, u), [a_ref[u, mp] for mp in range(nmap)])

    for g in range(gs):
        acc_a, acc_b = acc_ref[2 * g], acc_ref[2 * g + 1]
        oa = acc_a[:LANE] / acc_a[LANE:LANE + 1]
        ob = acc_b[:LANE] / acc_b[LANE:LANE + 1]
        if mode == "diff":
            lam = (jnp.exp(jnp.sum(lq1_ref[...] * lk1_ref[...], axis=-1, keepdims=True))
                   - jnp.exp(jnp.sum(lq2_ref[...] * lk2_ref[...], axis=-1, keepdims=True)) + lambda_init)
            o = (oa - lam * ob).T
            o = _rms(o, subg_ref[...], 1e-5) * (1.0 - lambda_init)
        else:
            row = lax.broadcasted_iota(I32, oa.shape, 0)
            o = jnp.where(row < CHUNK, oa, ob).T
        o_ref[:, lanes(g)] = o.astype(o_ref.dtype)


def _flash(mode, q_arr, k_arr, v_arr, *, batch, seq, groups, q_blk0, k_blk0, v_blk0, extra=(), lambda_init=0.0,
           tq=512, gs=2):
    qw = (LANE if mode == "diff" else 2 * LANE) * gs
    tq = min(tq, seq)
    tk = tq // 2
    nq = seq // tq
    n = batch * seq
    nmap = 2 * gs
    assert groups % gs == 0 and q_blk0 % gs == 0 and k_blk0 % gs == 0 and v_blk0 % gs == 0
    qb, kb, vb = q_blk0 // gs, k_blk0 // gs, v_blk0 // gs
    in_specs = [pl.BlockSpec(e.shape, lambda b, g, i: (0, 0)) for e in extra]
    in_specs += [
        pl.BlockSpec((tq, qw), lambda b, g, i: (b * nq + i, qb + g)),
        pl.BlockSpec((seq, qw), lambda b, g, i: (b, kb + g)),
        pl.BlockSpec((seq, gs * LANE), lambda b, g, i: (b, vb + g)),
    ]
    return pl.pallas_call(
        functools.partial(_flash_kernel, mode=mode, tq=tq, gs=gs, lambda_init=lambda_init),
        grid=(batch, groups // gs, nq),
        in_specs=in_specs,
        out_specs=pl.BlockSpec((tq, gs * LANE), lambda b, g, i: (b * nq + i, g)),
        out_shape=jax.ShapeDtypeStruct((n, groups * LANE), BF16),
        scratch_shapes=[pltpu.VMEM((seq // tk, gs, ACC_ROWS, tk), BF16), pltpu.VMEM((nmap, 1, tq), F32),
                        pltpu.VMEM((2, nmap, 1, tq), F32), pltpu.VMEM((2, nmap, 1, tq), F32),
                        pltpu.VMEM((nmap, ACC_ROWS, tq), F32),
                        pltpu.VMEM((2, nmap, tk, tq), F32), pltpu.VMEM((2, nmap, tk, tq), BF16)],
        compiler_params=_cp(3),
        name="flash_" + mode,
    )(*extra, q_arr, k_arr, v_arr)


BAND_TQ = 4 * CHUNK
BAND_W = (CA_LEFT + 4) * CHUNK
BAND_PAD = CA_LEFT * CHUNK
BAND_NW = BAND_W // BAND_TQ
BAND_EXT = 1024


def _band_kernel(q_ref, k_ref, v_ref, ext_ref, o_ref, kpad_ref, vt_ref, s_ref, bm_ref, bias_ref, *, seq, gp):
    t = pl.program_id(2)
    nt = pl.num_programs(2)
    npad = BAND_PAD // BAND_TQ

    nh = 2 * gp
    lanes = lambda i: slice(i * LANE, (i + 1) * LANE)

    def produce(tile, slot, front):
        qr = q_ref[pl.ds(pl.multiple_of(tile * BAND_TQ, BAND_TQ), BAND_TQ), :]
        qhs = []
        for g in range(gp):
            qhs += list(_split_heads(qr[:, lanes(g)] * jnp.asarray(CHUNK ** -0.5, BF16)))
        tops = [None] * nh
        for u in range(BAND_NW):
            start = pl.multiple_of((tile + u) * BAND_TQ, BAND_TQ)
            kb = kpad_ref[pl.ds(start, BAND_TQ), :]
            for hh in range(nh):
                s = lax.dot_general(kb[:, lanes(hh // 2)], qhs[hh], _NT, preferred_element_type=F32)
                s = s + bias_ref[hh, u * BAND_TQ:(u + 1) * BAND_TQ, :]
                if front:
                    invalid = lax.broadcasted_iota(I32, s.shape, 0) + (tile + u) * BAND_TQ < BAND_PAD
                    s = jnp.where(invalid, NEG_INF, s)
                s_ref[slot, hh, u] = s
                top = jnp.max(s, axis=0, keepdims=True)
                tops[hh] = top if tops[hh] is None else jnp.maximum(tops[hh], top)
        for hh in range(nh):
            bm_ref[slot, hh] = tops[hh]

    @pl.when((t == 0) & (pl.program_id(1) == 0))
    def _():
        ci = lax.broadcasted_iota(I32, (BAND_W, BAND_TQ), 1) >> 6
        cj = lax.broadcasted_iota(I32, (BAND_W, BAND_TQ), 0) >> 6
        vis = (cj >= ci) & (cj <= ci + CA_LEFT)
        for hh in range(nh):
            rows = jnp.broadcast_to(ext_ref[hh // 2, hh % 2], (BAND_W, BAND_EXT))
            skew = pltpu.roll(rows, BAND_EXT - (BAND_W - 1), 1, stride=1, stride_axis=0)
            bias_ref[hh] = jnp.where(vis, skew[:, :BAND_TQ], NEG_INF)

    @pl.when(t == 0)
    def _():
        kpad_ref[0:BAND_PAD, :] = jnp.zeros((BAND_PAD, gp * LANE), BF16)
        kpad_ref[BAND_PAD:BAND_PAD + seq, :] = k_ref[...]
        for j in range(npad):
            vt_ref[j] = jnp.zeros((gp, ACC_ROWS, BAND_TQ), BF16)
        for j in range(seq // BAND_TQ):
            for g in range(gp):
                vt_ref[npad + j, g] = _vt_ext(v_ref[j * BAND_TQ:(j + 1) * BAND_TQ, lanes(g)])
        produce(0, 0, True)

    cur = t & 1
    nxt_tile = jnp.minimum(t + 1, nt - 1)

    def consume():
        outs = []
        for hh in range(nh):
            m = bm_ref[cur, hh]
            acc = None
            for u in range(BAND_NW):
                p = jnp.exp(s_ref[cur, hh, u] - m).astype(BF16)
                pv = jnp.dot(vt_ref[t + u, hh // 2], p, preferred_element_type=F32)
                acc = pv if acc is None else acc + pv
            outs.append(acc[:LANE] / acc[LANE:LANE + 1])
        row = lax.broadcasted_iota(I32, outs[0].shape, 0)
        rows = pl.ds(pl.multiple_of(t * BAND_TQ, BAND_TQ), BAND_TQ)
        for g in range(gp):
            o_ref[rows, lanes(g)] = jnp.where(row < CHUNK, outs[2 * g], outs[2 * g + 1]).T.astype(o_ref.dtype)

    def run(front):
        consume()
        produce(nxt_tile, 1 - cur, front)

    pl.when(t + 1 < npad)(lambda: run(True))
    pl.when(t + 1 >= npad)(lambda: run(False))


def _band_attention(qkv, ext, *, batch, seq, pairs, gp=4):
    n = batch * seq
    nt = seq // BAND_TQ
    assert pairs % gp == 0
    steps, w, nh = pairs // gp, gp * LANE, 2 * gp
    return pl.pallas_call(
        functools.partial(_band_kernel, seq=seq, gp=gp),
        grid=(steps, batch, nt),
        in_specs=[
            pl.BlockSpec((seq, w), lambda g, b, t: (b, g)),
            pl.BlockSpec((seq, w), lambda g, b, t: (b, steps + g)),
            pl.BlockSpec((seq, w), lambda g, b, t: (b, 2 * steps + g)),
            pl.BlockSpec((gp, 2, 1, BAND_EXT), lambda g, b, t: (g, 0, 0, 0)),
        ],
        out_specs=pl.BlockSpec((seq, w), lambda g, b, t: (b, g)),
        out_shape=jax.ShapeDtypeStruct((n, pairs * LANE), BF16),
        scratch_shapes=[pltpu.VMEM((seq + BAND_PAD, w), BF16),
                        pltpu.VMEM(((seq + BAND_PAD) // BAND_TQ, gp, ACC_ROWS, BAND_TQ), BF16),
                        pltpu.VMEM((2, nh, BAND_NW, BAND_TQ, BAND_TQ), F32),
                        pltpu.VMEM((2, nh, 1, BAND_TQ), F32),
                        pltpu.VMEM((nh, BAND_W, BAND_TQ), F32)],
        compiler_params=_cp(3),
        name="band_attention",
    )(qkv, qkv, qkv, ext)


def _band_bias_table(rel_bias):
    heads = rel_bias.shape[0]
    rel = np.minimum(np.arange(BAND_EXT), BAND_TQ + BAND_W - 2) - (BAND_W - 1) + BAND_PAD
    ext = rel_bias.astype(F32)[:, np.clip(rel, -CA_REL_CLIP, CA_REL_CLIP) + CA_REL_CLIP]
    return ext.reshape(heads // 2, 2, 1, BAND_EXT)


def _mla_cq_kernel(x_ref, w_ref, g_ref, o_ref):
    c = jnp.dot(x_ref[...], w_ref[...], preferred_element_type=F32)
    o_ref[...] = _rms(c, g_ref[...], RMS_EPS).astype(o_ref.dtype)


def _mla_q_kernel(c_ref, wa_ref, wb_ref, ct_ref, st_ref, o_ref, *, scale, heads):
    c = c_ref[...]
    a = jnp.dot(c, wa_ref[...], preferred_element_type=F32)
    b = jnp.dot(c, wb_ref[...], preferred_element_type=F32)
    ct = ct_ref[...] * scale
    st = st_ref[...] * scale
    for h in range(heads):
        sl = slice(h * LANE, (h + 1) * LANE)
        o_ref[:, sl] = (a[:, sl] * ct + b[:, sl] * st).astype(o_ref.dtype)


def _mla_ckv_kernel(x_ref, w_ref, g_ref, ct_ref, st_ref, o_ref, *, rank):
    y = jnp.dot(x_ref[...], w_ref[...], preferred_element_type=F32)
    o_ref[:, :rank] = _rms(y[:, :rank], g_ref[...], RMS_EPS).astype(o_ref.dtype)
    kr = y[:, rank:rank + LANE] * ct_ref[...] + y[:, rank + LANE:rank + 2 * LANE] * st_ref[...]
    o_ref[:, rank:] = kr.astype(o_ref.dtype)


GMLP_CHUNK = 128


def _sg_in_kernel(x_ref, w_ref, g_ref, b_ref, u_ref, v_ref, vbuf_ref, *, nc):
    width = u_ref.shape[1]
    x = x_ref[...]
    for c in range(0, 2 * width, nc):
        hc = jax.nn.gelu(jnp.dot(x, w_ref[:, c:c + nc], preferred_element_type=F32))
        if c < width:
            u_ref[:, c:c + nc] = hc.astype(u_ref.dtype)
        else:
            vbuf_ref[:, c - width:c - width + nc] = hc
    v = vbuf_ref[...]
    mu = jnp.mean(v, axis=-1, keepdims=True)
    vc = v - mu
    var = jnp.mean(vc * vc, axis=-1, keepdims=True)
    v_ref[...] = (vc * lax.rsqrt(var + LN_EPS) * g_ref[...] + b_ref[...]).astype(v_ref.dtype)


def _sg_mix_kernel(u_ref, v_ref, ws_ref, bs_ref, wo_ref, h_ref, g_ref, b_ref, wr_ref, oh_ref, info_ref, cnt_ref,
                   gated_ref, run_ref, before_ref, *, alpha, groups):
    tm = u_ref.shape[0]
    pl.when(pl.program_id(0) == 0)(lambda: _route_init(run_ref, before_ref))
    r = lax.broadcasted_iota(I32, (GMLP_CHUNK, GMLP_CHUNK), 0) >> 6
    c = lax.broadcasted_iota(I32, (GMLP_CHUNK, GMLP_CHUNK), 1) >> 6
    vis = c <= r
    for gi in range(groups):
        w = jnp.where(vis, ws_ref[gi], 0.0).astype(BF16)
        cs = slice(gi * LANE, (gi + 1) * LANE)
        for ch in range(tm // GMLP_CHUNK):
            rs = slice(ch * GMLP_CHUNK, (ch + 1) * GMLP_CHUNK)
            mixed = jnp.dot(w, v_ref[rs, cs], preferred_element_type=F32) + bs_ref[:, cs]
            gated_ref[rs, cs] = (u_ref[rs, cs].astype(F32) * mixed).astype(BF16)
    m = jnp.dot(gated_ref[...], wo_ref[...], preferred_element_type=F32)
    y = _res_ln(h_ref[...], m, g_ref[...], b_ref[...], alpha)
    oh_ref[...] = y
    _route(y, wr_ref, run_ref, before_ref, info_ref, cnt_ref)


INFO_IDX, INFO_RANK, INFO_GATE = 0, 2, 4
SUBLANES = 8


def _route_init(run_ref, before_ref):
    tr = before_ref.shape[0]
    run_ref[...] = jnp.zeros_like(run_ref)
    rr = lax.broadcasted_iota(I32, (tr, tr), 0)
    cc = lax.broadcasted_iota(I32, (tr, tr), 1)
    before_ref[...] = jnp.where(cc < rr, 1.0, 0.0).astype(BF16)


def _route(x, w_ref, run_ref, before_ref, info_ref, cnt_ref):
    tr = x.shape[0]
    x1 = x.astype(BF16)
    r1 = x - x1.astype(F32)
    x2 = r1.astype(BF16)
    x3 = (r1 - x2.astype(F32)).astype(BF16)
    w = w_ref[...]
    lane = lax.broadcasted_iota(I32, (tr, LANE), 1)
    t = (jnp.dot(x1, w, preferred_element_type=F32)
         + jnp.where(lane < 2 * N_EXPERTS, jnp.dot(x2, w, preferred_element_type=F32), 0.0)
         + jnp.where(lane < N_EXPERTS, jnp.dot(x3, w, preferred_element_type=F32), 0.0))
    logits = t + pltpu.roll(t, LANE - N_EXPERTS, 1) + pltpu.roll(t, LANE - 2 * N_EXPERTS, 1)
    big = jnp.float32(3e38)
    logits = jnp.where(lane < N_EXPERTS, logits, -big)

    lane_f = lane.astype(F32)
    v1 = jnp.max(logits, axis=-1, keepdims=True)
    i1 = jnp.min(jnp.where(logits == v1, lane_f, float(LANE)), axis=-1, keepdims=True)
    rest = jnp.where(lane_f == i1, -big, logits)
    v2 = jnp.max(rest, axis=-1, keepdims=True)
    i2 = jnp.min(jnp.where(rest == v2, lane_f, float(LANE)), axis=-1, keepdims=True)
    e = jnp.exp(v2 - v1)
    g1 = 1.0 / (1.0 + e)
    g2 = e / (1.0 + e)

    oh1 = jnp.where(lane_f == i1, 1.0, 0.0)
    oh2 = jnp.where(lane_f == i2, 1.0, 0.0)
    oh = oh1 + oh2
    prior = jnp.dot(before_ref[...], oh.astype(BF16), preferred_element_type=F32) + run_ref[...]
    rank1 = jnp.sum(prior * oh1, axis=-1, keepdims=True)
    rank2 = jnp.sum(prior * oh2, axis=-1, keepdims=True)
    run_ref[...] += jnp.sum(oh, axis=0, keepdims=True)
    cnt_ref[...] = run_ref[...]

    info = jnp.where(lane == INFO_IDX, i1, 0.0)
    info = jnp.where(lane == INFO_IDX + 1, i2, info)
    info = jnp.where(lane == INFO_RANK, rank1, info)
    info = jnp.where(lane == INFO_RANK + 1, rank2, info)
    info = jnp.where(lane == INFO_GATE, g1, info)
    info = jnp.where(lane == INFO_GATE + 1, g2, info)
    info_ref[...] = info


def _dispatch_kernel(zoff_ref, pos_ref, h_ref, xs_ref, zero_ref, sem, zsem):
    td = h_ref.shape[0]

    @pl.when(pl.program_id(0) == 0)
    def _():
        tile = zero_ref.shape[0]
        zero_ref[...] = jnp.zeros_like(zero_ref)

        def fill(k):
            start = pl.multiple_of(jnp.maximum(zoff_ref[k], 0), tile)
            return pltpu.make_async_copy(zero_ref, xs_ref.at[pl.ds(start, tile), :], zsem)

        for k in range(zoff_ref.shape[0]):
            pl.when(zoff_ref[k] >= 0)(lambda k=k: fill(k).start())
        for k in range(zoff_ref.shape[0]):
            pl.when(zoff_ref[k] >= 0)(lambda k=k: fill(k).wait())

    def start(g, carry):
        base = pl.multiple_of(g * SUBLANES, SUBLANES)
        for s in range(SUBLANES):
            for k in range(2):
                p = pos_ref[0, 0, 2 * (base + s) + k]
                pltpu.make_async_copy(h_ref.at[pl.ds(base + s, 1), :], xs_ref.at[pl.ds(p, 1), :],
                                      sem).start(priority=k)
        return carry

    lax.fori_loop(0, td // SUBLANES, start, 0)
    for _ in range(2):
        pltpu.make_async_copy(h_ref, xs_ref.at[pl.ds(0, td), :], sem).wait()


def _experts_kernel(te_ref, nu_ref, x_ref, w1_ref, w3_ref, w2_ref, o_ref, xb_ref, *, fc):
    i = pl.program_id(0)
    j = pl.program_id(1)
    del te_ref

    @pl.when(j == 0)
    def _():
        o_ref[...] = jnp.zeros_like(o_ref)
        xb_ref[...] = x_ref[...].astype(BF16)

    @pl.when(i < nu_ref[0])
    def _():
        x = xb_ref[...]
        tf = w1_ref.shape[1]
        gated = []
        for c in range(tf // fc):
            sl = slice(c * fc, (c + 1) * fc)
            a = jnp.dot(x, w1_ref[:, sl].astype(BF16), preferred_element_type=F32)
            b3 = jnp.dot(x, w3_ref[:, sl].astype(BF16), preferred_element_type=F32)
            gated.append((jax.nn.silu(a) * b3).astype(BF16))
        o_ref[...] += jnp.dot(jnp.concatenate(gated, axis=1), w2_ref[...].astype(BF16),
                              preferred_element_type=F32)


def _combine_kernel(pos_ref, posn_ref, info_ref, h_ref, g_ref, b_ref, ys_ref, oh_ref, ohb_ref, buf_ref, sem, *,
                    alpha):
    i = pl.program_id(0)
    tc = h_ref.shape[0]

    def gather(p_ref, slot):
        def start(g, carry):
            base = pl.multiple_of(g * SUBLANES, SUBLANES)
            for s in range(SUBLANES):
                for k in range(2):
                    p = p_ref[0, 0, 2 * (base + s) + k]
                    pltpu.make_async_copy(ys_ref.at[pl.ds(p, 1), :], buf_ref.at[slot, k, pl.ds(base + s, 1), :],
                                          sem.at[slot]).start(priority=k)
            return carry

        lax.fori_loop(0, tc // SUBLANES, start, 0)

    @pl.when(i == 0)
    def _():
        gather(pos_ref, 0)

    @pl.when(i + 1 < pl.num_programs(0))
    def _():
        gather(posn_ref, (i + 1) & 1)

    cur = i & 1
    for k in range(2):
        pltpu.make_async_copy(ys_ref.at[pl.ds(0, tc), :], buf_ref.at[cur, k], sem.at[cur]).wait()
    info = info_ref[...]
    g1 = info[:, INFO_GATE:INFO_GATE + 1]
    g2 = info[:, INFO_GATE + 1:INFO_GATE + 2]
    m = g1 * buf_ref[cur, 0] + g2 * buf_ref[cur, 1]
    y = _res_ln(h_ref[...], m, g_ref[...], b_ref[...], alpha)
    oh_ref[...] = y
    ohb_ref[...] = y.astype(BF16)


def _split3(w):
    hi = w.astype(BF16)
    r = w - hi.astype(F32)
    mid = r.astype(BF16)
    lo = (r - mid.astype(F32)).astype(BF16)
    return hi, mid, lo


def _router_weights(w_router):
    hi, mid, lo = _split3(w_router.astype(F32))
    pad = jnp.zeros((w_router.shape[0], LANE - 3 * N_EXPERTS), BF16)
    return jnp.concatenate([hi, mid, lo, pad], axis=1)


def _moe(h, info, cnt, w1, w3, w2, g, b, *, layer, alpha, tm=1024, tf=512, fc=256, td=512):
    n, d = h.shape
    f = w1.shape[3]
    tm = min(tm, n)
    td = min(td, n)
    tf = min(tf, f)
    row = lambda i: (i, 0)

    idx = info[:, INFO_IDX:INFO_IDX + 2].astype(I32)
    rank = info[:, INFO_RANK:INFO_RANK + 2].astype(I32)
    counts = cnt[0, :N_EXPERTS].astype(I32)
    padded = ((counts + tm - 1) // tm) * tm
    ends = jnp.cumsum(padded)
    starts = ends - padded
    pos = starts[idx] + rank
    n_tiles = (2 * n) // tm + N_EXPERTS
    n_used = (ends[-1] // tm).astype(I32)
    tile_start = jnp.arange(n_tiles, dtype=I32) * tm
    tile_e = jnp.sum(tile_start[:, None] >= ends[None, :], axis=1).astype(I32)
    tile_e = jnp.minimum(tile_e, tile_e[jnp.maximum(n_used - 1, 0)])
    rows = n_tiles * tm
    pos_blocks = pos.reshape(n // td, 1, 2 * td)

    group_fill = jnp.where(padded > 0, ends - tm, -1)
    tail = n_used + jnp.arange(N_EXPERTS, dtype=I32)
    tail_fill = jnp.where(tail < n_tiles, tail * tm, -1)
    zero_tiles = jnp.concatenate([group_fill, tail_fill]).astype(I32)
    xs = pl.pallas_call(
        _dispatch_kernel,
        grid_spec=pltpu.PrefetchScalarGridSpec(
            num_scalar_prefetch=1,
            grid=(n // td,),
            in_specs=[pl.BlockSpec((1, 1, 2 * td), lambda i, z: (i, 0, 0), memory_space=pltpu.SMEM),
                      pl.BlockSpec((td, d), lambda i, z: (i, 0))],
            out_specs=pl.BlockSpec(memory_space=pl.ANY),
            scratch_shapes=[pltpu.VMEM((tm, d), F32), pltpu.SemaphoreType.DMA(()), pltpu.SemaphoreType.DMA(())],
        ),
        out_shape=jax.ShapeDtypeStruct((rows, d), F32),
        compiler_params=_cp(1),
        name="moe_dispatch",
    )(zero_tiles, pos_blocks, h)

    nj = f // tf

    def x_map(i, j, te, nu):
        return (jnp.minimum(i, nu[0] - 1), 0)

    def w13_map(i, j, te, nu):
        return (layer, te[i], 0, jnp.where(i < nu[0], j, nj - 1))

    def w2_map(i, j, te, nu):
        return (layer, te[i], jnp.where(i < nu[0], j, nj - 1), 0)

    ys = pl.pallas_call(
        functools.partial(_experts_kernel, fc=fc),
        grid_spec=pltpu.PrefetchScalarGridSpec(
            num_scalar_prefetch=2,
            grid=(n_tiles, nj),
            in_specs=[pl.BlockSpec((tm, d), x_map),
                      pl.BlockSpec((None, None, d, tf), w13_map),
                      pl.BlockSpec((None, None, d, tf), w13_map),
                      pl.BlockSpec((None, None, tf, d), w2_map)],
            out_specs=pl.BlockSpec((tm, d), lambda i, j, te, nu: (i, 0)),
            scratch_shapes=[pltpu.VMEM((tm, d), BF16)],
        ),
        out_shape=jax.ShapeDtypeStruct((rows, d), F32),
        compiler_params=_cp(2),
        name="moe_experts",
    )(tile_e, n_used.reshape(1), xs, w1, w3, w2)

    tc = td
    last = n // tc - 1
    return pl.pallas_call(
        functools.partial(_combine_kernel, alpha=alpha),
        grid=(n // tc,),
        in_specs=[pl.BlockSpec((1, 1, 2 * tc), lambda i: (i, 0, 0), memory_space=pltpu.SMEM),
                  pl.BlockSpec((1, 1, 2 * tc), lambda i: (jnp.minimum(i + 1, last), 0, 0), memory_space=pltpu.SMEM),
                  pl.BlockSpec((tc, LANE), row),
                  pl.BlockSpec((tc, d), row),
                  _resident((1, d)), _resident((1, d)),
                  pl.BlockSpec(memory_space=pl.ANY)],
        out_specs=[pl.BlockSpec((tc, d), row), pl.BlockSpec((tc, d), row)],
        out_shape=[jax.ShapeDtypeStruct((n, d), F32), jax.ShapeDtypeStruct((n, d), BF16)],
        scratch_shapes=[pltpu.VMEM((2, 2, tc, d), F32), pltpu.SemaphoreType.DMA((2,))],
        compiler_params=_cp(1),
        name="moe_combine",
    )(pos_blocks, pos_blocks, info, h, g, b, ys)


def _diff_mixer(hb, wq, wk, wv, lq1, lk1, lq2, lk2, sub_g, *, batch, seq, lambda_init):
    d = wq.shape[0]
    heads = d // LANE
    qkv = _linear(hb, (wq.astype(BF16), wk.astype(BF16), wv.astype(BF16)), name="diff_qkv")
    extra = tuple(a.reshape(1, -1).astype(F32) for a in (lq1, lk1, lq2, lk2, sub_g))
    return _flash("diff", qkv, qkv, qkv, batch=batch, seq=seq, groups=heads, q_blk0=0, k_blk0=heads,
                  v_blk0=2 * heads, extra=extra, lambda_init=lambda_init)


def _band_mixer(h, hb, w_qkv, rel_bias, wo, g, b, wr, *, batch, seq, alpha):
    d = w_qkv.shape[0]
    qkv = _linear(hb, (w_qkv.astype(BF16),), name="band_qkv")
    o = _band_attention(qkv, _band_bias_table(rel_bias), batch=batch, seq=seq, pairs=d // LANE)
    return _proj_res_ln_route(o, wo.astype(BF16), h, g, b, wr, alpha=alpha, name="band_out")


def _mla_mixer(hb, w_dq, q_g, w_uq, w_dkv, kv_g, w_ukv, *, batch, seq, tm=512):
    n, d = hb.shape
    q_rank = w_dq.shape[1]
    kv_rank = kv_g.shape[-1]
    heads = w_ukv.shape[1] // (2 * CHUNK)
    nope, rope = 64, 2 * ROPE_HALF
    tm = min(tm, seq)
    row = lambda i: (i, 0)
    tab = lambda i: (i % (seq // tm), 0)

    inv_freq = ROPE_THETA ** (-jnp.arange(ROPE_HALF, dtype=F32) / ROPE_HALF)
    ang = jnp.arange(seq).astype(F32)[:, None] * inv_freq[None, :]
    cos, sin = jnp.cos(ang), jnp.sin(ang)
    z = lambda w: jnp.zeros((seq, w), F32)
    ct_q = jnp.concatenate([jnp.ones((seq, nope), F32), cos, cos, z(LANE - nope - rope)], axis=1)
    st_q = jnp.concatenate([z(nope), sin, sin, z(LANE - nope - rope)], axis=1)
    ct_k = jnp.concatenate([cos, cos, z(LANE - rope)], axis=1)
    st_k = jnp.concatenate([sin, sin, z(LANE - rope)], axis=1)

    wq3 = w_uq.reshape(q_rank, heads, nope + rope)
    qn, q1, q2 = wq3[..., :nope], wq3[..., nope:nope + ROPE_HALF], wq3[..., nope + ROPE_HALF:]
    zq = lambda w: jnp.zeros((q_rank, heads, w), w_uq.dtype)
    wa = jnp.concatenate([qn, q1, q2, zq(LANE - nope - rope)], axis=2).reshape(q_rank, heads * LANE)
    wb = jnp.concatenate([zq(nope), -q2, q1, zq(LANE - nope - rope)], axis=2).reshape(q_rank, heads * LANE)

    k1, k2 = w_dkv[:, kv_rank:kv_rank + ROPE_HALF], w_dkv[:, kv_rank + ROPE_HALF:]
    zd = lambda w: jnp.zeros((d, w), w_dkv.dtype)
    wd = jnp.concatenate([w_dkv[:, :kv_rank], k1, k2, zd(LANE - rope), -k2, k1, zd(LANE - rope)], axis=1)

    wkv3 = w_ukv.reshape(kv_rank, heads, nope + CHUNK)
    zk = lambda r, w: jnp.zeros((r, heads, w), w_ukv.dtype)
    k_top = jnp.concatenate([wkv3[..., :nope], zk(kv_rank, LANE - nope)], axis=2).reshape(kv_rank, heads * LANE)
    eye = jnp.broadcast_to(jnp.eye(rope, dtype=w_ukv.dtype)[:, None, :], (rope, heads, rope))
    k_rope = jnp.concatenate([zk(rope, nope), eye, zk(rope, LANE - nope - rope)], axis=2).reshape(rope, heads * LANE)
    v_top = wkv3[..., nope:].reshape(kv_rank, heads * CHUNK)
    wkv = jnp.concatenate([
        jnp.concatenate([k_top, v_top], axis=1),
        jnp.concatenate([k_rope, jnp.zeros((rope, heads * CHUNK), w_ukv.dtype)], axis=1),
        jnp.zeros((LANE - rope, heads * (LANE + CHUNK)), w_ukv.dtype)], axis=0)

    cq = pl.pallas_call(
        _mla_cq_kernel,
        grid=(n // tm,),
        in_specs=[pl.BlockSpec((tm, d), row), _resident((d, q_rank)), _resident((1, q_rank))],
        out_specs=pl.BlockSpec((tm, q_rank), row),
        out_shape=jax.ShapeDtypeStruct((n, q_rank), BF16),
        compiler_params=_cp(1),
        name="mla_cq",
    )(hb, w_dq.astype(BF16), q_g.reshape(1, -1))

    q = pl.pallas_call(
        functools.partial(_mla_q_kernel, scale=float((nope + rope) ** -0.5), heads=heads),
        grid=(n // tm,),
        in_specs=[pl.BlockSpec((tm, q_rank), row), _resident((q_rank, heads * LANE)),
                  _resident((q_rank, heads * LANE)), pl.BlockSpec((tm, LANE), tab), pl.BlockSpec((tm, LANE), tab)],
        out_specs=pl.BlockSpec((tm, heads * LANE), row),
        out_shape=jax.ShapeDtypeStruct((n, heads * LANE), BF16),
        compiler_params=_cp(1),
        name="mla_q",
    )(cq, wa.astype(BF16), wb.astype(BF16), ct_q, st_q)

    ckr = pl.pallas_call(
        functools.partial(_mla_ckv_kernel, rank=kv_rank),
        grid=(n // tm,),
        in_specs=[pl.BlockSpec((tm, d), row), _resident((d, kv_rank + 2 * LANE)), _resident((1, kv_rank)),
                  pl.BlockSpec((tm, LANE), tab), pl.BlockSpec((tm, LANE), tab)],
        out_specs=pl.BlockSpec((tm, kv_rank + LANE), row),
        out_shape=jax.ShapeDtypeStruct((n, kv_rank + LANE), BF16),
        compiler_params=_cp(1),
        name="mla_ckv",
    )(hb, wd.astype(BF16), kv_g.reshape(1, -1), ct_k, st_k)

    kv = _linear(ckr, (wkv.astype(BF16),), name="mla_kv")
    return _flash("mla", q, kv, kv, batch=batch, seq=seq, groups=heads // 2, q_blk0=0, k_blk0=0, v_blk0=heads)


def _sg_mixer(h, hb, w_in, vg, vb, w_s, b_s, w_out, g, b, wr, *, alpha):
    n, d = hb.shape
    width = w_out.shape[0]
    groups = w_s.shape[0]
    tm = min(ROUTE_TILE, n)
    row = lambda i: (i, 0)
    r_specs, r_shapes, r_scratch = _route_outputs(n, tm)
    u, v = pl.pallas_call(
        functools.partial(_sg_in_kernel, nc=min(512, width)),
        grid=(n // tm,),
        in_specs=[pl.BlockSpec((tm, d), row), _resident((d, 2 * width)), _resident((1, width)),
                  _resident((1, width))],
        out_specs=[pl.BlockSpec((tm, width), row), pl.BlockSpec((tm, width), row)],
        out_shape=[jax.ShapeDtypeStruct((n, width), BF16), jax.ShapeDtypeStruct((n, width), BF16)],
        scratch_shapes=[pltpu.VMEM((tm, width), F32)],
        compiler_params=_cp(1),
        name="sg_in",
    )(hb, w_in.astype(BF16), vg.reshape(1, -1), vb.reshape(1, -1))
    bs_full = jnp.repeat(b_s.T.astype(F32), width // groups, axis=1)
    return pl.pallas_call(
        functools.partial(_sg_mix_kernel, alpha=alpha, groups=groups),
        grid=(n // tm,),
        in_specs=[pl.BlockSpec((tm, width), row), pl.BlockSpec((tm, width), row),
                  _resident((groups, GMLP_CHUNK, GMLP_CHUNK)), _resident((GMLP_CHUNK, width)),
                  _resident((width, d)), pl.BlockSpec((tm, d), row), _resident((1, d)), _resident((1, d)),
                  _resident((d, LANE))],
        out_specs=[pl.BlockSpec((tm, d), row)] + r_specs,
        out_shape=[jax.ShapeDtypeStruct((n, d), F32)] + r_shapes,
        scratch_shapes=[pltpu.VMEM((tm, width), BF16)] + r_scratch,
        compiler_params=_cp(1),
        name="sg_mix",
    )(u, v, w_s.astype(F32), bs_full, w_out.astype(BF16), h, g, b, wr)


def kernel(x, ln_mix_g, ln_mix_b, ln_ffn_g, ln_ffn_b, diff_wq, diff_wk, diff_wv, diff_lq1, diff_lk1, diff_lq2, diff_lk2, diff_sub_g, diff_wo, ca_w_qkv, ca_rel_bias, ca_wo, mla_w_dq, mla_q_norm_g, mla_w_uq, mla_w_dkv, mla_kv_norm_g, mla_w_ukv, mla_wo, sg_w_in, sg_v_norm_g, sg_v_norm_b, sg_w_s, sg_b_s, sg_w_out, ffn_w1, ffn_w3, ffn_w2, moe_w_router, moe_w1, moe_w3, moe_w2):
    batch, seq, d = x.shape
    depth = ln_mix_g.shape[0]
    alpha = float((2 * depth) ** 0.25)
    h = x.reshape(batch * seq, d).astype(F32)
    hb = h
    vec = lambda a: a.reshape(1, -1).astype(F32)
    for i in range(depth):
        kind, j = i % 4, i // 4
        mg, mb = vec(ln_mix_g[i]), vec(ln_mix_b[i])
        fg, fb = vec(ln_ffn_g[i]), vec(ln_ffn_b[i])
        if kind in (0, 2):
            if kind == 0:
                a = _diff_mixer(hb, diff_wq[j], diff_wk[j], diff_wv[j], diff_lq1[j], diff_lk1[j], diff_lq2[j],
                                diff_lk2[j], diff_sub_g[j], batch=batch, seq=seq,
                                lambda_init=0.8 - 0.6 * math.exp(-0.3 * i))
                wo = diff_wo[j]
            else:
                a = _mla_mixer(hb, mla_w_dq[j], mla_q_norm_g[j], mla_w_uq[j], mla_w_dkv[j], mla_kv_norm_g[j],
                               mla_w_ukv[j], batch=batch, seq=seq)
                wo = mla_wo[j]
            h, hb = _mix_ffn(a, wo.astype(BF16), mg, mb, ffn_w1[i // 2].astype(BF16), ffn_w3[i // 2].astype(BF16),
                             ffn_w2[i // 2].astype(BF16), h, fg, fb, alpha=alpha)
        else:
            wr = _router_weights(moe_w_router[i // 2])
            if kind == 1:
                h, info, cnt = _band_mixer(h, hb, ca_w_qkv[j], ca_rel_bias[j], ca_wo[j], mg, mb, wr, batch=batch,
                                           seq=seq, alpha=alpha)
            else:
                h, info, cnt = _sg_mixer(h, hb, sg_w_in[j], sg_v_norm_g[j], sg_v_norm_b[j], sg_w_s[j], sg_b_s[j],
                                         sg_w_out[j], mg, mb, wr, alpha=alpha)
            h, hb = _moe(h, info, cnt, moe_w1, moe_w3, moe_w2, fg, fb, layer=i // 2, alpha=alpha)
    return h.reshape(batch, seq, d).astype(x.dtype)
```

```python
import functools
import math

import jax
import jax.numpy as jnp
import numpy as np
from jax import lax
from jax.experimental import pallas as pl
from jax.experimental.pallas import tpu as pltpu

F32 = jnp.float32
BF16 = jnp.bfloat16
I32 = jnp.int32

LANE = 128
VMEM_LIMIT = 56 * 1024 * 1024

CHUNK = 64
N_EXPERTS = 8
LN_EPS = 1e-5
RMS_EPS = 1e-6
NEG_INF = -1e30
ROPE_THETA = 10000.0
CA_LEFT = 8
CA_REL_CLIP = 128
ROPE_HALF = 16


def _cp(n_axes, vmem=VMEM_LIMIT):
    return pltpu.CompilerParams(dimension_semantics=("arbitrary",) * n_axes, vmem_limit_bytes=vmem)


def _resident(shape):
    nd = len(shape)
    return pl.BlockSpec(shape, lambda *_: (0,) * nd, pipeline_mode=pl.Buffered(1))


def _res_ln(h, m, g, b, alpha):
    z = alpha * h + m
    mu = jnp.mean(z, axis=-1, keepdims=True)
    zc = z - mu
    var = jnp.mean(zc * zc, axis=-1, keepdims=True)
    return zc * lax.rsqrt(var + LN_EPS) * g + b


def _rms(x, g, eps):
    ms = jnp.mean(x * x, axis=-1, keepdims=True)
    return x * lax.rsqrt(ms + eps) * g


def _linear_kernel(x_ref, *refs):
    *w_refs, o_ref = refs
    x = x_ref[...].astype(BF16)
    col = 0
    for w_ref in w_refs:
        n = w_ref.shape[1]
        o_ref[:, col:col + n] = jnp.dot(x, w_ref[...], preferred_element_type=F32).astype(o_ref.dtype)
        col += n


def _linear(x, ws, *, tm=512, out_dtype=BF16, name="linear"):
    m, k = x.shape
    n = sum(w.shape[1] for w in ws)
    tm = min(tm, m)
    return pl.pallas_call(
        _linear_kernel,
        grid=(m // tm,),
        in_specs=[pl.BlockSpec((tm, k), lambda i: (i, 0))] + [_resident(w.shape) for w in ws],
        out_specs=pl.BlockSpec((tm, n), lambda i: (i, 0)),
        out_shape=jax.ShapeDtypeStruct((m, n), out_dtype),
        compiler_params=_cp(1),
        name=name,
    )(x, *ws)


ROUTE_TILE = 512


def _route_outputs(m, tm):
    specs = [pl.BlockSpec((tm, LANE), lambda i: (i, 0)), pl.BlockSpec((1, LANE), lambda i: (0, 0))]
    shapes = [jax.ShapeDtypeStruct((m, LANE), F32), jax.ShapeDtypeStruct((1, LANE), F32)]
    scratch = [pltpu.VMEM((1, LANE), F32), pltpu.VMEM((tm, tm), BF16)]
    return specs, shapes, scratch


def _proj_route_kernel(a_ref, w_ref, h_ref, g_ref, b_ref, wr_ref, oh_ref, info_ref, cnt_ref, run_ref, before_ref, *,
                       alpha):
    pl.when(pl.program_id(0) == 0)(lambda: _route_init(run_ref, before_ref))
    m = jnp.dot(a_ref[...], w_ref[...], preferred_element_type=F32)
    y = _res_ln(h_ref[...], m, g_ref[...], b_ref[...], alpha)
    oh_ref[...] = y
    _route(y, wr_ref, run_ref, before_ref, info_ref, cnt_ref)


def _proj_res_ln_route(a, w, h, g, b, wr, *, alpha, name):
    m, k = a.shape
    d = w.shape[1]
    tm = min(ROUTE_TILE, m)
    row = lambda i: (i, 0)
    r_specs, r_shapes, r_scratch = _route_outputs(m, tm)
    return pl.pallas_call(
        functools.partial(_proj_route_kernel, alpha=alpha),
        grid=(m // tm,),
        in_specs=[pl.BlockSpec((tm, k), row), _resident((k, d)), pl.BlockSpec((tm, d), row),
                  _resident((1, d)), _resident((1, d)), _resident((d, LANE))],
        out_specs=[pl.BlockSpec((tm, d), row)] + r_specs,
        out_shape=[jax.ShapeDtypeStruct((m, d), F32)] + r_shapes,
        scratch_shapes=r_scratch,
        compiler_params=_cp(1),
        name=name,
    )(a, w, h, g, b, wr)


def _mix_ffn_kernel(a_ref, wo_ref, mg_ref, mb_ref, w1_ref, w3_ref, w2_ref, h_ref, g_ref, b_ref, oh_ref, ohb_ref,
                    hm_ref, xb_ref, acc_ref, *, alpha, fc):
    mix = jnp.dot(a_ref[...], wo_ref[...], preferred_element_type=F32)
    hm = _res_ln(h_ref[...], mix, mg_ref[...], mb_ref[...], alpha)
    hm_ref[...] = hm
    xb_ref[...] = hm.astype(BF16)
    x = xb_ref[...]
    f = w1_ref.shape[1]
    for c in range(f // fc):
        sl = slice(c * fc, (c + 1) * fc)
        a = jnp.dot(x, w1_ref[:, sl], preferred_element_type=F32)
        b3 = jnp.dot(x, w3_ref[:, sl], preferred_element_type=F32)
        gated = (jax.nn.silu(a) * b3).astype(BF16)
        part = jnp.dot(gated, w2_ref[sl, :], preferred_element_type=F32)
        if c == 0:
            acc_ref[...] = part
        else:
            acc_ref[...] += part
    y = _res_ln(hm_ref[...], acc_ref[...], g_ref[...], b_ref[...], alpha)
    oh_ref[...] = y
    ohb_ref[...] = y.astype(BF16)


def _mix_ffn(a, wo, mg, mb, w1, w3, w2, h, g, b, *, alpha, tm=512, fc=256):
    m, d = h.shape
    k = a.shape[1]
    f = w1.shape[1]
    tm = min(tm, m)
    row = lambda i: (i, 0)
    vecs = [_resident((1, d)), _resident((1, d))]
    return pl.pallas_call(
        functools.partial(_mix_ffn_kernel, alpha=alpha, fc=fc),
        grid=(m // tm,),
        in_specs=[pl.BlockSpec((tm, k), row), _resident((k, d))] + vecs
                 + [_resident((d, f)), _resident((d, f)), _resident((f, d)), pl.BlockSpec((tm, d), row)] + vecs,
        out_specs=[pl.BlockSpec((tm, d), row), pl.BlockSpec((tm, d), row)],
        out_shape=[jax.ShapeDtypeStruct((m, d), F32), jax.ShapeDtypeStruct((m, d), BF16)],
        scratch_shapes=[pltpu.VMEM((tm, d), F32), pltpu.VMEM((tm, d), BF16), pltpu.VMEM((tm, d), F32)],
        compiler_params=_cp(1),
        name="mix_ffn",
    )(a, wo, mg, mb, w1, w3, w2, h, g, b)


ONES_ROWS = 16
ACC_ROWS = LANE + ONES_ROWS


def _vt_ext(v_blk):
    vt = v_blk.astype(F32).T.astype(BF16)
    return jnp.concatenate([vt, jnp.ones((ONES_ROWS, v_blk.shape[0]), BF16)], axis=0)


def _split_heads(x):
    low = lax.broadcasted_iota(I32, x.shape, 1) < CHUNK
    zero = jnp.zeros_like(x)
    return jnp.where(low, x, zero), jnp.where(low, zero, x)


_NT = (((1,), (1,)), ((), ()))


def _flash_kernel(*refs, mode, tq, gs, lambda_init):
    if mode == "diff":
        lq1_ref, lk1_ref, lq2_ref, lk2_ref, subg_ref, q_ref, k_ref, v_ref, o_ref, *scratch = refs
    else:
        q_ref, k_ref, v_ref, o_ref, *scratch = refs
    vt_ref, m_ref, a_ref, bm_ref, acc_ref, s_ref, p_ref = scratch
    qi = pl.program_id(2)
    tk = tq // 2
    nmap = 2 * gs
    lanes = lambda i: slice(i * LANE, (i + 1) * LANE)

    @pl.when(qi == 0)
    def _():
        for j in range(vt_ref.shape[0]):
            for g in range(gs):
                vt_ref[j, g] = _vt_ext(v_ref[j * tk:(j + 1) * tk, lanes(g)])

    qs = []
    for g in range(gs):
        if mode == "diff":
            qs += list(_split_heads(q_ref[:, lanes(g)] * jnp.asarray(CHUNK ** -0.5, BF16)))
        else:
            qs += [q_ref[:, lanes(2 * g)], q_ref[:, lanes(2 * g + 1)]]
    m_ref[...] = jnp.full(m_ref.shape, NEG_INF, F32)
    acc_ref[...] = jnp.zeros(acc_ref.shape, F32)

    def scores(blk):
        start = pl.multiple_of(blk * tk, tk)
        kb = k_ref[pl.ds(start, tk), :]
        kks = [kb[:, lanes(mp // 2 if mode == "diff" else mp)] for mp in range(nmap)]
        return [lax.dot_general(kks[mp], qs[mp], _NT, preferred_element_type=F32) for mp in range(nmap)]

    def values(blk, slot):
        return [jnp.dot(vt_ref[blk, mp // 2], p_ref[slot, mp], preferred_element_type=F32) for mp in range(nmap)]

    def stash(slot, ss):
        for mp in range(nmap):
            s_ref[slot, mp] = ss[mp]
            bm_ref[slot, mp] = jnp.max(ss[mp], axis=0, keepdims=True)

    def softmax(slot, masked):
        for mp in range(nmap):
            s = s_ref[slot, mp]
            if masked:
                kc = (lax.broadcasted_iota(I32, s.shape, 0) + slot * tk) >> 6
                qc = lax.broadcasted_iota(I32, s.shape, 1) >> 6
                s = jnp.where(kc <= qc, s, NEG_INF)
                blk_max = jnp.max(s, axis=0, keepdims=True)
            else:
                blk_max = bm_ref[slot, mp]
            m_old = m_ref[mp]
            m_new = jnp.maximum(m_old, blk_max)
            p_ref[slot, mp] = jnp.exp(s - m_new).astype(BF16)
            a_ref[slot, mp] = jnp.exp(m_old - m_new)
            m_ref[mp] = m_new

    def fold(pv, scale):
        for mp in range(nmap):
            acc_ref[mp] = scale[mp] * acc_ref[mp] + pv[mp]

    def trip(blk, masked, produce):
        scale = [[a_ref[u, mp] for mp in range(nmap)] for u in range(2)]
        pvs = [values(jnp.maximum(blk - 2 + u, 0), u) for u in range(2)]
        nxt = [scores(blk + 2 + u) for u in range(2)] if produce else []
        for u in range(2):
            softmax(u, masked)
        for u in range(2):
            fold(pvs[u], scale[u])
        for u, s in enumerate(nxt):
            stash(u, s)

    for u in range(2):
        stash(u, scores(u))
    p_ref[...] = jnp.zeros(p_ref.shape, BF16)
    a_ref[...] = jnp.ones(a_ref.shape, F32)

    def step(j, carry):
        trip(2 * j, False, True)
        return carry

    lax.fori_loop(0, qi, step, 0)
    trip(2 * qi, True, False)
    for u in range(2):
        fold(values(2 * qi + u, u), [a_ref[u, mp] for mp in range(nmap)])

    for g in range(gs):
        acc_a, acc_b = acc_ref[2 * g], acc_ref[2 * g + 1]
        oa = acc_a[:LANE] / acc_a[LANE:LANE + 1]
        ob = acc_b[:LANE] / acc_b[LANE:LANE + 1]
        if mode == "diff":
            lam = (jnp.exp(jnp.sum(lq1_ref[...] * lk1_ref[...], axis=-1, keepdims=True))
                   - jnp.exp(jnp.sum(lq2_ref[...] * lk2_ref[...], axis=-1, keepdims=True)) + lambda_init)
            o = (oa - lam * ob).T
            o = _rms(o, subg_ref[...], 1e-5) * (1.0 - lambda_init)
        else:
            row = lax.broadcasted_iota(I32, oa.shape, 0)
            o = jnp.where(row < CHUNK, oa, ob).T
        o_ref[:, lanes(g)] = o.astype(o_ref.dtype)


def _flash(mode, q_arr, k_arr, v_arr, *, batch, seq, groups, q_blk0, k_blk0, v_blk0, extra=(), lambda_init=0.0,
           tq=512, gs=2):
    qw = (LANE if mode == "diff" else 2 * LANE) * gs
    tq = min(tq, seq)
    tk = tq // 2
    nq = seq // tq
    n = batch * seq
    nmap = 2 * gs
    assert groups % gs == 0 and q_blk0 % gs == 0 and k_blk0 % gs == 0 and v_blk0 % gs == 0
    qb, kb, vb = q_blk0 // gs, k_blk0 // gs, v_blk0 // gs
    in_specs = [pl.BlockSpec(e.shape, lambda b, g, i: (0, 0)) for e in extra]
    in_specs += [
        pl.BlockSpec((tq, qw), lambda b, g, i: (b * nq + i, qb + g)),
        pl.BlockSpec((seq, qw), lambda b, g, i: (b, kb + g)),
        pl.BlockSpec((seq, gs * LANE), lambda b, g, i: (b, vb + g)),
    ]
    return pl.pallas_call(
        functools.partial(_flash_kernel, mode=mode, tq=tq, gs=gs, lambda_init=lambda_init),
        grid=(batch, groups // gs, nq),
        in_specs=in_specs,
        out_specs=pl.BlockSpec((tq, gs * LANE), lambda b, g, i: (b * nq + i, g)),
        out_shape=jax.ShapeDtypeStruct((n, groups * LANE), BF16),
        scratch_shapes=[pltpu.VMEM((seq // tk, gs, ACC_ROWS, tk), BF16), pltpu.VMEM((nmap, 1, tq), F32),
                        pltpu.VMEM((2, nmap, 1, tq), F32), pltpu.VMEM((2, nmap, 1, tq), F32),
                        pltpu.VMEM((nmap, ACC_ROWS, tq), F32),
                        pltpu.VMEM((2, nmap, tk, tq), F32), pltpu.VMEM((2, nmap, tk, tq), BF16)],
        compiler_params=_cp(3),
        name="flash_" + mode,
    )(*extra, q_arr, k_arr, v_arr)


BAND_TQ = 4 * CHUNK
BAND_W = (CA_LEFT + 4) * CHUNK
BAND_PAD = CA_LEFT * CHUNK
BAND_NW = BAND_W // BAND_TQ
BAND_EXT = 1024


def _band_kernel(q_ref, k_ref, v_ref, ext_ref, o_ref, kpad_ref, vt_ref, s_ref, bm_ref, bias_ref, *, seq, gp):
    t = pl.program_id(2)
    nt = pl.num_programs(2)
    npad = BAND_PAD // BAND_TQ

    nh = 2 * gp
    lanes = lambda i: slice(i * LANE, (i + 1) * LANE)

    def produce(tile, slot, front):
        qr = q_ref[pl.ds(pl.multiple_of(tile * BAND_TQ, BAND_TQ), BAND_TQ), :]
        qhs = []
        for g in range(gp):
            qhs += list(_split_heads(qr[:, lanes(g)] * jnp.asarray(CHUNK ** -0.5, BF16)))
        tops = [None] * nh
        for u in range(BAND_NW):
            start = pl.multiple_of((tile + u) * BAND_TQ, BAND_TQ)
            kb = kpad_ref[pl.ds(start, BAND_TQ), :]
            for hh in range(nh):
                s = lax.dot_general(kb[:, lanes(hh // 2)], qhs[hh], _NT, preferred_element_type=F32)
                s = s + bias_ref[hh, u * BAND_TQ:(u + 1) * BAND_TQ, :]
                if front:
                    invalid = lax.broadcasted_iota(I32, s.shape, 0) + (tile + u) * BAND_TQ < BAND_PAD
                    s = jnp.where(invalid, NEG_INF, s)
                s_ref[slot, hh, u] = s
                top = jnp.max(s, axis=0, keepdims=True)
                tops[hh] = top if tops[hh] is None else jnp.maximum(tops[hh], top)
        for hh in range(nh):
            bm_ref[slot, hh] = tops[hh]

    @pl.when((t == 0) & (pl.program_id(1) == 0))
    def _():
        ci = lax.broadcasted_iota(I32, (BAND_W, BAND_TQ), 1) >> 6
        cj = lax.broadcasted_iota(I32, (BAND_W, BAND_TQ), 0) >> 6
        vis = (cj >= ci) & (cj <= ci + CA_LEFT)
        for hh in range(nh):
            rows = jnp.broadcast_to(ext_ref[hh // 2, hh % 2], (BAND_W, BAND_EXT))
            skew = pltpu.roll(rows, BAND_EXT - (BAND_W - 1), 1, stride=1, stride_axis=0)
            bias_ref[hh] = jnp.where(vis, skew[:, :BAND_TQ], NEG_INF)

    @pl.when(t == 0)
    def _():
        kpad_ref[0:BAND_PAD, :] = jnp.zeros((BAND_PAD, gp * LANE), BF16)
        kpad_ref[BAND_PAD:BAND_PAD + seq, :] = k_ref[...]
        for j in range(npad):
            vt_ref[j] = jnp.zeros((gp, ACC_ROWS, BAND_TQ), BF16)
        for j in range(seq // BAND_TQ):
            for g in range(gp):
                vt_ref[npad + j, g] = _vt_ext(v_ref[j * BAND_TQ:(j + 1) * BAND_TQ, lanes(g)])
        produce(0, 0, True)

    cur = t & 1
    nxt_tile = jnp.minimum(t + 1, nt - 1)

    def consume():
        outs = []
        for hh in range(nh):
            m = bm_ref[cur, hh]
            acc = None
            for u in range(BAND_NW):
                p = jnp.exp(s_ref[cur, hh, u] - m).astype(BF16)
                pv = jnp.dot(vt_ref[t + u, hh // 2], p, preferred_element_type=F32)
                acc = pv if acc is None else acc + pv
            outs.append(acc[:LANE] / acc[LANE:LANE + 1])
        row = lax.broadcasted_iota(I32, outs[0].shape, 0)
        rows = pl.ds(pl.multiple_of(t * BAND_TQ, BAND_TQ), BAND_TQ)
        for g in range(gp):
            o_ref[rows, lanes(g)] = jnp.where(row < CHUNK, outs[2 * g], outs[2 * g + 1]).T.astype(o_ref.dtype)

    def run(front):
        consume()
        produce(nxt_tile, 1 - cur, front)

    pl.when(t + 1 < npad)(lambda: run(True))
    pl.when(t + 1 >= npad)(lambda: run(False))


def _band_attention(qkv, ext, *, batch, seq, pairs, gp=4):
    n = batch * seq
    nt = seq // BAND_TQ
    assert pairs % gp == 0
    steps, w, nh = pairs // gp, gp * LANE, 2 * gp
    return pl.pallas_call(
        functools.partial(_band_kernel, seq=seq, gp=gp),
        grid=(steps, batch, nt),
        in_specs=[
            pl.BlockSpec((seq, w), lambda g, b, t: (b, g)),
            pl.BlockSpec((seq, w), lambda g, b, t: (b, steps + g)),
            pl.BlockSpec((seq, w), lambda g, b, t: (b, 2 * steps + g)),
            pl.BlockSpec((gp, 2, 1, BAND_EXT), lambda g, b, t: (g, 0, 0, 0)),
        ],
        out_specs=pl.BlockSpec((seq, w), lambda g, b, t: (b, g)),
        out_shape=jax.ShapeDtypeStruct((n, pairs * LANE), BF16),
        scratch_shapes=[pltpu.VMEM((seq + BAND_PAD, w), BF16),
                        pltpu.VMEM(((seq + BAND_PAD) // BAND_TQ, gp, ACC_ROWS, BAND_TQ), BF16),
                        pltpu.VMEM((2, nh, BAND_NW, BAND_TQ, BAND_TQ), F32),
                        pltpu.VMEM((2, nh, 1, BAND_TQ), F32),
                        pltpu.VMEM((nh, BAND_W, BAND_TQ), F32)],
        compiler_params=_cp(3),
        name="band_attention",
    )(qkv, qkv, qkv, ext)


def _band_bias_table(rel_bias):
    heads = rel_bias.shape[0]
    rel = np.minimum(np.arange(BAND_EXT), BAND_TQ + BAND_W - 2) - (BAND_W - 1) + BAND_PAD
    ext = rel_bias.astype(F32)[:, np.clip(rel, -CA_REL_CLIP, CA_REL_CLIP) + CA_REL_CLIP]
    return ext.reshape(heads // 2, 2, 1, BAND_EXT)


def _mla_cq_kernel(x_ref, w_ref, g_ref, o_ref):
    c = jnp.dot(x_ref[...], w_ref[...], preferred_element_type=F32)
    o_ref[...] = _rms(c, g_ref[...], RMS_EPS).astype(o_ref.dtype)


def _mla_q_kernel(c_ref, wa_ref, wb_ref, ct_ref, st_ref, o_ref, *, scale, heads):
    c = c_ref[...]
    a = jnp.dot(c, wa_ref[...], preferred_element_type=F32)
    b = jnp.dot(c, wb_ref[...], preferred_element_type=F32)
    ct = ct_ref[...] * scale
    st = st_ref[...] * scale
    for h in range(heads):
        sl = slice(h * LANE, (h + 1) * LANE)
        o_ref[:, sl] = (a[:, sl] * ct + b[:, sl] * st).astype(o_ref.dtype)


def _mla_ckv_kernel(x_ref, w_ref, g_ref, ct_ref, st_ref, o_ref, *, rank):
    y = jnp.dot(x_ref[...], w_ref[...], preferred_element_type=F32)
    o_ref[:, :rank] = _rms(y[:, :rank], g_ref[...], RMS_EPS).astype(o_ref.dtype)
    kr = y[:, rank:rank + LANE] * ct_ref[...] + y[:, rank + LANE:rank + 2 * LANE] * st_ref[...]
    o_ref[:, rank:] = kr.astype(o_ref.dtype)


GMLP_CHUNK = 128


def _sg_in_kernel(x_ref, w_ref, g_ref, b_ref, u_ref, v_ref, vbuf_ref, *, nc):
    width = u_ref.shape[1]
    x = x_ref[...]
    for c in range(0, 2 * width, nc):
        hc = jax.nn.gelu(jnp.dot(x, w_ref[:, c:c + nc], preferred_element_type=F32))
        if c < width:
            u_ref[:, c:c + nc] = hc.astype(u_ref.dtype)
        else:
            vbuf_ref[:, c - width:c - width + nc] = hc
    v = vbuf_ref[...]
    mu = jnp.mean(v, axis=-1, keepdims=True)
    vc = v - mu
    var = jnp.mean(vc * vc, axis=-1, keepdims=True)
    v_ref[...] = (vc * lax.rsqrt(var + LN_EPS) * g_ref[...] + b_ref[...]).astype(v_ref.dtype)


def _sg_mix_kernel(u_ref, v_ref, ws_ref, bs_ref, wo_ref, h_ref, g_ref, b_ref, wr_ref, oh_ref, info_ref, cnt_ref,
                   gated_ref, run_ref, before_ref, *, alpha, groups):
    tm = u_ref.shape[0]
    pl.when(pl.program_id(0) == 0)(lambda: _route_init(run_ref, before_ref))
    r = lax.broadcasted_iota(I32, (GMLP_CHUNK, GMLP_CHUNK), 0) >> 6
    c = lax.broadcasted_iota(I32, (GMLP_CHUNK, GMLP_CHUNK), 1) >> 6
    vis = c <= r
    for gi in range(groups):
        w = jnp.where(vis, ws_ref[gi], 0.0).astype(BF16)
        cs = slice(gi * LANE, (gi + 1) * LANE)
        for ch in range(tm // GMLP_CHUNK):
            rs = slice(ch * GMLP_CHUNK, (ch + 1) * GMLP_CHUNK)
            mixed = jnp.dot(w, v_ref[rs, cs], preferred_element_type=F32) + bs_ref[:, cs]
            gated_ref[rs, cs] = (u_ref[rs, cs].astype(F32) * mixed).astype(BF16)
    m = jnp.dot(gated_ref[...], wo_ref[...], preferred_element_type=F32)
    y = _res_ln(h_ref[...], m, g_ref[...], b_ref[...], alpha)
    oh_ref[...] = y
    _route(y, wr_ref, run_ref, before_ref, info_ref, cnt_ref)


INFO_IDX, INFO_RANK, INFO_GATE = 0, 2, 4
SUBLANES = 8


def _route_init(run_ref, before_ref):
    tr = before_ref.shape[0]
    run_ref[...] = jnp.zeros_like(run_ref)
    rr = lax.broadcasted_iota(I32, (tr, tr), 0)
    cc = lax.broadcasted_iota(I32, (tr, tr), 1)
    before_ref[...] = jnp.where(cc < rr, 1.0, 0.0).astype(BF16)


def _route(x, w_ref, run_ref, before_ref, info_ref, cnt_ref):
    tr = x.shape[0]
    x1 = x.astype(BF16)
    r1 = x - x1.astype(F32)
    x2 = r1.astype(BF16)
    x3 = (r1 - x2.astype(F32)).astype(BF16)
    w = w_ref[...]
    lane = lax.broadcasted_iota(I32, (tr, LANE), 1)
    t = (jnp.dot(x1, w, preferred_element_type=F32)
         + jnp.where(lane < 2 * N_EXPERTS, jnp.dot(x2, w, preferred_element_type=F32), 0.0)
         + jnp.where(lane < N_EXPERTS, jnp.dot(x3, w, preferred_element_type=F32), 0.0))
    logits = t + pltpu.roll(t, LANE - N_EXPERTS, 1) + pltpu.roll(t, LANE - 2 * N_EXPERTS, 1)
    big = jnp.float32(3e38)
    logits = jnp.where(lane < N_EXPERTS, logits, -big)

    lane_f = lane.astype(F32)
    v1 = jnp.max(logits, axis=-1, keepdims=True)
    i1 = jnp.min(jnp.where(logits == v1, lane_f, float(LANE)), axis=-1, keepdims=True)
    rest = jnp.where(lane_f == i1, -big, logits)
    v2 = jnp.max(rest, axis=-1, keepdims=True)
    i2 = jnp.min(jnp.where(rest == v2, lane_f, float(LANE)), axis=-1, keepdims=True)
    e = jnp.exp(v2 - v1)
    g1 = 1.0 / (1.0 + e)
    g2 = e / (1.0 + e)

    oh1 = jnp.where(lane_f == i1, 1.0, 0.0)
    oh2 = jnp.where(lane_f == i2, 1.0, 0.0)
    oh = oh1 + oh2
    prior = jnp.dot(before_ref[...], oh.astype(BF16), preferred_element_type=F32) + run_ref[...]
    rank1 = jnp.sum(prior * oh1, axis=-1, keepdims=True)
    rank2 = jnp.sum(prior * oh2, axis=-1, keepdims=True)
    run_ref[...] += jnp.sum(oh, axis=0, keepdims=True)
    cnt_ref[...] = run_ref[...]

    info = jnp.where(lane == INFO_IDX, i1, 0.0)
    info = jnp.where(lane == INFO_IDX + 1, i2, info)
    info = jnp.where(lane == INFO_RANK, rank1, info)
    info = jnp.where(lane == INFO_RANK + 1, rank2, info)
    info = jnp.where(lane == INFO_GATE, g1, info)
    info = jnp.where(lane == INFO_GATE + 1, g2, info)
    info_ref[...] = info


def _dispatch_kernel(zoff_ref, pos_ref, h_ref, xs_ref, zero_ref, sem, zsem):
    td = h_ref.shape[0]

    @pl.when(pl.program_id(0) == 0)
    def _():
        tile = zero_ref.shape[0]
        zero_ref[...] = jnp.zeros_like(zero_ref)

        def fill(k):
            start = pl.multiple_of(jnp.maximum(zoff_ref[k], 0), tile)
            return pltpu.make_async_copy(zero_ref, xs_ref.at[pl.ds(start, tile), :], zsem)

        for k in range(zoff_ref.shape[0]):
            pl.when(zoff_ref[k] >= 0)(lambda k=k: fill(k).start())
        for k in range(zoff_ref.shape[0]):
            pl.when(zoff_ref[k] >= 0)(lambda k=k: fill(k).wait())

    def start(g, carry):
        base = pl.multiple_of(g * SUBLANES, SUBLANES)
        for s in range(SUBLANES):
            for k in range(2):
                p = pos_ref[0, 0, 2 * (base + s) + k]
                pltpu.make_async_copy(h_ref.at[pl.ds(base + s, 1), :], xs_ref.at[pl.ds(p, 1), :],
                                      sem).start(priority=k)
        return carry

    lax.fori_loop(0, td // SUBLANES, start, 0)
    for _ in range(2):
        pltpu.make_async_copy(h_ref, xs_ref.at[pl.ds(0, td), :], sem).wait()


def _experts_kernel(te_ref, nu_ref, x_ref, w1_ref, w3_ref, w2_ref, o_ref, xb_ref, *, fc):
    i = pl.program_id(0)
    j = pl.program_id(1)
    del te_ref

    @pl.when(j == 0)
    def _():
        o_ref[...] = jnp.zeros_like(o_ref)
        xb_ref[...] = x_ref[...].astype(BF16)

    @pl.when(i < nu_ref[0])
    def _():
        x = xb_ref[...]
        tf = w1_ref.shape[1]
        gated = []
        for c in range(tf // fc):
            sl = slice(c * fc, (c + 1) * fc)
            a = jnp.dot(x, w1_ref[:, sl].astype(BF16), preferred_element_type=F32)
            b3 = jnp.dot(x, w3_ref[:, sl].astype(BF16), preferred_element_type=F32)
            gated.append((jax.nn.silu(a) * b3).astype(BF16))
        o_ref[...] += jnp.dot(jnp.concatenate(gated, axis=1), w2_ref[...].astype(BF16),
                              preferred_element_type=F32)


def _combine_kernel(pos_ref, posn_ref, info_ref, h_ref, g_ref, b_ref, ys_ref, oh_ref, ohb_ref, buf_ref, sem, *,
                    alpha):
    i = pl.program_id(0)
    tc = h_ref.shape[0]

    def gather(p_ref, slot):
        def start(g, carry):
            base = pl.multiple_of(g * SUBLANES, SUBLANES)
            for s in range(SUBLANES):
                for k in range(2):
                    p = p_ref[0, 0, 2 * (base + s) + k]
                    pltpu.make_async_copy(ys_ref.at[pl.ds(p, 1), :], buf_ref.at[slot, k, pl.ds(base + s, 1), :],
                                          sem.at[slot]).start(priority=k)
            return carry

        lax.fori_loop(0, tc // SUBLANES, start, 0)

    @pl.when(i == 0)
    def _():
        gather(pos_ref, 0)

    @pl.when(i + 1 < pl.num_programs(0))
    def _():
        gather(posn_ref, (i + 1) & 1)

    cur = i & 1
    for k in range(2):
        pltpu.make_async_copy(ys_ref.at[pl.ds(0, tc), :], buf_ref.at[cur, k], sem.at[cur]).wait()
    info = info_ref[...]
    g1 = info[:, INFO_GATE:INFO_GATE + 1]
    g2 = info[:, INFO_GATE + 1:INFO_GATE + 2]
    m = g1 * buf_ref[cur, 0] + g2 * buf_ref[cur, 1]
    y = _res_ln(h_ref[...], m, g_ref[...], b_ref[...], alpha)
    oh_ref[...] = y
    ohb_ref[...] = y.astype(BF16)


def _split3(w):
    hi = w.astype(BF16)
    r = w - hi.astype(F32)
    mid = r.astype(BF16)
    lo = (r - mid.astype(F32)).astype(BF16)
    return hi, mid, lo


def _router_weights(w_router):
    hi, mid, lo = _split3(w_router.astype(F32))
    pad = jnp.zeros((w_router.shape[0], LANE - 3 * N_EXPERTS), BF16)
    return jnp.concatenate([hi, mid, lo, pad], axis=1)


def _moe(h, info, cnt, w1, w3, w2, g, b, *, layer, alpha, tm=1024, tf=512, fc=256, td=1024):
    n, d = h.shape
    f = w1.shape[3]
    tm = min(tm, n)
    td = min(td, n)
    tf = min(tf, f)
    row = lambda i: (i, 0)

    idx = info[:, INFO_IDX:INFO_IDX + 2].astype(I32)
    rank = info[:, INFO_RANK:INFO_RANK + 2].astype(I32)
    counts = cnt[0, :N_EXPERTS].astype(I32)
    padded = ((counts + tm - 1) // tm) * tm
    ends = jnp.cumsum(padded)
    starts = ends - padded
    pos = starts[idx] + rank
    n_tiles = (2 * n) // tm + N_EXPERTS
    n_used = (ends[-1] // tm).astype(I32)
    tile_start = jnp.arange(n_tiles, dtype=I32) * tm
    tile_e = jnp.sum(tile_start[:, None] >= ends[None, :], axis=1).astype(I32)
    tile_e = jnp.minimum(tile_e, tile_e[jnp.maximum(n_used - 1, 0)])
    rows = n_tiles * tm
    pos_blocks = pos.reshape(n // td, 1, 2 * td)

    group_fill = jnp.where(padded > 0, ends - tm, -1)
    tail = n_used + jnp.arange(N_EXPERTS, dtype=I32)
    tail_fill = jnp.where(tail < n_tiles, tail * tm, -1)
    zero_tiles = jnp.concatenate([group_fill, tail_fill]).astype(I32)
    xs = pl.pallas_call(
        _dispatch_kernel,
        grid_spec=pltpu.PrefetchScalarGridSpec(
            num_scalar_prefetch=1,
            grid=(n // td,),
            in_specs=[pl.BlockSpec((1, 1, 2 * td), lambda i, z: (i, 0, 0), memory_space=pltpu.SMEM),
                      pl.BlockSpec((td, d), lambda i, z: (i, 0))],
            out_specs=pl.BlockSpec(memory_space=pl.ANY),
            scratch_shapes=[pltpu.VMEM((tm, d), F32), pltpu.SemaphoreType.DMA(()), pltpu.SemaphoreType.DMA(())],
        ),
        out_shape=jax.ShapeDtypeStruct((rows, d), F32),
        compiler_params=_cp(1),
        name="moe_dispatch",
    )(zero_tiles, pos_blocks, h)

    nj = f // tf

    def x_map(i, j, te, nu):
        return (jnp.minimum(i, nu[0] - 1), 0)

    def w13_map(i, j, te, nu):
        return (layer, te[i], 0, jnp.where(i < nu[0], j, nj - 1))

    def w2_map(i, j, te, nu):
        return (layer, te[i], jnp.where(i < nu[0], j, nj - 1), 0)

    ys = pl.pallas_call(
        functools.partial(_experts_kernel, fc=fc),
        grid_spec=pltpu.PrefetchScalarGridSpec(
            num_scalar_prefetch=2,
            grid=(n_tiles, nj),
            in_specs=[pl.BlockSpec((tm, d), x_map),
                      pl.BlockSpec((None, None, d, tf), w13_map),
                      pl.BlockSpec((None, None, d, tf), w13_map),
                      pl.BlockSpec((None, None, tf, d), w2_map)],
            out_specs=pl.BlockSpec((tm, d), lambda i, j, te, nu: (i, 0)),
            scratch_shapes=[pltpu.VMEM((tm, d), BF16)],
        ),
        out_shape=jax.ShapeDtypeStruct((rows, d), F32),
        compiler_params=_cp(2),
        name="moe_experts",
    )(tile_e, n_used.reshape(1), xs, w1, w3, w2)

    tc = td
    last = n // tc - 1
    return pl.pallas_call(
        functools.partial(_combine_kernel, alpha=alpha),
        grid=(n // tc,),
        in_specs=[pl.BlockSpec((1, 1, 2 * tc), lambda i: (i, 0, 0), memory_space=pltpu.SMEM),
                  pl.BlockSpec((1, 1, 2 * tc), lambda i: (jnp.minimum(i + 1, last), 0, 0), memory_space=pltpu.SMEM),
                  pl.BlockSpec((tc, LANE), row),
                  pl.BlockSpec((tc, d), row),
                  _resident((1, d)), _resident((1, d)),
                  pl.BlockSpec(memory_space=pl.ANY)],
        out_specs=[pl.BlockSpec((tc, d), row), pl.BlockSpec((tc, d), row)],
        out_shape=[jax.ShapeDtypeStruct((n, d), F32), jax.ShapeDtypeStruct((n, d), BF16)],
        scratch_shapes=[pltpu.VMEM((2, 2, tc, d), F32), pltpu.SemaphoreType.DMA((2,))],
        compiler_params=_cp(1),
        name="moe_combine",
    )(pos_blocks, pos_blocks, info, h, g, b, ys)


def _diff_mixer(hb, wq, wk, wv, lq1, lk1, lq2, lk2, sub_g, *, batch, seq, lambda_init):
    d = wq.shape[0]
    heads = d // LANE
    qkv = _linear(hb, (wq.astype(BF16), wk.astype(BF16), wv.astype(BF16)), name="diff_qkv")
    extra = tuple(a.reshape(1, -1).astype(F32) for a in (lq1, lk1, lq2, lk2, sub_g))
    return _flash("diff", qkv, qkv, qkv, batch=batch, seq=seq, groups=heads, q_blk0=0, k_blk0=heads,
                  v_blk0=2 * heads, extra=extra, lambda_init=lambda_init)


def _band_mixer(h, hb, w_qkv, rel_bias, wo, g, b, wr, *, batch, seq, alpha):
    d = w_qkv.shape[0]
    qkv = _linear(hb, (w_qkv.astype(BF16),), name="band_qkv")
    o = _band_attention(qkv, _band_bias_table(rel_bias), batch=batch, seq=seq, pairs=d // LANE)
    return _proj_res_ln_route(o, wo.astype(BF16), h, g, b, wr, alpha=alpha, name="band_out")


def _mla_mixer(hb, w_dq, q_g, w_uq, w_dkv, kv_g, w_ukv, *, batch, seq, tm=512):
    n, d = hb.shape
    q_rank = w_dq.shape[1]
    kv_rank = kv_g.shape[-1]
    heads = w_ukv.shape[1] // (2 * CHUNK)
    nope, rope = 64, 2 * ROPE_HALF
    tm = min(tm, seq)
    row = lambda i: (i, 0)
    tab = lambda i: (i % (seq // tm), 0)

    inv_freq = ROPE_THETA ** (-jnp.arange(ROPE_HALF, dtype=F32) / ROPE_HALF)
    ang = jnp.arange(seq).astype(F32)[:, None] * inv_freq[None, :]
    cos, sin = jnp.cos(ang), jnp.sin(ang)
    z = lambda w: jnp.zeros((seq, w), F32)
    ct_q = jnp.concatenate([jnp.ones((seq, nope), F32), cos, cos, z(LANE - nope - rope)], axis=1)
    st_q = jnp.concatenate([z(nope), sin, sin, z(LANE - nope - rope)], axis=1)
    ct_k = jnp.concatenate([cos, cos, z(LANE - rope)], axis=1)
    st_k = jnp.concatenate([sin, sin, z(LANE - rope)], axis=1)

    wq3 = w_uq.reshape(q_rank, heads, nope + rope)
    qn, q1, q2 = wq3[..., :nope], wq3[..., nope:nope + ROPE_HALF], wq3[..., nope + ROPE_HALF:]
    zq = lambda w: jnp.zeros((q_rank, heads, w), w_uq.dtype)
    wa = jnp.concatenate([qn, q1, q2, zq(LANE - nope - rope)], axis=2).reshape(q_rank, heads * LANE)
    wb = jnp.concatenate([zq(nope), -q2, q1, zq(LANE - nope - rope)], axis=2).reshape(q_rank, heads * LANE)

    k1, k2 = w_dkv[:, kv_rank:kv_rank + ROPE_HALF], w_dkv[:, kv_rank + ROPE_HALF:]
    zd = lambda w: jnp.zeros((d, w), w_dkv.dtype)
    wd = jnp.concatenate([w_dkv[:, :kv_rank], k1, k2, zd(LANE - rope), -k2, k1, zd(LANE - rope)], axis=1)

    wkv3 = w_ukv.reshape(kv_rank, heads, nope + CHUNK)
    zk = lambda r, w: jnp.zeros((r, heads, w), w_ukv.dtype)
    k_top = jnp.concatenate([wkv3[..., :nope], zk(kv_rank, LANE - nope)], axis=2).reshape(kv_rank, heads * LANE)
    eye = jnp.broadcast_to(jnp.eye(rope, dtype=w_ukv.dtype)[:, None, :], (rope, heads, rope))
    k_rope = jnp.concatenate([zk(rope, nope), eye, zk(rope, LANE - nope - rope)], axis=2).reshape(rope, heads * LANE)
    v_top = wkv3[..., nope:].reshape(kv_rank, heads * CHUNK)
    wkv = jnp.concatenate([
        jnp.concatenate([k_top, v_top], axis=1),
        jnp.concatenate([k_rope, jnp.zeros((rope, heads * CHUNK), w_ukv.dtype)], axis=1),
        jnp.zeros((LANE - rope, heads * (LANE + CHUNK)), w_ukv.dtype)], axis=0)

    cq = pl.pallas_call(
        _mla_cq_kernel,
        grid=(n // tm,),
        in_specs=[pl.BlockSpec((tm, d), row), _resident((d, q_rank)), _resident((1, q_rank))],
        out_specs=pl.BlockSpec((tm, q_rank), row),
        out_shape=jax.ShapeDtypeStruct((n, q_rank), BF16),
        compiler_params=_cp(1),
        name="mla_cq",
    )(hb, w_dq.astype(BF16), q_g.reshape(1, -1))

    q = pl.pallas_call(
        functools.partial(_mla_q_kernel, scale=float((nope + rope) ** -0.5), heads=heads),
        grid=(n // tm,),
        in_specs=[pl.BlockSpec((tm, q_rank), row), _resident((q_rank, heads * LANE)),
                  _resident((q_rank, heads * LANE)), pl.BlockSpec((tm, LANE), tab), pl.BlockSpec((tm, LANE), tab)],
        out_specs=pl.BlockSpec((tm, heads * LANE), row),
        out_shape=jax.ShapeDtypeStruct((n, heads * LANE), BF16),
        compiler_params=_cp(1),
        name="mla_q",
    )(cq, wa.astype(BF16), wb.astype(BF16), ct_q, st_q)

    ckr = pl.pallas_call(
        functools.partial(_mla_ckv_kernel, rank=kv_rank),
        grid=(n // tm,),
        in_specs=[pl.BlockSpec((tm, d), row), _resident((d, kv_rank + 2 * LANE)), _resident((1, kv_rank)),
                  pl.BlockSpec((tm, LANE), tab), pl.BlockSpec((tm, LANE), tab)],
        out_specs=pl.BlockSpec((tm, kv_rank + LANE), row),
        out_shape=jax.ShapeDtypeStruct((n, kv_rank + LANE), BF16),
        compiler_params=_cp(1),
        name="mla_ckv",
    )(hb, wd.astype(BF16), kv_g.reshape(1, -1), ct_k, st_k)

    kv = _linear(ckr, (wkv.astype(BF16),), name="mla_kv")
    return _flash("mla", q, kv, kv, batch=batch, seq=seq, groups=heads // 2, q_blk0=0, k_blk0=0, v_blk0=heads)


def _sg_mixer(h, hb, w_in, vg, vb, w_s, b_s, w_out, g, b, wr, *, alpha):
    n, d = hb.shape
    width = w_out.shape[0]
    groups = w_s.shape[0]
    tm = min(ROUTE_TILE, n)
    row = lambda i: (i, 0)
    r_specs, r_shapes, r_scratch = _route_outputs(n, tm)
    u, v = pl.pallas_call(
        functools.partial(_sg_in_kernel, nc=min(512, width)),
        grid=(n // tm,),
        in_specs=[pl.BlockSpec((tm, d), row), _resident((d, 2 * width)), _resident((1, width)),
                  _resident((1, width))],
        out_specs=[pl.BlockSpec((tm, width), row), pl.BlockSpec((tm, width), row)],
        out_shape=[jax.ShapeDtypeStruct((n, width), BF16), jax.ShapeDtypeStruct((n, width), BF16)],
        scratch_shapes=[pltpu.VMEM((tm, width), F32)],
        compiler_params=_cp(1),
        name="sg_in",
    )(hb, w_in.astype(BF16), vg.reshape(1, -1), vb.reshape(1, -1))
    bs_full = jnp.repeat(b_s.T.astype(F32), width // groups, axis=1)
    return pl.pallas_call(
        functools.partial(_sg_mix_kernel, alpha=alpha, groups=groups),
        grid=(n // tm,),
        in_specs=[pl.BlockSpec((tm, width), row), pl.BlockSpec((tm, width), row),
                  _resident((groups, GMLP_CHUNK, GMLP_CHUNK)), _resident((GMLP_CHUNK, width)),
                  _resident((width, d)), pl.BlockSpec((tm, d), row), _resident((1, d)), _resident((1, d)),
                  _resident((d, LANE))],
        out_specs=[pl.BlockSpec((tm, d), row)] + r_specs,
        out_shape=[jax.ShapeDtypeStruct((n, d), F32)] + r_shapes,
        scratch_shapes=[pltpu.VMEM((tm, width), BF16)] + r_scratch,
        compiler_params=_cp(1),
        name="sg_mix",
    )(u, v, w_s.astype(F32), bs_full, w_out.astype(BF16), h, g, b, wr)


def kernel(x, ln_mix_g, ln_mix_b, ln_ffn_g, ln_ffn_b, diff_wq, diff_wk, diff_wv, diff_lq1, diff_lk1, diff_lq2, diff_lk2, diff_sub_g, diff_wo, ca_w_qkv, ca_rel_bias, ca_wo, mla_w_dq, mla_q_norm_g, mla_w_uq, mla_w_dkv, mla_kv_norm_g, mla_w_ukv, mla_wo, sg_w_in, sg_v_norm_g, sg_v_norm_b, sg_w_s, sg_b_s, sg_w_out, ffn_w1, ffn_w3, ffn_w2, moe_w_router, moe_w1, moe_w3, moe_w2):
    batch, seq, d = x.shape
    depth = ln_mix_g.shape[0]
    alpha = float((2 * depth) ** 0.25)
    h = x.reshape(batch * seq, d).astype(F32)
    hb = h
    vec = lambda a: a.reshape(1, -1).astype(F32)
    for i in range(depth):
        kind, j = i % 4, i // 4
        mg, mb = vec(ln_mix_g[i]), vec(ln_mix_b[i])
        fg, fb = vec(ln_ffn_g[i]), vec(ln_ffn_b[i])
        if kind in (0, 2):
            if kind == 0:
                a = _diff_mixer(hb, diff_wq[j], diff_wk[j], diff_wv[j], diff_lq1[j], diff_lk1[j], diff_lq2[j],
                                diff_lk2[j], diff_sub_g[j], batch=batch, seq=seq,
                                lambda_init=0.8 - 0.6 * math.exp(-0.3 * i))
                wo = diff_wo[j]
            else:
                a = _mla_mixer(hb, mla_w_dq[j], mla_q_norm_g[j], mla_w_uq[j], mla_w_dkv[j], mla_kv_norm_g[j],
                               mla_w_ukv[j], batch=batch, seq=seq)
                wo = mla_wo[j]
            h, hb = _mix_ffn(a, wo.astype(BF16), mg, mb, ffn_w1[i // 2].astype(BF16), ffn_w3[i // 2].astype(BF16),
                             ffn_w2[i // 2].astype(BF16), h, fg, fb, alpha=alpha)
        else:
            wr = _router_weights(moe_w_router[i // 2])
            if kind == 1:
                h, info, cnt = _band_mixer(h, hb, ca_w_qkv[j], ca_rel_bias[j], ca_wo[j], mg, mb, wr, batch=batch,
                                           seq=seq, alpha=alpha)
            else:
                h, info, cnt = _sg_mixer(h, hb, sg_w_in[j], sg_v_norm_g[j], sg_v_norm_b[j], sg_w_s[j], sg_b_s[j],
                                         sg_w_out[j], mg, mb, wr, alpha=alpha)
            h, hb = _moe(h, info, cnt, moe_w1, moe_w3, moe_w2, fg, fb, layer=i // 2, alpha=alpha)
    return h.reshape(batch, seq, d).astype(x.dtype)
```
